```python
import math
import jax
import jax.numpy as jnp
from jax import lax
import numpy as np

D_MODEL = 2048
BATCH = 16
SEQ = 2048
DEPTH = 4

GRID_W = 64
CTX_LEN = 256

A_HEADS = 8
A_HEAD_DIM = 64
A_WIDTH = A_HEADS * A_HEAD_DIM
A_LORA_W = 64
A_LORA_A = 64
A_LORA_G = 128
A_COLS = 3 * A_WIDTH + 2 * A_LORA_W + 2 * A_LORA_A + A_LORA_G
A_GN_EPS = 64e-5
B_HEADS = 4
B_QK_DIM = 64
B_V_DIM = 2 * B_QK_DIM
B_WIDTH = B_HEADS * B_V_DIM
B_COLS = 3 * B_WIDTH
B_QBLOCK = 128
C_HEADS = 8
C_HEAD_DIM = 64
C_WIDTH = C_HEADS * C_HEAD_DIM
C_COLS = 3 * C_WIDTH
NA_ROWS = 8
NA_COLS = 16
D_WIDTH = 512
D_BLOCKS = 8
D_BLOCK_DIM = D_WIDTH // D_BLOCKS
D_CONV = 4
RG_C = 8.0
D_COLS = 2 * D_WIDTH
N_BRANCH = 4
BRANCH_WIDTH = 512
GATE_COLS = N_BRANCH * D_MODEL
OFF_A = 0
OFF_B = OFF_A + A_COLS
OFF_C = OFF_B + B_COLS
OFF_D = OFF_C + C_COLS
OFF_G = OFF_D + D_COLS
N_IN = OFF_G + GATE_COLS
N_EXPERTS = 16
N_GROUPS = 4
EXPERTS_PER_GROUP = N_EXPERTS // N_GROUPS
TOP_K = 2
D_FF = D_MODEL // 2

ROPE_BASE = 10000.0
NORM_EPS = 1e-6

kernel_name = 'hybrid_prefix_diffusion_block'


def rms_norm(x, g):
    xf = x.astype(jnp.float32)
    y = xf * lax.rsqrt(jnp.mean(jnp.square(xf), axis=-1, keepdims=True) + NORM_EPS)
    return (y * g.astype(jnp.float32)).astype(x.dtype)


def modulate(h, shift, scale):
    return h * (1 + scale) + shift


def axial_rope(n_tok, dim):
    t = jnp.arange(n_tok)
    quarter = dim // 4
    inv = 1.0 / (ROPE_BASE ** (jnp.arange(quarter, dtype=jnp.float32) / quarter))
    row = (t // GRID_W).astype(jnp.float32)[:, None] * inv
    col = (t % GRID_W).astype(jnp.float32)[:, None] * inv
    ang = jnp.concatenate([row, col], axis=-1)
    return jnp.cos(ang), jnp.sin(ang)


def apply_rope(x, cos, sin):
    shp = (1, x.shape[1]) + (1,) * (x.ndim - 3) + (cos.shape[-1],)
    cs, sn = cos.reshape(shp), sin.reshape(shp)
    x1, x2 = jnp.split(x.astype(jnp.float32), 2, axis=-1)
    return jnp.concatenate([x1 * cs - x2 * sn, x1 * sn + x2 * cs], axis=-1).astype(x.dtype)


def centred_shift(z, mu):
    zero = jnp.zeros_like(z[:, :1])
    prev = jnp.concatenate([zero, z[:, :-1]], axis=1)
    nxt = jnp.concatenate([z[:, 1:], zero], axis=1)
    return z + mu[0] * (prev - z) + mu[1] * (nxt - z)


def depthwise_conv(z, w, b):
    out = lax.conv_general_dilated(
        z, w.astype(z.dtype)[:, None, :], window_strides=(1,),
        padding=[((D_CONV - 1) // 2, D_CONV // 2)],
        dimension_numbers=('NWC', 'WIO', 'NWC'), feature_group_count=z.shape[-1])
    return out + b


def a_heads(t):
    return t.reshape(t.shape[:2] + (A_HEADS, A_HEAD_DIM))


def rwkv7_prepare(z, mu, w0, w2, a0, a2, g2, k_k, k_a):
    Bn, L, _ = z.shape
    z = centred_shift(z, mu)
    r = z[..., 0:A_WIDTH]
    k = z[..., A_WIDTH:2 * A_WIDTH]
    v = z[..., 2 * A_WIDTH:3 * A_WIDTH]
    o = 3 * A_WIDTH
    wl = z[..., o:o + 2 * A_LORA_W].reshape(Bn, L, 2, A_LORA_W)
    o += 2 * A_LORA_W
    al = z[..., o:o + 2 * A_LORA_A].reshape(Bn, L, 2, A_LORA_A)
    o += 2 * A_LORA_A
    gl = z[..., o:o + A_LORA_G]
    w_log = -jax.nn.softplus(-(w0 + jnp.einsum('bldr,drc->bldc', jnp.tanh(wl), w2)).astype(jnp.float32)) - 0.5
    decay = jnp.exp(-jnp.exp(w_log))
    a = jax.nn.sigmoid((a0 + jnp.einsum('bldr,drc->bldc', al, a2)).astype(jnp.float32))
    g = jnp.einsum('blr,rc->blc', jax.nn.sigmoid(gl), g2).astype(jnp.float32)
    kf = k.astype(jnp.float32)
    kk = a_heads(kf * k_k)
    kk = kk / jnp.maximum(jnp.sqrt(jnp.sum(kk * kk, axis=-1, keepdims=True)), 1e-12)
    k_eff = kf[:, :, None, :] * (1.0 + (a - 1.0) * k_a)
    return dict(r=a_heads(r.astype(jnp.float32)), k=a_heads(kf), v=a_heads(v.astype(jnp.float32)),
                g=g, decay=decay, a=a, kk=kk, k_eff=k_eff)


def rwkv7_scan(r, decay, kk, kka, v, k, s0, reverse):
    emit = r is not None

    def step(S, inp):
        r_t, w_t, kk_t, b_t, v_t, k_t = inp
        sa = jnp.einsum('bhvk,bhk->bhv', S, kk_t)
        S = S * w_t[:, :, None, :] - sa[..., None] * b_t[:, :, None, :] + v_t[..., None] * k_t[:, :, None, :]
        return S, (jnp.einsum('bhvk,bhk->bhv', S, r_t) if emit else None)

    def tm(t):
        return None if t is None else jnp.moveaxis(t, 1, 0)

    S, ys = lax.scan(step, s0, (tm(r), tm(decay), tm(kk), tm(kka), tm(v), tm(k)), reverse=reverse)
    return S, (jnp.moveaxis(ys, 0, 1) if emit else None)


def rwkv7_readout(y, p, r_k, gn_g, gn_b):
    Bn, L = y.shape[:2]
    mean = jnp.mean(y, axis=-1, keepdims=True)
    var = jnp.mean(jnp.square(y - mean), axis=-1, keepdims=True)
    yn = ((y - mean) * lax.rsqrt(var + A_GN_EPS)).reshape(Bn, L, A_WIDTH) * gn_g + gn_b
    bonus = jnp.sum(p['r'] * p['k'] * r_k, axis=-1, keepdims=True) * p['v']
    return (yn + bonus.reshape(Bn, L, A_WIDTH)) * p['g']


def mixer_rwkv7(zc, zl, mu, w0, w2, a0, a2, g2, k_k, k_a, r_k, gn_g, gn_b, ctx_out):
    pc = rwkv7_prepare(zc, mu, w0, w2, a0, a2, g2, k_k, k_a)
    pl = rwkv7_prepare(zl, mu, w0, w2, a0, a2, g2, k_k, k_a)
    s0 = jnp.zeros((zl.shape[0], A_HEADS, A_HEAD_DIM, A_HEAD_DIM), jnp.float32)
    yl, yc = [], []
    for d, rev in ((0, False), (1, True)):
        def dir_args(p):
            return (a_heads(p['decay'][:, :, d]), p['kk'], p['kk'] * a_heads(p['a'][:, :, d]),
                    p['v'], a_heads(p['k_eff'][:, :, d]))
        s_ctx, y_c = rwkv7_scan(pc['r'] if ctx_out else None, *dir_args(pc), s0, rev)
        _, y_l = rwkv7_scan(pl['r'], *dir_args(pl), s_ctx, rev)
        yl.append(y_l)
        yc.append(y_c)
    out_l = rwkv7_readout(yl[0] + yl[1], pl, r_k, gn_g, gn_b).astype(zl.dtype)
    out_c = rwkv7_readout(yc[0] + yc[1], pc, r_k, gn_g, gn_b).astype(zc.dtype) if ctx_out else None
    return out_l, out_c


def diff_attend(q, k, v, lam):
    s = jnp.einsum('bqhtd,bkhtd->bhtqk', q, k).astype(jnp.float32) * (B_QK_DIM ** -0.5)
    p = jax.nn.softmax(s, axis=-1)
    w = p[:, :, 0] - lam * p[:, :, 1]
    return jnp.einsum('bhqk,bkhd->bqhd', w.astype(v.dtype), v)


def mixer_diff(zc, zl, qn_g, kn_g, lam_vecs, subln_g, lam_init, ctx_out):
    def split(z):
        Bn, L, _ = z.shape
        q = z[..., :B_WIDTH].reshape(Bn, L, B_HEADS, 2, B_QK_DIM)
        k = z[..., B_WIDTH:2 * B_WIDTH].reshape(Bn, L, B_HEADS, 2, B_QK_DIM)
        v = z[..., 2 * B_WIDTH:].reshape(Bn, L, B_HEADS, B_V_DIM)
        return rms_norm(q, qn_g), rms_norm(k, kn_g), v

    lv = lam_vecs.astype(jnp.float32)
    lam = jnp.exp(jnp.sum(lv[0] * lv[1])) - jnp.exp(jnp.sum(lv[2] * lv[3])) + lam_init
    qc, kc, vc = split(zc)
    ql, kl, vl = split(zl)
    Bn, S = zl.shape[:2]
    cos, sin = axial_rope(S, B_QK_DIM)
    ql = apply_rope(ql, cos, sin)
    kl = apply_rope(kl, cos, sin)
    k_all = jnp.concatenate([kl, kc], axis=1)
    v_all = jnp.concatenate([vl, vc], axis=1)
    nb = S // B_QBLOCK
    qb = jnp.moveaxis(ql.reshape(Bn, nb, B_QBLOCK, B_HEADS, 2, B_QK_DIM), 1, 0)
    ob = lax.map(lambda q: diff_attend(q, k_all, v_all, lam), qb)
    ol = jnp.moveaxis(ob, 0, 1).reshape(Bn, S, B_HEADS, B_V_DIM)

    def post(o):
        return (rms_norm(o, subln_g) * (1.0 - lam_init)).reshape(o.shape[0], o.shape[1], B_WIDTH).astype(zl.dtype)

    out_c = post(diff_attend(qc, kc, vc, lam)) if ctx_out else None
    return post(ol), out_c


def plain_attend(q, k, v):
    s = jnp.einsum('bqhd,bkhd->bhqk', q, k).astype(jnp.float32) * (q.shape[-1] ** -0.5)
    p = jax.nn.softmax(s, axis=-1)
    return jnp.einsum('bhqk,bkhd->bqhd', p.astype(v.dtype), v)


def mixer_na(zc, zl, qn_g, kn_g, rpb, ctx_out):
    def split(z):
        Bn, L, _ = z.shape
        q = z[..., :C_WIDTH].reshape(Bn, L, C_HEADS, C_HEAD_DIM)
        k = z[..., C_WIDTH:2 * C_WIDTH].reshape(Bn, L, C_HEADS, C_HEAD_DIM)
        v = z[..., 2 * C_WIDTH:].reshape(Bn, L, C_HEADS, C_HEAD_DIM)
        return rms_norm(q, qn_g), rms_norm(k, kn_g), v

    qc, kc, vc = split(zc)
    ql, kl, vl = split(zl)
    Bn, S = zl.shape[:2]
    rows = S // GRID_W
    kh = min(NA_ROWS, rows)
    kw = min(NA_COLS, GRID_W)
    n_loc = kh * GRID_W
    scale = C_HEAD_DIM ** -0.5
    kg = kl.reshape(Bn, rows, GRID_W, C_HEADS, C_HEAD_DIM)
    vg = vl.reshape(Bn, rows, GRID_W, C_HEADS, C_HEAD_DIM)
    qg = jnp.moveaxis(ql.reshape(Bn, rows, GRID_W, C_HEADS, C_HEAD_DIM), 1, 0)
    cq = jnp.arange(GRID_W)
    cstart = jnp.clip(cq - kw // 2, 0, GRID_W - kw)
    col_ok = (cq[None, :] >= cstart[:, None]) & (cq[None, :] < cstart[:, None] + kw)
    dc_idx = jnp.clip(cq[None, :] - cq[:, None] + NA_COLS - 1, 0, 2 * NA_COLS - 2)

    def one_row(args):
        r, q = args
        rs = jnp.clip(r - kh // 2, 0, rows - kh)
        kb = lax.dynamic_slice_in_dim(kg, rs, kh, axis=1)
        vb = lax.dynamic_slice_in_dim(vg, rs, kh, axis=1)
        dr_idx = rs + jnp.arange(kh) - r + NA_ROWS - 1
        bias = rpb[:, dr_idx[None, :, None], dc_idx[:, None, :]]
        s_loc = jnp.einsum('bqhd,bjkhd->bhqjk', q, kb).astype(jnp.float32) * scale + bias
        s_loc = jnp.where(col_ok[:, None, :], s_loc, -jnp.inf).reshape(Bn, C_HEADS, GRID_W, n_loc)
        s_ctx = jnp.einsum('bqhd,bchd->bhqc', q, kc).astype(jnp.float32) * scale
        p = jax.nn.softmax(jnp.concatenate([s_loc, s_ctx], axis=-1), axis=-1).astype(vg.dtype)
        return (jnp.einsum('bhqj,bjhd->bqhd', p[..., :n_loc], vb.reshape(Bn, n_loc, C_HEADS, C_HEAD_DIM))
                + jnp.einsum('bhqc,bchd->bqhd', p[..., n_loc:], vc))

    og = lax.map(one_row, (jnp.arange(rows), qg))
    out_l = jnp.moveaxis(og, 0, 1).reshape(Bn, S, C_WIDTH).astype(zl.dtype)
    out_c = plain_attend(qc, kc, vc).reshape(zc.shape[0], zc.shape[1], C_WIDTH).astype(zc.dtype) if ctx_out else None
    return out_l, out_c


def lru_scan(a, b, h0, reverse):
    def combine(e1, e2):
        return e1[0] * e2[0], e2[0] * e1[1] + e2[1]
    a_cum, b_cum = lax.associative_scan(combine, (a, b), axis=1, reverse=reverse)
    h = a_cum * h0[:, None, :] + b_cum
    return h, (h[:, 0] if reverse else h[:, -1])


def mixer_rglru(zc, zl, conv_w, conv_b, wa, ba, wx, bx, lam, ctx_out):
    def gates(xr, d):
        Bn, L, _ = xr.shape
        xh = xr.reshape(Bn, L, D_BLOCKS, D_BLOCK_DIM)
        rg = jax.nn.sigmoid((jnp.einsum('blni,nij->blnj', xh, wa[d]).reshape(Bn, L, D_WIDTH) + ba[d]).astype(jnp.float32))
        ig = jax.nn.sigmoid((jnp.einsum('blni,nij->blnj', xh, wx[d]).reshape(Bn, L, D_WIDTH) + bx[d]).astype(jnp.float32))
        log_a = -RG_C * jax.nn.softplus(-lam[d].astype(jnp.float32)) * rg
        return jnp.exp(log_a), jnp.sqrt(-jnp.expm1(2.0 * log_a)) * ig * xr.astype(jnp.float32)

    xr_c = depthwise_conv(zc[..., :D_WIDTH], conv_w, conv_b)
    xr_l = depthwise_conv(zl[..., :D_WIDTH], conv_w, conv_b)
    h0 = jnp.zeros((zl.shape[0], D_WIDTH), jnp.float32)
    hs_c, hs_l = [], []
    for d, rev in ((0, False), (1, True)):
        h_c, fin_c = lru_scan(*gates(xr_c, d), h0, rev)
        h_l, _ = lru_scan(*gates(xr_l, d), fin_c, rev)
        hs_c.append(h_c)
        hs_l.append(h_l)
    out_l = ((hs_l[0] + hs_l[1]) * jax.nn.gelu(zl[..., D_WIDTH:].astype(jnp.float32))).astype(zl.dtype)
    out_c = ((hs_c[0] + hs_c[1]) * jax.nn.gelu(zc[..., D_WIDTH:].astype(jnp.float32))).astype(zc.dtype) if ctx_out else None
    return out_l, out_c


def merge_branches(z_gate, branches, w_branch, w_out):
    Bn, L, _ = z_gate.shape
    gates = jax.nn.sigmoid(z_gate.reshape(Bn, L, N_BRANCH, D_MODEL))
    y = gates[:, :, 0] * (branches[0] @ w_branch[0])
    for j in range(1, N_BRANCH):
        y = y + gates[:, :, j] * (branches[j] @ w_branch[j])
    return y @ w_out


def moe_ffn(h, router_w, router_b, w_gate, w_up, w_down):
    T = h.shape[0]
    s = jax.nn.sigmoid((h @ router_w).astype(jnp.float32))
    grp = (s + router_b).reshape(T, N_GROUPS, EXPERTS_PER_GROUP)
    g_best = jnp.argmax(jnp.sum(lax.top_k(grp, TOP_K)[0], axis=-1), axis=-1)
    in_grp = (g_best[:, None] == jnp.arange(N_GROUPS))[:, :, None]
    _, e_idx = lax.top_k(jnp.where(in_grp, grp, -jnp.inf).reshape(T, N_EXPERTS), TOP_K)
    wts = jnp.take_along_axis(s, e_idx, axis=1)
    wts = wts / jnp.sum(wts, axis=-1, keepdims=True)
    flat_e = e_idx.reshape(-1)
    order = jnp.argsort(flat_e)
    tok = order // TOP_K
    sizes = jnp.bincount(flat_e, length=N_EXPERTS).astype(jnp.int32)
    xs = h[tok]
    act = jax.nn.silu(lax.ragged_dot(xs, w_gate, sizes)) * lax.ragged_dot(xs, w_up, sizes)
    out = lax.ragged_dot(act, w_down, sizes).astype(jnp.float32) * wts.reshape(-1)[order][:, None]
    return jax.ops.segment_sum(out, tok, num_segments=T).astype(h.dtype)


def setup_inputs(seed: int = 0) -> dict:
    key = jax.random.key(seed)
    keys = iter(jax.random.split(key, 48))

    def nrm(shape, std):
        return jax.random.normal(next(keys), shape, jnp.float32) * std

    def uni(shape, lo, hi):
        return jax.random.uniform(next(keys), shape, jnp.float32, lo, hi)

    L = DEPTH
    x = nrm((BATCH, SEQ, D_MODEL), 1.0)
    c = nrm((BATCH, D_MODEL), 1.0)
    ctx = nrm((BATCH, CTX_LEN, D_MODEL), 1.0)
    c_ctx = nrm((D_MODEL,), 1.0)
    w_mod = nrm((L, D_MODEL, 6 * D_MODEL), 0.5 * D_MODEL ** -0.5)
    b_mod = nrm((L, 6 * D_MODEL), 0.02)
    norm1_g = 1.0 + nrm((L, D_MODEL), 0.02)
    norm2_g = 1.0 + nrm((L, D_MODEL), 0.02)
    w_in = nrm((L, D_MODEL, N_IN), D_MODEL ** -0.5)
    a_mu = uni((L, 2, A_COLS), 0.0, 0.5)
    a_w0 = uni((L, 2, A_WIDTH), -6.0, 1.0)
    a_w2 = nrm((L, 2, A_LORA_W, A_WIDTH), 0.1)
    a_a0 = nrm((L, 2, A_WIDTH), 0.1)
    a_a2 = nrm((L, 2, A_LORA_A, A_WIDTH), 0.1)
    a_g2 = nrm((L, A_LORA_G, A_WIDTH), A_LORA_G ** -0.5)
    a_kk = 1.0 + nrm((L, A_WIDTH), 0.1)
    a_ka = 1.0 + nrm((L, A_WIDTH), 0.1)
    a_rk = nrm((L, A_HEADS, A_HEAD_DIM), 0.1)
    a_gn_g = 1.0 + nrm((L, A_WIDTH), 0.02)
    a_gn_b = nrm((L, A_WIDTH), 0.02)
    b_qn = 1.0 + nrm((L, 2, B_QK_DIM), 0.02)
    b_kn = 1.0 + nrm((L, 2, B_QK_DIM), 0.02)
    b_lam = nrm((L, 4, B_QK_DIM), 0.1)
    b_subln = 1.0 + nrm((L, B_V_DIM), 0.02)
    c_qn = 1.0 + nrm((L, C_HEAD_DIM), 0.02)
    c_kn = 1.0 + nrm((L, C_HEAD_DIM), 0.02)
    c_rpb = nrm((L, C_HEADS, 2 * NA_ROWS - 1, 2 * NA_COLS - 1), 0.1)
    d_conv_w = nrm((L, D_CONV, D_WIDTH), D_CONV ** -0.5)
    d_conv_b = nrm((L, D_WIDTH), 0.02)
    d_wa = nrm((L, 2, D_BLOCKS, D_BLOCK_DIM, D_BLOCK_DIM), D_BLOCK_DIM ** -0.5)
    d_ba = nrm((L, 2, D_WIDTH), 0.02)
    d_wx = nrm((L, 2, D_BLOCKS, D_BLOCK_DIM, D_BLOCK_DIM), D_BLOCK_DIM ** -0.5)
    d_bx = nrm((L, 2, D_WIDTH), 0.02)
    lru_root = uni((L, 2, D_WIDTH), 0.9, 0.999) ** (1.0 / RG_C)
    d_lam = jnp.log(lru_root) - jnp.log1p(-lru_root)
    w_branch = nrm((L, N_BRANCH, BRANCH_WIDTH, D_MODEL), BRANCH_WIDTH ** -0.5)
    w_out = nrm((L, D_MODEL, D_MODEL), D_MODEL ** -0.5)
    router_w = nrm((D_MODEL, N_EXPERTS), D_MODEL ** -0.5)
    router_b = nrm((N_EXPERTS,), 0.01)
    e_gate = nrm((L, N_EXPERTS, D_MODEL, D_FF), D_MODEL ** -0.5)
    e_up = nrm((L, N_EXPERTS, D_MODEL, D_FF), D_MODEL ** -0.5)
    e_down = nrm((L, N_EXPERTS, D_FF, D_MODEL), D_FF ** -0.5)
    return {'x': x, 'c': c, 'ctx': ctx, 'c_ctx': c_ctx, 'w_mod': w_mod, 'b_mod': b_mod,
            'norm1_g': norm1_g, 'norm2_g': norm2_g, 'w_in': w_in,
            'a_mu': a_mu, 'a_w0': a_w0, 'a_w2': a_w2, 'a_a0': a_a0, 'a_a2': a_a2, 'a_g2': a_g2,
            'a_kk': a_kk, 'a_ka': a_ka, 'a_rk': a_rk, 'a_gn_g': a_gn_g, 'a_gn_b': a_gn_b,
            'b_qn': b_qn, 'b_kn': b_kn, 'b_lam': b_lam, 'b_subln': b_subln,
            'c_qn': c_qn, 'c_kn': c_kn, 'c_rpb': c_rpb,
            'd_conv_w': d_conv_w, 'd_conv_b': d_conv_b, 'd_wa': d_wa, 'd_ba': d_ba,
            'd_wx': d_wx, 'd_bx': d_bx, 'd_lam': d_lam,
            'w_branch': w_branch, 'w_out': w_out, 'router_w': router_w, 'router_b': router_b,
            'e_gate': e_gate, 'e_up': e_up, 'e_down': e_down}


def reference(x, c, ctx, c_ctx, w_mod, b_mod, norm1_g, norm2_g, w_in,
              a_mu, a_w0, a_w2, a_a0, a_a2, a_g2, a_kk, a_ka, a_rk, a_gn_g, a_gn_b,
              b_qn, b_kn, b_lam, b_subln, c_qn, c_kn, c_rpb,
              d_conv_w, d_conv_b, d_wa, d_ba, d_wx, d_bx, d_lam,
              w_branch, w_out, router_w, router_b, e_gate, e_up, e_down):
    xl, xc = x, ctx
    silu_lat = jax.nn.silu(c)
    silu_ctx = jax.nn.silu(c_ctx)
    for i in range(DEPTH):
        ctx_out = i < DEPTH - 1
        lam_init = 0.8 - 0.6 * math.exp(-0.3 * i)
        m_l = jnp.split((silu_lat @ w_mod[i] + b_mod[i])[:, None, :], 6, axis=-1)
        m_c = jnp.split(silu_ctx @ w_mod[i] + b_mod[i], 6, axis=-1)
        zl = modulate(rms_norm(xl, norm1_g[i]), m_l[0], m_l[1]) @ w_in[i]
        zc = modulate(rms_norm(xc, norm1_g[i]), m_c[0], m_c[1]) @ w_in[i]
        oa = mixer_rwkv7(zc[..., OFF_A:OFF_B], zl[..., OFF_A:OFF_B], a_mu[i], a_w0[i], a_w2[i], a_a0[i],
                         a_a2[i], a_g2[i], a_kk[i], a_ka[i], a_rk[i], a_gn_g[i], a_gn_b[i], ctx_out)
        ob = mixer_diff(zc[..., OFF_B:OFF_C], zl[..., OFF_B:OFF_C], b_qn[i], b_kn[i], b_lam[i], b_subln[i],
                        lam_init, ctx_out)
        oc = mixer_na(zc[..., OFF_C:OFF_D], zl[..., OFF_C:OFF_D], c_qn[i], c_kn[i], c_rpb[i], ctx_out)
        od = mixer_rglru(zc[..., OFF_D:OFF_G], zl[..., OFF_D:OFF_G], d_conv_w[i], d_conv_b[i], d_wa[i], d_ba[i],
                         d_wx[i], d_bx[i], d_lam[i], ctx_out)
        xl = xl + m_l[2] * merge_branches(zl[..., OFF_G:], (oa[0], ob[0], oc[0], od[0]), w_branch[i], w_out[i])
        hl = modulate(rms_norm(xl, norm2_g[i]), m_l[3], m_l[4]).reshape(-1, D_MODEL)
        if ctx_out:
            xc = xc + m_c[2] * merge_branches(zc[..., OFF_G:], (oa[1], ob[1], oc[1], od[1]), w_branch[i], w_out[i])
            hc = modulate(rms_norm(xc, norm2_g[i]), m_c[3], m_c[4]).reshape(-1, D_MODEL)
            y = moe_ffn(jnp.concatenate([hl, hc], axis=0), router_w, router_b, e_gate[i], e_up[i], e_down[i])
            xl = xl + m_l[5] * y[:hl.shape[0]].reshape(xl.shape)
            xc = xc + m_c[5] * y[hl.shape[0]:].reshape(xc.shape)
        else:
            xl = xl + m_l[5] * moe_ffn(hl, router_w, router_b, e_gate[i], e_up[i], e_down[i]).reshape(xl.shape)
    return xl
```

```python
import functools
import math

import numpy as np
import jax
import jax.numpy as jnp
from jax import lax
from jax.experimental import pallas as pl
from jax.experimental.pallas import tpu as pltpu

F32 = jnp.float32
BF16 = jnp.bfloat16

D_MODEL = 2048
DEPTH = 4
GRID_W = 64
CTX = 256
SEQ = 2048
ROWS = CTX + SEQ
HEAD = 64
A_WIDTH = 512
A_COLS = 1920
A_GN_EPS = 64e-5
B_HEADS = 4
C_HEADS = 8
NA_ROWS = 8
NA_COLS = 16
D_WIDTH = 512
D_CONV = 4
RG_C = 8.0
N_BRANCH = 4
N_EXPERTS = 16
N_GROUPS = 4
D_FF = 1024
ROPE_BASE = 10000.0
NORM_EPS = 1e-6
Z_COLS = 6144
BLK = 512

VMEM_LIMIT = 56 * 1024 * 1024

CHUNK = 64
TQ = 256
FFN_TM = 512


def _cp(*sem):
    return pltpu.CompilerParams(dimension_semantics=sem, vmem_limit_bytes=VMEM_LIMIT)


def _dot(a, b):
    return jnp.dot(a, b, preferred_element_type=F32)


def _dot_nt(a, b):
    return lax.dot_general(a, b, (((1,), (1,)), ((), ())), preferred_element_type=F32)


def _dot_tn(a, b):
    return lax.dot_general(a, b, (((0,), (0,)), ((), ())), preferred_element_type=F32)


def _split(x):
    hi = x.astype(BF16)
    lo = (x - hi.astype(F32)).astype(BF16)
    return hi, lo


def _dot_x(x, m):
    hi, lo = _split(x)
    return _dot(hi, m) + _dot(lo, m)


def _sigmoid(x):
    return 1.0 / (1.0 + jnp.exp(-x))


def _softplus(x):
    return jnp.maximum(x, 0.0) + jnp.log(1.0 + jnp.exp(-jnp.abs(x)))


def _iota(shape, dim):
    return lax.broadcasted_iota(jnp.int32, shape, dim)


def _seg_ones(width, seg):
    r = np.arange(width)
    return jnp.asarray((r[:, None] // seg) == (r[None, :] // seg), dtype=BF16)


def _mod_body(c_ref, w_ref, b_ref, o_ref):
    c = c_ref[...]
    s = (c * _sigmoid(c)).astype(BF16)
    o_ref[...] = _dot(s, w_ref[...].astype(BF16)) + b_ref[...]


def mod_vectors(cc, w_mod, b_mod):
    depth, d, n = w_mod.shape
    r = cc.shape[0]
    tn = 1536
    return pl.pallas_call(
        _mod_body,
        grid=(depth, n // tn),
        in_specs=[pl.BlockSpec((r, d), lambda l, j: (0, 0)),
                  pl.BlockSpec((None, d, tn), lambda l, j: (l, 0, j)),
                  pl.BlockSpec((None, 1, tn), lambda l, j: (l, 0, j))],
        out_specs=pl.BlockSpec((None, r, tn), lambda l, j: (l, 0, j)),
        out_shape=jax.ShapeDtypeStruct((depth, r, n), F32),
        compiler_params=_cp("arbitrary", "arbitrary"),
        name="mod_vectors",
    )(cc, w_mod, b_mod.reshape(depth, 1, n))


def _mod_rows(mod_ref, idx, row0, tm):
    rows = row0 + _iota((tm, 1), 0)
    return jnp.where(rows < CTX, mod_ref[0, 1, idx:idx + 1, :], mod_ref[0, 0, idx:idx + 1, :])


def _norm_mod(x, g, mod_ref, shift_idx, row0, tm):
    ms = jnp.mean(x * x, axis=-1, keepdims=True)
    y = x * lax.rsqrt(ms + NORM_EPS) * g
    return y * (1.0 + _mod_rows(mod_ref, shift_idx + 1, row0, tm)) + _mod_rows(mod_ref, shift_idx, row0, tm)


def _inproj_body(x_ref, mod_ref, g_ref, w_ref, z_ref, hn_ref, *, tm, tpb):
    i = pl.program_id(0)

    @pl.when(pl.program_id(1) == 0)
    def _():
        row0 = (i % tpb) * tm
        hn_ref[...] = _norm_mod(x_ref[...], g_ref[...], mod_ref, 0, row0, tm).astype(BF16)

    z_ref[...] = _dot(hn_ref[...], w_ref[...])


def in_projection(x, mod, g, w, *, tm=768, tn=512):
    t, d = x.shape
    n = w.shape[1]
    tpb = ROWS // tm
    return pl.pallas_call(
        functools.partial(_inproj_body, tm=tm, tpb=tpb),
        grid=(t // tm, n // tn),
        in_specs=[pl.BlockSpec((tm, d), lambda i, j: (i, 0)),
                  pl.BlockSpec((1, 2, 6, d), lambda i, j: (i // tpb, 0, 0, 0)),
                  pl.BlockSpec((1, d), lambda i, j: (0, 0)),
                  pl.BlockSpec((d, tn), lambda i, j: (0, j))],
        out_specs=[pl.BlockSpec((tm, tn), lambda i, j: (i, j)),
                   pl.BlockSpec((tm, d), lambda i, j: (i, 0))],
        out_shape=[jax.ShapeDtypeStruct((t, n), F32), jax.ShapeDtypeStruct((t, d), BF16)],
        compiler_params=_cp("arbitrary", "arbitrary"),
        name="in_projection",
    )(x, mod, g, w)


def _tile_flags(ti, tpb):
    ctx_tiles = CTX // TQ
    has_prev = jnp.logical_and(ti != 0, ti != ctx_tiles)
    has_next = jnp.logical_and(ti != ctx_tiles - 1, ti != tpb - 1)
    return has_prev, has_next


def _halo_specs(width, col, n_tiles):
    r8 = TQ // 8
    return [pl.BlockSpec((TQ, width), lambda i: (i, col)),
            pl.BlockSpec((8, width), lambda i: (jnp.maximum(i * r8 - 1, 0), col)),
            pl.BlockSpec((8, width), lambda i: (jnp.minimum((i + 1) * r8, n_tiles * r8 - 1), col))]


def _shift_rows(z, halo_row, k, use_halo):
    n = z.shape[0]
    rows = _iota((n, 1), 0)
    fill = jnp.where(use_halo, halo_row, 0.0)
    if k > 0:
        return jnp.where(rows < k, fill, pltpu.roll(z, k, 0))
    return jnp.where(rows >= n + k, fill, pltpu.roll(z, n + k, 0))


def _rwkv_prep_body(z_ref, zp_ref, zn_ref, mu_ref, w2_ref, a2_ref, g2_ref, w0_ref, a0_ref,
                    kk_ref, ka_ref, rk_ref, seg_ref, o_ref, *, tpb):
    has_prev, has_next = _tile_flags(pl.program_id(0) % tpb, tpb)
    z = z_ref[...]
    prev = _shift_rows(z, zp_ref[7:8, :], 1, has_prev)
    nxt = _shift_rows(z, zn_ref[0:1, :], -1, has_next)
    zs = z + mu_ref[0:1, :] * (prev - z) + mu_ref[1:2, :] * (nxt - z)
    r = zs[:, 0:512]
    k = zs[:, 512:1024]
    v = zs[:, 1024:1536]
    wl = zs[:, 1536:1664]
    al = zs[:, 1664:1792]
    gl = zs[:, 1792:1920]
    seg = seg_ref[...]
    wlog = -_softplus(-(w0_ref[...] + _dot(jnp.tanh(wl).astype(BF16), w2_ref[...]))) - 0.5
    lw = -jnp.exp(wlog)
    a = _sigmoid(a0_ref[...] + _dot(al.astype(BF16), a2_ref[...]))
    g = _dot(_sigmoid(gl).astype(BF16), g2_ref[...])
    kk = k * kk_ref[...]
    kk = kk / jnp.maximum(jnp.sqrt(_dot_x(kk * kk, seg)), 1e-12)
    bonus = _dot_x(r * k * rk_ref[...], seg) * v
    ka = ka_ref[...]
    o_ref[:, 0:512] = r
    o_ref[:, 512:1024] = v
    o_ref[:, 1024:1536] = kk
    o_ref[:, 1536:2048] = bonus
    o_ref[:, 2048:2560] = g
    for d in range(2):
        ad = a[:, d * 512:(d + 1) * 512]
        base = 2560 + d * 1536
        o_ref[:, base:base + 512] = lw[:, d * 512:(d + 1) * 512]
        o_ref[:, base + 512:base + 1024] = kk * ad
        o_ref[:, base + 1024:base + 1536] = k * (1.0 + (ad - 1.0) * ka)


A_PREP_COLS = 2560 + 2 * 1536


def rwkv_prepare(z, mu, w2bd, a2bd, g2, w0, a0, k_k, k_a, r_k):
    t = z.shape[0]
    n_tiles = t // TQ
    tpb = ROWS // TQ
    full = lambda shape: pl.BlockSpec(shape, lambda i: (0,) * len(shape))
    return pl.pallas_call(
        functools.partial(_rwkv_prep_body, tpb=tpb),
        grid=(n_tiles,),
        in_specs=_halo_specs(2048, 0, n_tiles) + [
            full((2, 2048)), full((128, 1024)), full((128, 1024)), full((128, 512)),
            full((1, 1024)), full((1, 1024)), full((1, 512)), full((1, 512)), full((1, 512)),
            full((512, 512))],
        out_specs=pl.BlockSpec((TQ, A_PREP_COLS), lambda i: (i, 0)),
        out_shape=jax.ShapeDtypeStruct((t, A_PREP_COLS), F32),
        compiler_params=_cp("arbitrary"),
        name="rwkv_prepare",
    )(z, z, z, mu, w2bd, a2bd, g2, w0, a0, k_k, k_a, r_k, _seg_ones(512, HEAD))


def _rwkv_scan_body(r_ref, v_ref, kk_ref, lw_ref, b_ref, ke_ref, *rest, reverse, last, cpb):
    if last:
        y0_ref, bonus_ref, g_ref, gng_ref, gnb_ref, seg_ref, o_ref, s_ref = rest
    else:
        o_ref, s_ref = rest
    c = CHUNK

    @pl.when(pl.program_id(0) % cpb == 0)
    def _():
        s_ref[...] = jnp.zeros_like(s_ref)

    ti = _iota((c, c), 0)
    si = _iota((c, c), 1)
    before = (si > ti) if reverse else (si < ti)
    upto = jnp.logical_or(before, si == ti)
    tri = jnp.where(upto, 1.0, 0.0).astype(BF16)
    eye = jnp.where(si == ti, 1.0, 0.0)
    pair = (ti >> 1) == (si >> 1)
    off_masks = []
    lg = 1
    while (2 << lg) <= c:
        off_masks.append(jnp.logical_and((ti >> (lg + 1)) == (si >> (lg + 1)), (ti >> lg) != (si >> lg)))
        lg += 1
    ti2 = _iota((c, 2 * c), 0)
    si2 = _iota((c, 2 * c), 1) % c
    upto2 = (si2 >= ti2) if reverse else (si2 <= ti2)

    lw = lw_ref[...]
    lw_hi, lw_lo = _split(lw)
    cs = _dot(tri, lw_hi) + _dot(tri, lw_lo)
    tot = jnp.sum(lw, axis=0, keepdims=True)
    e_pos = jnp.exp(cs)
    e_neg = jnp.exp(-cs)
    e_hat = jnp.exp(tot - cs)
    gamma = jnp.exp(tot)
    rt = r_ref[...] * e_pos
    at = -kk_ref[...] * jnp.exp(cs - lw)
    b = b_ref[...]
    ke = ke_ref[...]
    bt = b * e_neg
    kt = ke * e_neg
    bh = b * e_hat
    kh = ke * e_hat
    v = v_ref[...]

    ys = []
    for h in range(A_WIDTH // HEAD):
        sl = slice(h * HEAD, (h + 1) * HEAD)
        a_h, r_h, b_h, k_h, v_h = (m[:, sl].astype(BF16) for m in (at, rt, bt, kt, v))
        lab = jnp.where(before, _dot_nt(a_h, b_h), 0.0)
        lak = jnp.where(before, _dot_nt(a_h, k_h), 0.0)
        bk_h = jnp.concatenate([b_h, k_h], axis=0)
        m = jnp.where(upto2, _dot_nt(r_h, bk_h), 0.0).astype(BF16)
        tinv = eye + jnp.where(pair, lab, 0.0)
        for off_mask in off_masks:
            tb = tinv.astype(BF16)
            x = _dot(jnp.where(off_mask, lab, 0.0).astype(BF16), tb)
            tinv = tinv + _dot(tb, x.astype(BF16))
        lv = _dot(lak.astype(BF16), v_h)
        tb = tinv.astype(BF16)
        w1 = _dot(tb, a_h.astype(BF16))
        u2 = _dot(tb, lv.astype(BF16))
        s0 = s_ref[h]
        s0b = s0.astype(BF16)
        u = _dot_nt(w1.astype(BF16), s0b) + u2
        uv = jnp.concatenate([u.astype(BF16), v_h], axis=0)
        ys.append(_dot_nt(r_h, s0b) + _dot(m, uv))
        bkh = jnp.concatenate([bh[:, sl].astype(BF16), kh[:, sl].astype(BF16)], axis=0)
        s_ref[h] = s0 * gamma[:, sl] + _dot_tn(uv, bkh)
    y = jnp.concatenate(ys, axis=1)
    if last:
        y = y + y0_ref[...]
        seg = seg_ref[...]
        inv = 1.0 / HEAD
        mean = _dot_x(y, seg) * inv
        yc = y - mean
        var = _dot_x(yc * yc, seg) * inv
        yn = yc * lax.rsqrt(var + A_GN_EPS) * gng_ref[...] + gnb_ref[...]
        o_ref[...] = (yn + bonus_ref[...]) * g_ref[...]
    else:
        o_ref[...] = y


def rwkv_scan(prep, direction, y0=None, gn_g=None, gn_b=None):
    t = prep.shape[0]
    n_chunks = t // CHUNK
    cpb = ROWS // CHUNK
    cctx = CTX // CHUNK
    reverse = direction == 1
    last = y0 is not None

    def blk(i):
        b = i // cpb
        p = i % cpb
        if reverse:
            p = jnp.where(p < cctx, cctx - 1 - p, cpb - 1 + cctx - p)
        return b * cpb + p

    def col(cb):
        return pl.BlockSpec((CHUNK, BLK), lambda i: (blk(i), cb))

    base = 5 + 3 * direction
    in_specs = [col(0), col(1), col(2), col(base), col(base + 1), col(base + 2)]
    args = [prep] * 6
    if last:
        in_specs += [pl.BlockSpec((CHUNK, BLK), lambda i: (blk(i), 0)), col(3), col(4),
                     pl.BlockSpec((1, BLK), lambda i: (0, 0)), pl.BlockSpec((1, BLK), lambda i: (0, 0)),
                     pl.BlockSpec((BLK, BLK), lambda i: (0, 0))]
        args += [y0, prep, prep, gn_g, gn_b, _seg_ones(512, HEAD)]
    return pl.pallas_call(
        functools.partial(_rwkv_scan_body, reverse=reverse, last=last, cpb=cpb),
        grid=(n_chunks,),
        in_specs=in_specs,
        out_specs=pl.BlockSpec((CHUNK, BLK), lambda i: (blk(i), 0)),
        out_shape=jax.ShapeDtypeStruct((t, BLK), F32),
        scratch_shapes=[pltpu.VMEM((A_WIDTH // HEAD, HEAD, HEAD), F32)],
        compiler_params=_cp("arbitrary"),
        name="rwkv_scan_%d" % direction,
    )(*args)


def _rope(x, cos, sin):
    lane = _iota(x.shape, 1)
    partner = jnp.where((lane % HEAD) < HEAD // 2, pltpu.roll(x, 128 - HEAD // 2, 1), pltpu.roll(x, HEAD // 2, 1))
    return x * cos + partner * sin


def _head_rms(x, gain, seg):
    ms = _dot_x(x * x, seg) * (1.0 / HEAD)
    return x * lax.rsqrt(ms + NORM_EPS) * gain


def _diff_body(q_ref, k_ref, v_ref, cos_ref, sin_ref, qn_ref, kn_ref, lam_ref, sub_ref, seg_ref,
               o_ref, kh_ref, vh_ref, *, lam_init, nq):
    iq = pl.program_id(2)
    seg = seg_ref[...]

    @pl.when(iq == 0)
    def _():
        kh = _rope(_head_rms(k_ref[...], kn_ref[...], seg), cos_ref[...], sin_ref[...])
        kh_ref[...] = kh.astype(BF16)
        vh_ref[...] = v_ref[...].astype(BF16)

    lv = lam_ref[...]
    lam = (jnp.exp(jnp.sum(lv[0:1] * lv[1:2], axis=1, keepdims=True))
           - jnp.exp(jnp.sum(lv[2:3] * lv[3:4], axis=1, keepdims=True)) + lam_init)
    r0 = pl.multiple_of(iq * TQ, TQ)
    q = _rope(_head_rms(q_ref[...], qn_ref[...], seg), cos_ref[pl.ds(r0, TQ), :], sin_ref[pl.ds(r0, TQ), :])
    q = q * (HEAD ** -0.5)
    lane = _iota(q.shape, 1)
    q1 = jnp.where(lane < HEAD, q, 0.0).astype(BF16)
    q2 = jnp.where(lane >= HEAD, q, 0.0).astype(BF16)

    def attend(kh, vh):
        def soft(qs):
            s = _dot_nt(qs, kh)
            e = jnp.exp(s - jnp.max(s, axis=-1, keepdims=True))
            return e / jnp.sum(e, axis=-1, keepdims=True)
        w = soft(q1) - lam * soft(q2)
        o = _dot(w.astype(BF16), vh)
        ms = jnp.mean(o * o, axis=-1, keepdims=True)
        o_ref[...] = o * lax.rsqrt(ms + NORM_EPS) * sub_ref[...] * (1.0 - lam_init)

    @pl.when(iq == 0)
    def _():
        attend(kh_ref[0:CTX, :], vh_ref[0:CTX, :])

    @pl.when(iq != 0)
    def _():
        attend(kh_ref[...], vh_ref[...])


def diff_attention(z, cos, sin, qn, kn, lam_vecs, subln, lam_init):
    t = z.shape[0]
    nb = t // ROWS
    nq = ROWS // TQ
    cb = BLK // 128
    full = lambda shape: pl.BlockSpec(shape, lambda b, h, i: (0,) * len(shape))
    return pl.pallas_call(
        functools.partial(_diff_body, lam_init=lam_init, nq=nq),
        grid=(nb, B_HEADS, nq),
        in_specs=[pl.BlockSpec((TQ, 128), lambda b, h, i: (b * nq + i, 4 * cb + h)),
                  pl.BlockSpec((ROWS, 128), lambda b, h, i: (b, 5 * cb + h)),
                  pl.BlockSpec((ROWS, 128), lambda b, h, i: (b, 6 * cb + h)),
                  full((ROWS, 128)), full((ROWS, 128)), full((1, 128)), full((1, 128)),
                  full((4, HEAD)), full((1, 128)), full((128, 128))],
        out_specs=pl.BlockSpec((TQ, 128), lambda b, h, i: (b * nq + i, h)),
        out_shape=jax.ShapeDtypeStruct((t, BLK), F32),
        scratch_shapes=[pltpu.VMEM((ROWS, 128), BF16), pltpu.VMEM((ROWS, 128), BF16)],
        compiler_params=_cp("arbitrary", "arbitrary", "arbitrary"),
        name="diff_attention",
    )(z, z, z, cos, sin, qn, kn, lam_vecs, subln, _seg_ones(128, HEAD))


def rope_tables():
    t = np.arange(SEQ)
    quarter = HEAD // 4
    inv = (1.0 / (ROPE_BASE ** (np.arange(quarter, dtype=np.float32) / quarter))).astype(np.float32)
    row = (t // GRID_W).astype(np.float32)[:, None] * inv
    col = (t % GRID_W).astype(np.float32)[:, None] * inv
    ang = np.concatenate([row, col], axis=-1)
    cos = np.concatenate([np.ones((CTX, HEAD // 2), np.float32), np.cos(ang)], axis=0)
    sin = np.concatenate([np.zeros((CTX, HEAD // 2), np.float32), np.sin(ang)], axis=0)
    cos = np.tile(cos, (1, 4))
    sin = np.tile(np.concatenate([-sin, sin], axis=1), (1, 2))
    return jnp.asarray(cos, F32), jnp.asarray(sin, F32)


def _rpb_body(r_ref, e_ref, o_ref):
    o_ref[...] = _dot_x(r_ref[...], e_ref[...])


def rpb_tables(rpb):
    depth = rpb.shape[0]
    nr, nc = 2 * NA_ROWS - 1, 2 * NA_COLS - 1
    q = np.arange(GRID_W)
    dc = np.clip(q[None, :] - q[:, None] + NA_COLS - 1, 0, nc - 1)
    onehot = (np.arange(32)[:, None, None] == dc[None]).astype(np.float32).reshape(32, GRID_W * GRID_W)
    r2 = jnp.pad(rpb.reshape(depth * C_HEADS * nr, nc), ((0, 0), (0, 32 - nc)))
    rows = r2.shape[0]
    toe = pl.pallas_call(
        _rpb_body,
        grid=(1,),
        in_specs=[pl.BlockSpec((rows, 32), lambda i: (0, 0)), pl.BlockSpec((32, GRID_W * GRID_W), lambda i: (0, 0))],
        out_specs=pl.BlockSpec((rows, GRID_W * GRID_W), lambda i: (0, 0)),
        out_shape=jax.ShapeDtypeStruct((rows, GRID_W * GRID_W), F32),
        compiler_params=_cp("arbitrary"),
        name="rpb_tables",
    )(r2, jnp.asarray(onehot, BF16))
    toe = toe.reshape(depth, C_HEADS, nr, GRID_W, GRID_W)
    cases = [toe[:, :, NA_ROWS - 1 - d:2 * NA_ROWS - 1 - d] for d in range(NA_ROWS)]
    tab = jnp.stack(cases, axis=1)
    tab = jnp.transpose(tab, (0, 1, 2, 4, 3, 5))
    return tab.reshape(depth, NA_ROWS, C_HEADS, GRID_W, NA_ROWS * GRID_W)


def _na_body(q_ref, k_ref, v_ref, bias_ref, qn_ref, kn_ref, seg_ref, o_ref, kh_ref, vh_ref, *, nctx):
    step = pl.program_id(1)
    seg = seg_ref[...]

    @pl.when(step == 0)
    def _():
        kh_ref[...] = _head_rms(k_ref[...], kn_ref[...], seg).astype(BF16)
        vh_ref[...] = v_ref[...].astype(BF16)

    q = _head_rms(q_ref[...], qn_ref[...], seg) * (HEAD ** -0.5)
    lane = _iota((GRID_W, 128), 1)
    low = lane < HEAD

    def heads(body):
        for hp in range(C_HEADS // 2):
            sl = slice(hp * 128, (hp + 1) * 128)
            qp = q[:, sl]
            outs = []
            for sub in range(2):
                qs = jnp.where(low if sub == 0 else jnp.logical_not(low), qp, 0.0).astype(BF16)
                outs.append(body(2 * hp + sub, qs, sl))
            o_ref[:, sl] = jnp.where(low, outs[0], outs[1])

    @pl.when(step < nctx)
    def _():
        def body(h, qs, sl):
            s = _dot_nt(qs, kh_ref[0:CTX, sl])
            e = jnp.exp(s - jnp.max(s, axis=-1, keepdims=True))
            p = e / jnp.sum(e, axis=-1, keepdims=True)
            return _dot(p.astype(BF16), vh_ref[0:CTX, sl])
        heads(body)

    @pl.when(step >= nctx)
    def _():
        rows = SEQ // GRID_W
        r = step - nctx
        rs = jnp.clip(r - NA_ROWS // 2, 0, rows - NA_ROWS)
        k0 = pl.multiple_of(CTX + rs * GRID_W, GRID_W)
        nloc = NA_ROWS * GRID_W
        qi = _iota((GRID_W, nloc), 0)
        kc = _iota((GRID_W, nloc), 1) % GRID_W
        cstart = jnp.clip(qi - NA_COLS // 2, 0, GRID_W - NA_COLS)
        ok = jnp.logical_and(kc >= cstart, kc < cstart + NA_COLS)

        def body(h, qs, sl):
            s_loc = _dot_nt(qs, kh_ref[pl.ds(k0, nloc), sl]) + bias_ref[0, h]
            s_loc = jnp.where(ok, s_loc, -jnp.inf)
            s_ctx = _dot_nt(qs, kh_ref[0:CTX, sl])
            mx = jnp.maximum(jnp.max(s_loc, axis=-1, keepdims=True), jnp.max(s_ctx, axis=-1, keepdims=True))
            e_loc = jnp.exp(s_loc - mx)
            e_ctx = jnp.exp(s_ctx - mx)
            den = jnp.sum(e_loc, axis=-1, keepdims=True) + jnp.sum(e_ctx, axis=-1, keepdims=True)
            p_loc = (e_loc / den).astype(BF16)
            p_ctx = (e_ctx / den).astype(BF16)
            return _dot(p_loc, vh_ref[pl.ds(k0, nloc), sl]) + _dot(p_ctx, vh_ref[0:CTX, sl])
        heads(body)


def na_attention(z, bias, qn, kn, layer):
    t = z.shape[0]
    nb = t // ROWS
    steps = ROWS // GRID_W
    nctx = CTX // GRID_W
    rows = SEQ // GRID_W

    def case(i):
        r = jnp.maximum(i - nctx, 0)
        return r - jnp.clip(r - NA_ROWS // 2, 0, rows - NA_ROWS)

    full = lambda shape: pl.BlockSpec(shape, lambda b, i: (0,) * len(shape))
    return pl.pallas_call(
        functools.partial(_na_body, nctx=nctx),
        grid=(nb, steps),
        in_specs=[pl.BlockSpec((GRID_W, BLK), lambda b, i: (b * steps + i, 7)),
                  pl.BlockSpec((ROWS, BLK), lambda b, i: (b, 8)),
                  pl.BlockSpec((ROWS, BLK), lambda b, i: (b, 9)),
                  pl.BlockSpec((None, 1, C_HEADS, GRID_W, NA_ROWS * GRID_W), lambda b, i: (layer, case(i), 0, 0, 0)),
                  full((1, BLK)), full((1, BLK)), full((BLK, BLK))],
        out_specs=pl.BlockSpec((GRID_W, BLK), lambda b, i: (b * steps + i, 0)),
        out_shape=jax.ShapeDtypeStruct((t, BLK), F32),
        scratch_shapes=[pltpu.VMEM((ROWS, BLK), BF16), pltpu.VMEM((ROWS, BLK), BF16)],
        compiler_params=_cp("arbitrary", "arbitrary"),
        name="na_attention",
    )(z, z, z, bias, qn, kn, _seg_ones(512, HEAD))


def _lru_body(x_ref, xp_ref, xn_ref, gate_ref, cw_ref, cb_ref, wa_ref, ba_ref, wx_ref, bx_ref, lam_ref,
              *rest, reverse, last, tpb):
    if last:
        h0_ref, o_ref, carry_ref = rest
    else:
        o_ref, carry_ref = rest
    p = pl.program_id(0) % tpb
    has_prev, has_next = _tile_flags(_lru_tile(p, tpb, reverse), tpb)

    @pl.when(p == 0)
    def _():
        carry_ref[...] = jnp.zeros_like(carry_ref)

    x = x_ref[...]
    n = x.shape[0]
    xm1 = _shift_rows(x, xp_ref[7:8, :], 1, has_prev)
    xp1 = _shift_rows(x, xn_ref[0:1, :], -1, has_next)
    rows = _iota((n, 1), 0)
    nx = jnp.where(has_next, xn_ref[0:2, :], 0.0)
    xp2 = pltpu.roll(x, n - 2, 0)
    xp2 = jnp.where(rows == n - 2, nx[0:1, :], jnp.where(rows == n - 1, nx[1:2, :], xp2))
    xr = (cw_ref[0:1, :] * xm1 + cw_ref[1:2, :] * x + cw_ref[2:3, :] * xp1 + cw_ref[3:4, :] * xp2) + cb_ref[...]
    xb = xr.astype(BF16)
    rg = _sigmoid(_dot(xb, wa_ref[...]) + ba_ref[...])
    ig = _sigmoid(_dot(xb, wx_ref[...]) + bx_ref[...])
    log_a = -RG_C * _softplus(-lam_ref[...]) * rg
    a = jnp.exp(log_a)
    b = jnp.sqrt(jnp.maximum(1.0 - jnp.exp(2.0 * log_a), 0.0)) * ig * xr
    k = 1
    while k < n:
        if reverse:
            keep = rows < n - k
            a_s = pltpu.roll(a, n - k, 0)
            b_s = pltpu.roll(b, n - k, 0)
        else:
            keep = rows >= k
            a_s = pltpu.roll(a, k, 0)
            b_s = pltpu.roll(b, k, 0)
        b = jnp.where(keep, a * b_s + b, b)
        a = jnp.where(keep, a * a_s, a)
        k *= 2
    h = a * carry_ref[...] + b
    carry_ref[...] = h[0:1, :] if reverse else h[n - 1:n, :]
    if last:
        g = gate_ref[...]
        gelu = 0.5 * g * (1.0 + jnp.tanh(math.sqrt(2.0 / math.pi) * (g + 0.044715 * g * g * g)))
        o_ref[...] = (h0_ref[...] + h) * gelu
    else:
        o_ref[...] = h


def _lru_tile(p, tpb, reverse):
    cctx = CTX // TQ
    if reverse:
        return jnp.where(p < cctx, cctx - 1 - p, tpb - 1 + cctx - p)
    return p


def rglru(z, direction, conv_w, conv_b, wa_bd, ba, wx_bd, bx, lam, h0=None):
    t = z.shape[0]
    n_tiles = t // TQ
    tpb = ROWS // TQ
    reverse = direction == 1
    last = h0 is not None

    def blk(i):
        return (i // tpb) * tpb + _lru_tile(i % tpb, tpb, reverse)

    r8 = TQ // 8
    full = lambda shape: pl.BlockSpec(shape, lambda i: (0,) * len(shape))
    in_specs = [pl.BlockSpec((TQ, BLK), lambda i: (blk(i), 10)),
                pl.BlockSpec((8, BLK), lambda i: (jnp.maximum(blk(i) * r8 - 1, 0), 10)),
                pl.BlockSpec((8, BLK), lambda i: (jnp.minimum((blk(i) + 1) * r8, n_tiles * r8 - 1), 10)),
                pl.BlockSpec((TQ, BLK), lambda i: (blk(i), 11)),
                full((D_CONV, BLK)), full((1, BLK)), full((BLK, BLK)), full((1, BLK)),
                full((BLK, BLK)), full((1, BLK)), full((1, BLK))]
    args = [z, z, z, z, conv_w, conv_b, wa_bd, ba, wx_bd, bx, lam]
    if last:
        in_specs.append(pl.BlockSpec((TQ, BLK), lambda i: (blk(i), 0)))
        args.append(h0)

    return pl.pallas_call(
        functools.partial(_lru_body, reverse=reverse, last=last, tpb=tpb),
        grid=(n_tiles,),
        in_specs=in_specs,
        out_specs=pl.BlockSpec((TQ, BLK), lambda i: (blk(i), 0)),
        out_shape=jax.ShapeDtypeStruct((t, BLK), F32),
        scratch_shapes=[pltpu.VMEM((1, BLK), F32)],
        compiler_params=_cp("arbitrary"),
        name="rglru_%d" % direction,
    )(*args)


def _merge_body(hn_ref, wg_ref, ba_ref, bb_ref, bc_ref, bd_ref, wb_ref, y_ref, *, tn):
    g = _dot(hn_ref[...], wg_ref[...])
    acc = None
    for j, br in enumerate((ba_ref, bb_ref, bc_ref, bd_ref)):
        term = _sigmoid(g[:, j * tn:(j + 1) * tn]) * _dot(br[...].astype(BF16), wb_ref[j])
        acc = term if acc is None else acc + term
    y_ref[...] = acc.astype(BF16)


def merge_branches(hn, wg, branches, wb, *, tm=768, tn=256):
    t, d = hn.shape
    br_spec = pl.BlockSpec((tm, BLK), lambda i, j: (i, 0))
    return pl.pallas_call(
        functools.partial(_merge_body, tn=tn),
        grid=(t // tm, d // tn),
        in_specs=[pl.BlockSpec((tm, d), lambda i, j: (i, 0)),
                  pl.BlockSpec((None, d, N_BRANCH * tn), lambda i, j: (j, 0, 0)),
                  br_spec, br_spec, br_spec, br_spec,
                  pl.BlockSpec((N_BRANCH, BLK, tn), lambda i, j: (0, 0, j))],
        out_specs=pl.BlockSpec((tm, tn), lambda i, j: (i, j)),
        out_shape=jax.ShapeDtypeStruct((t, d), BF16),
        compiler_params=_cp("arbitrary", "arbitrary"),
        name="merge_branches",
    )(hn, wg, *branches, wb)


def _post_body(y_ref, wo_ref, x_ref, mod_ref, g2_ref, rw_ref, rb_ref,
               xo_ref, h2_ref, e_ref, w_ref, rank_ref, cnt_ref, base_ref, *, tm, tpb):
    i = pl.program_id(0)
    row0 = (i % tpb) * tm

    @pl.when(i == 0)
    def _():
        base_ref[...] = jnp.zeros_like(base_ref)

    x = x_ref[...] + _mod_rows(mod_ref, 2, row0, tm) * _dot(y_ref[...], wo_ref[...])
    xo_ref[...] = x
    h = _norm_mod(x, g2_ref[...], mod_ref, 3, row0, tm)
    h2_ref[...] = h
    hh, hl = _split(h)
    logits = _dot_nt(rw_ref[0], hh) + _dot_nt(rw_ref[0], hl) + _dot_nt(rw_ref[1], hh)
    s = _sigmoid(logits)
    sb = s + rb_ref[...]
    per = N_EXPERTS // N_GROUPS
    srow = [s[e:e + 1, :] for e in range(N_EXPERTS)]
    brow = [sb[e:e + 1, :] for e in range(N_EXPERTS)]
    gscore = []
    for g in range(N_GROUPS):
        best = None
        for a in range(per):
            for b in range(a + 1, per):
                pair = brow[per * g + a] + brow[per * g + b]
                best = pair if best is None else jnp.maximum(best, pair)
        gscore.append(best)
    gbest = jnp.zeros((1, tm), jnp.int32)
    bscore = gscore[0]
    for g in range(1, N_GROUPS):
        better = gscore[g] > bscore
        gbest = jnp.where(better, g, gbest)
        bscore = jnp.where(better, gscore[g], bscore)

    def pick(rows, i):
        out = rows[i]
        for g in range(1, N_GROUPS):
            out = jnp.where(gbest == g, rows[per * g + i], out)
        return out

    cand = [pick(brow, i) for i in range(per)]
    cval = [pick(srow, i) for i in range(per)]
    i1 = jnp.zeros((1, tm), jnp.int32)
    m1 = cand[0]
    w1 = cval[0]
    for i_ in range(1, per):
        better = cand[i_] > m1
        i1 = jnp.where(better, i_, i1)
        m1 = jnp.where(better, cand[i_], m1)
        w1 = jnp.where(better, cval[i_], w1)
    i2 = jnp.full((1, tm), -1, jnp.int32)
    m2 = jnp.full((1, tm), -jnp.inf, F32)
    w2 = jnp.zeros((1, tm), F32)
    for i_ in range(per):
        better = jnp.logical_and(i1 != i_, cand[i_] > m2)
        i2 = jnp.where(better, i_, i2)
        m2 = jnp.where(better, cand[i_], m2)
        w2 = jnp.where(better, cval[i_], w2)
    e1 = per * gbest + i1
    e2 = per * gbest + i2
    wsum = w1 + w2
    e_ref[0:1, :] = e1
    e_ref[1:2, :] = e2
    w_ref[0:1, :] = w1 / wsum
    w_ref[1:2, :] = w2 / wsum
    er = _iota((N_EXPERTS, tm), 0)
    oh1 = jnp.where(er == e1, 1.0, 0.0)
    oh2 = jnp.where(er == e2, 1.0, 0.0)
    upper = jnp.where(_iota((tm, tm), 0) < _iota((tm, tm), 1), 1.0, 0.0).astype(BF16)
    ex1 = _dot(oh1.astype(BF16), upper)
    ex2 = _dot(oh2.astype(BF16), upper)
    tot1 = jnp.sum(oh1, axis=1, keepdims=True)
    tot2 = jnp.sum(oh2, axis=1, keepdims=True)
    base = base_ref[...]
    rank_ref[0:1, :] = jnp.sum(oh1 * (base + ex1), axis=0, keepdims=True).astype(jnp.int32)
    rank_ref[1:2, :] = jnp.sum(oh2 * (base + tot1 + ex2), axis=0, keepdims=True).astype(jnp.int32)
    base = base + tot1 + tot2
    base_ref[...] = base
    cnt_ref[...] = jnp.broadcast_to(base, cnt_ref.shape)


def post_attention(y, w_out, x, mod, g2, rw, rb, *, tm=256):
    t, d = x.shape
    tpb = ROWS // tm
    full = lambda shape: pl.BlockSpec(shape, lambda i: (0,) * len(shape))
    row = pl.BlockSpec((tm, d), lambda i: (i, 0))
    tok = pl.BlockSpec((2, tm), lambda i: (0, i))
    return pl.pallas_call(
        functools.partial(_post_body, tm=tm, tpb=tpb),
        grid=(t // tm,),
        in_specs=[row, full((d, d)), row, pl.BlockSpec((1, 2, 6, d), lambda i: (i // tpb, 0, 0, 0)),
                  full((1, d)), full((2, N_EXPERTS, d)), full((N_EXPERTS, 1))],
        out_specs=[row, row, tok, tok, tok, full((N_EXPERTS, 128))],
        out_shape=[jax.ShapeDtypeStruct((t, d), F32), jax.ShapeDtypeStruct((t, d), F32),
                   jax.ShapeDtypeStruct((2, t), jnp.int32), jax.ShapeDtypeStruct((2, t), F32),
                   jax.ShapeDtypeStruct((2, t), jnp.int32), jax.ShapeDtypeStruct((N_EXPERTS, 128), F32)],
        scratch_shapes=[pltpu.VMEM((N_EXPERTS, 1), F32)],
        compiler_params=_cp("arbitrary"),
        name="post_attention",
    )(y, w_out, x, mod, g2, rw, rb)


def _row_copy(src, dst, sem):
    return pltpu.make_async_copy(src, dst, sem)


def _dispatch_body(pos_ref, h_ref, xs_in_ref, xs_ref, sem, *, tm):
    del xs_in_ref

    def start(t, carry):
        for k in range(2):
            p = pos_ref[0, 0, k * tm + t]
            _row_copy(h_ref.at[pl.ds(t, 1)], xs_ref.at[pl.ds(p, 1)], sem).start()
        return carry

    lax.fori_loop(0, tm, start, 0)

    def wait(t, carry):
        for k in range(2):
            _row_copy(h_ref.at[pl.ds(0, 1)], xs_ref.at[pl.ds(0, 1)], sem).wait()
        return carry

    lax.fori_loop(0, tm, wait, 0)


def moe_dispatch(pos3, h2, n_rows, *, tm=256):
    t, d = h2.shape
    xs0 = jnp.zeros((n_rows, d), F32)
    return pl.pallas_call(
        functools.partial(_dispatch_body, tm=tm),
        grid=(t // tm,),
        in_specs=[pl.BlockSpec((1, 1, 2 * tm), lambda i: (i, 0, 0), memory_space=pltpu.SMEM),
                  pl.BlockSpec((tm, d), lambda i: (i, 0)),
                  pl.BlockSpec(memory_space=pl.ANY)],
        out_specs=pl.BlockSpec(memory_space=pl.ANY),
        out_shape=jax.ShapeDtypeStruct((n_rows, d), F32),
        scratch_shapes=[pltpu.SemaphoreType.DMA(())],
        input_output_aliases={2: 0},
        compiler_params=_cp("arbitrary"),
        name="moe_dispatch",
    )(pos3, h2, xs0)


def _ffn_body(te_ref, nv_ref, x_ref, wg_ref, wu_ref, wd_ref, y_ref):
    del te_ref
    i = pl.program_id(0)

    @pl.when(i < nv_ref[0])
    def _():
        x = x_ref[...].astype(BF16)
        g = _dot(x, wg_ref[...])
        u = _dot(x, wu_ref[...])
        y_ref[...] = _dot((g * _sigmoid(g) * u).astype(BF16), wd_ref[...])

    @pl.when(i >= nv_ref[0])
    def _():
        y_ref[...] = jnp.zeros_like(y_ref)


def moe_ffn(tile_expert, n_valid, xs, wg, wu, wd):
    n_rows, d = xs.shape
    ff = wg.shape[-1]
    n_tiles = n_rows // FFN_TM
    grid_spec = pltpu.PrefetchScalarGridSpec(
        num_scalar_prefetch=2,
        grid=(n_tiles,),
        in_specs=[pl.BlockSpec((FFN_TM, d), lambda i, te, nv: (jnp.minimum(i, nv[0] - 1), 0)),
                  pl.BlockSpec((None, d, ff), lambda i, te, nv: (te[i], 0, 0)),
                  pl.BlockSpec((None, d, ff), lambda i, te, nv: (te[i], 0, 0)),
                  pl.BlockSpec((None, ff, d), lambda i, te, nv: (te[i], 0, 0))],
        out_specs=pl.BlockSpec((FFN_TM, d), lambda i, te, nv: (i, 0)))
    return pl.pallas_call(
        _ffn_body,
        grid_spec=grid_spec,
        out_shape=jax.ShapeDtypeStruct((n_rows, d), F32),
        compiler_params=_cp("arbitrary"),
        name="moe_ffn",
    )(tile_expert, n_valid, xs, wg, wu, wd)


def _combine_body(pos_ref, ys_ref, x_ref, w_ref, mod_ref, o_ref, buf_ref, sem, *, tm, tpb):
    def start(t, carry):
        for k in range(2):
            p = pos_ref[0, 0, k * tm + t]
            _row_copy(ys_ref.at[pl.ds(p, 1)], buf_ref.at[k, pl.ds(t, 1)], sem).start()
        return carry

    lax.fori_loop(0, tm, start, 0)

    def wait(t, carry):
        for k in range(2):
            _row_copy(ys_ref.at[pl.ds(0, 1)], buf_ref.at[k, pl.ds(0, 1)], sem).wait()
        return carry

    lax.fori_loop(0, tm, wait, 0)
    w = w_ref[...]
    y = w[:, 0:1] * buf_ref[0] + w[:, 1:2] * buf_ref[1]
    row0 = (pl.program_id(0) % tpb) * tm
    o_ref[...] = x_ref[...] + _mod_rows(mod_ref, 5, row0, tm) * y


def moe_combine(pos3, ys, x, wt, mod, *, tm=256):
    t, d = x.shape
    tpb = ROWS // tm
    row = pl.BlockSpec((tm, d), lambda i: (i, 0))
    return pl.pallas_call(
        functools.partial(_combine_body, tm=tm, tpb=tpb),
        grid=(t // tm,),
        in_specs=[pl.BlockSpec((1, 1, 2 * tm), lambda i: (i, 0, 0), memory_space=pltpu.SMEM),
                  pl.BlockSpec(memory_space=pl.ANY), row,
                  pl.BlockSpec((tm, 2), lambda i: (i, 0)),
                  pl.BlockSpec((1, 2, 6, d), lambda i: (i // tpb, 0, 0, 0))],
        out_specs=row,
        out_shape=jax.ShapeDtypeStruct((t, d), F32),
        scratch_shapes=[pltpu.VMEM((2, tm, d), F32), pltpu.SemaphoreType.DMA(())],
        compiler_params=_cp("arbitrary"),
        name="moe_combine",
    )(pos3, ys, x, wt, mod)


def moe_layout(e, rank, counts, *, tm=256):
    t = e.shape[1]
    n_rows_max = -(-(2 * t + N_EXPERTS * (FFN_TM - 1)) // FFN_TM) * FFN_TM
    n_tiles = n_rows_max // FFN_TM
    cnt = counts[:, 0].astype(jnp.int32)
    gsz = ((cnt + FFN_TM - 1) // FFN_TM) * FFN_TM
    end = jnp.cumsum(gsz)
    off = end - gsz
    onehot = e[:, :, None] == jnp.arange(N_EXPERTS, dtype=jnp.int32)
    pos = rank + jnp.sum(jnp.where(onehot, off, 0), axis=-1)
    pos3 = pos.reshape(2, t // tm, tm).transpose(1, 0, 2).reshape(t // tm, 1, 2 * tm)
    n_valid = (end[-1] // FFN_TM).astype(jnp.int32)
    starts = jnp.minimum(jnp.arange(n_tiles, dtype=jnp.int32), n_valid - 1) * FFN_TM
    tile_expert = jnp.sum(starts[:, None] >= end[None, :], axis=1).astype(jnp.int32)
    return pos3, tile_expert, n_valid.reshape(1), n_rows_max


def _block_diag(w):
    n, bi, bj = w.shape[-3:]
    eye = jnp.eye(n, dtype=w.dtype)
    out = w[..., :, :, None, :] * eye[:, None, :, None]
    return out.reshape(w.shape[:-3] + (n * bi, n * bj))


def kernel(x, c, ctx, c_ctx, w_mod, b_mod, norm1_g, norm2_g, w_in, a_mu, a_w0, a_w2, a_a0, a_a2, a_g2, a_kk, a_ka,
           a_rk, a_gn_g, a_gn_b, b_qn, b_kn, b_lam, b_subln, c_qn, c_kn, c_rpb, d_conv_w, d_conv_b, d_wa, d_ba,
           d_wx, d_bx, d_lam, w_branch, w_out, router_w, router_b, e_gate, e_up, e_down):
    nb = x.shape[0]
    d = D_MODEL
    t = nb * ROWS
    xs = jnp.concatenate([ctx, x], axis=1).reshape(t, d)

    cc = jnp.concatenate([c, c_ctx[None], jnp.zeros((-(nb + 1) % 8, d), F32)], axis=0)
    mods = mod_vectors(cc, w_mod, b_mod)
    mod_l = mods[:, :nb].reshape(DEPTH, nb, 1, 6, d)
    mod_c = jnp.broadcast_to(mods[:, nb].reshape(DEPTH, 1, 1, 6, d), (DEPTH, nb, 1, 6, d))
    mod = jnp.concatenate([mod_l, mod_c], axis=2)

    o_b, o_c, o_d, o_g = A_COLS, A_COLS + 1536, A_COLS + 3072, A_COLS + 3072 + 1024
    w_mix = jnp.concatenate([w_in[:, :, :o_b], jnp.zeros((DEPTH, d, 2048 - A_COLS), F32), w_in[:, :, o_b:o_g]],
                            axis=-1).astype(BF16)
    tn_m = 256
    w_gate = w_in[:, :, o_g:].astype(BF16).reshape(DEPTH, d, N_BRANCH, d // tn_m, tn_m)
    w_gate = jnp.transpose(w_gate, (0, 3, 1, 2, 4)).reshape(DEPTH, d // tn_m, d, N_BRANCH * tn_m)
    mu = jnp.pad(a_mu, ((0, 0), (0, 0), (0, 2048 - A_COLS)))
    w2bd = _block_diag(a_w2).astype(BF16)
    a2bd = _block_diag(a_a2).astype(BF16)
    g2 = a_g2.astype(BF16)
    wa_bd = _block_diag(d_wa).astype(BF16)
    wx_bd = _block_diag(d_wx).astype(BF16)
    wb = w_branch.astype(BF16)
    wo = w_out.astype(BF16)
    rw = jnp.stack(_split(router_w.T), axis=0)
    rb = router_b.reshape(N_EXPERTS, 1)
    eg, eu, ed = e_gate.astype(BF16), e_up.astype(BF16), e_down.astype(BF16)
    cos, sin = rope_tables()
    bias = rpb_tables(c_rpb)
    qn_c = jnp.tile(c_qn, (1, C_HEADS)).reshape(DEPTH, 1, BLK)
    kn_c = jnp.tile(c_kn, (1, C_HEADS)).reshape(DEPTH, 1, BLK)

    for l in range(DEPTH):
        lam_init = 0.8 - 0.6 * math.exp(-0.3 * l)
        z, hn = in_projection(xs, mod[l], norm1_g[l].reshape(1, d), w_mix[l])
        prep = rwkv_prepare(z, mu[l], w2bd[l], a2bd[l], g2[l], a_w0[l].reshape(1, 1024), a_a0[l].reshape(1, 1024),
                            a_kk[l].reshape(1, BLK), a_ka[l].reshape(1, BLK), a_rk[l].reshape(1, BLK))
        ya = rwkv_scan(prep, 0)
        oa = rwkv_scan(prep, 1, ya, a_gn_g[l].reshape(1, BLK), a_gn_b[l].reshape(1, BLK))
        ob = diff_attention(z, cos, sin, b_qn[l].reshape(1, 128), b_kn[l].reshape(1, 128), b_lam[l],
                            b_subln[l].reshape(1, 128), lam_init)
        oc = na_attention(z, bias, qn_c[l], kn_c[l], l)
        lru = [(d_conv_w[l], d_conv_b[l].reshape(1, BLK), wa_bd[l, k], d_ba[l, k].reshape(1, BLK), wx_bd[l, k],
                d_bx[l, k].reshape(1, BLK), d_lam[l, k].reshape(1, BLK)) for k in range(2)]
        hd = rglru(z, 0, *lru[0])
        od = rglru(z, 1, *lru[1], h0=hd)
        y = merge_branches(hn, w_gate[l], (oa, ob, oc, od), wb[l], tn=tn_m)
        xs, h2, e, wts, rank, counts = post_attention(y, wo[l], xs, mod[l], norm2_g[l].reshape(1, d), rw, rb)
        pos3, tile_expert, n_valid, n_rows = moe_layout(e, rank, counts)
        xg = moe_dispatch(pos3, h2, n_rows)
        yg = moe_ffn(tile_expert, n_valid, xg, eg[l], eu[l], ed[l])
        xs = moe_combine(pos3, yg, xs, wts.T, mod[l])
    return xs.reshape(nb, ROWS, d)[:, CTX:]
```

```python
import functools
import math

import numpy as np
import jax
import jax.numpy as jnp
from jax import lax
from jax.experimental import pallas as pl
from jax.experimental.pallas import tpu as pltpu

F32 = jnp.float32
BF16 = jnp.bfloat16

D_MODEL = 2048
DEPTH = 4
GRID_W = 64
CTX = 256
SEQ = 2048
ROWS = CTX + SEQ
HEAD = 64
A_WIDTH = 512
A_COLS = 1920
A_GN_EPS = 64e-5
B_HEADS = 4
C_HEADS = 8
NA_ROWS = 8
NA_COLS = 16
D_WIDTH = 512
D_CONV = 4
RG_C = 8.0
N_BRANCH = 4
N_EXPERTS = 16
N_GROUPS = 4
D_FF = 1024
ROPE_BASE = 10000.0
NORM_EPS = 1e-6
Z_COLS = 6144
BLK = 512

VMEM_LIMIT = 56 * 1024 * 1024

CHUNK = 64
TQ = 256
FFN_TM = 512


def _cp(*sem):
    return pltpu.CompilerParams(dimension_semantics=sem, vmem_limit_bytes=VMEM_LIMIT)


def _dot(a, b):
    return jnp.dot(a, b, preferred_element_type=F32)


def _dot_nt(a, b):
    return lax.dot_general(a, b, (((1,), (1,)), ((), ())), preferred_element_type=F32)


def _dot_tn(a, b):
    return lax.dot_general(a, b, (((0,), (0,)), ((), ())), preferred_element_type=F32)


def _split(x):
    hi = x.astype(BF16)
    lo = (x - hi.astype(F32)).astype(BF16)
    return hi, lo


def _dot_x(x, m):
    hi, lo = _split(x)
    return _dot(hi, m) + _dot(lo, m)


def _sigmoid(x):
    return 1.0 / (1.0 + jnp.exp(-x))


def _softplus(x):
    return jnp.maximum(x, 0.0) + jnp.log(1.0 + jnp.exp(-jnp.abs(x)))


def _iota(shape, dim):
    return lax.broadcasted_iota(jnp.int32, shape, dim)


def _seg_ones(width, seg):
    r = np.arange(width)
    return jnp.asarray((r[:, None] // seg) == (r[None, :] // seg), dtype=BF16)


def _mod_body(c_ref, w_ref, b_ref, o_ref):
    c = c_ref[...]
    s = (c * _sigmoid(c)).astype(BF16)
    o_ref[...] = _dot(s, w_ref[...].astype(BF16)) + b_ref[...]


def mod_vectors(cc, w_mod, b_mod):
    depth, d, n = w_mod.shape
    r = cc.shape[0]
    tn = 1536
    return pl.pallas_call(
        _mod_body,
        grid=(depth, n // tn),
        in_specs=[pl.BlockSpec((r, d), lambda l, j: (0, 0)),
                  pl.BlockSpec((None, d, tn), lambda l, j: (l, 0, j)),
                  pl.BlockSpec((None, 1, tn), lambda l, j: (l, 0, j))],
        out_specs=pl.BlockSpec((None, r, tn), lambda l, j: (l, 0, j)),
        out_shape=jax.ShapeDtypeStruct((depth, r, n), F32),
        compiler_params=_cp("arbitrary", "arbitrary"),
        name="mod_vectors",
    )(cc, w_mod, b_mod.reshape(depth, 1, n))


def _mod_rows(mod_ref, idx, row0, tm):
    rows = row0 + _iota((tm, 1), 0)
    return jnp.where(rows < CTX, mod_ref[0, 1, idx:idx + 1, :], mod_ref[0, 0, idx:idx + 1, :])


def _norm_mod(x, g, mod_ref, shift_idx, row0, tm):
    ms = jnp.mean(x * x, axis=-1, keepdims=True)
    y = x * lax.rsqrt(ms + NORM_EPS) * g
    return y * (1.0 + _mod_rows(mod_ref, shift_idx + 1, row0, tm)) + _mod_rows(mod_ref, shift_idx, row0, tm)


def _inproj_body(x_ref, mod_ref, g_ref, w_ref, z_ref, hn_ref, *, tm, tpb, slab):
    i = pl.program_id(0)

    @pl.when(pl.program_id(1) == 0)
    def _():
        for r in range(0, tm, slab):
            row0 = (i % tpb) * tm + r
            hn_ref[r:r + slab, :] = _norm_mod(x_ref[r:r + slab, :], g_ref[...], mod_ref, 0, row0, slab).astype(BF16)

    z_ref[...] = _dot(hn_ref[...], w_ref[...])


def in_projection(x, mod, g, w, *, tm=1152, tn=512):
    t, d = x.shape
    n = w.shape[1]
    tpb = ROWS // tm
    return pl.pallas_call(
        functools.partial(_inproj_body, tm=tm, tpb=tpb, slab=128),
        grid=(t // tm, n // tn),
        in_specs=[pl.BlockSpec((tm, d), lambda i, j: (i, 0)),
                  pl.BlockSpec((1, 2, 6, d), lambda i, j: (i // tpb, 0, 0, 0)),
                  pl.BlockSpec((1, d), lambda i, j: (0, 0)),
                  pl.BlockSpec((d, tn), lambda i, j: (0, j))],
        out_specs=[pl.BlockSpec((tm, tn), lambda i, j: (i, j)),
                   pl.BlockSpec((tm, d), lambda i, j: (i, 0))],
        out_shape=[jax.ShapeDtypeStruct((t, n), F32), jax.ShapeDtypeStruct((t, d), BF16)],
        compiler_params=_cp("arbitrary", "arbitrary"),
        name="in_projection",
    )(x, mod, g, w)


def _tile_flags(ti, tpb):
    ctx_tiles = CTX // TQ
    has_prev = jnp.logical_and(ti != 0, ti != ctx_tiles)
    has_next = jnp.logical_and(ti != ctx_tiles - 1, ti != tpb - 1)
    return has_prev, has_next


def _halo_specs(width, col, n_tiles):
    r8 = TQ // 8
    return [pl.BlockSpec((TQ, width), lambda i: (i, col)),
            pl.BlockSpec((8, width), lambda i: (jnp.maximum(i * r8 - 1, 0), col)),
            pl.BlockSpec((8, width), lambda i: (jnp.minimum((i + 1) * r8, n_tiles * r8 - 1), col))]


def _shift_rows(z, halo_row, k, use_halo):
    n = z.shape[0]
    rows = _iota((n, 1), 0)
    fill = jnp.where(use_halo, halo_row, 0.0)
    if k > 0:
        return jnp.where(rows < k, fill, pltpu.roll(z, k, 0))
    return jnp.where(rows >= n + k, fill, pltpu.roll(z, n + k, 0))


def _rwkv_prep_body(z_ref, zp_ref, zn_ref, mu_ref, w2_ref, a2_ref, g2_ref, w0_ref, a0_ref,
                    kk_ref, ka_ref, rk_ref, seg_ref, o_ref, *, tpb):
    has_prev, has_next = _tile_flags(pl.program_id(0) % tpb, tpb)
    z = z_ref[...]
    prev = _shift_rows(z, zp_ref[7:8, :], 1, has_prev)
    nxt = _shift_rows(z, zn_ref[0:1, :], -1, has_next)
    zs = z + mu_ref[0:1, :] * (prev - z) + mu_ref[1:2, :] * (nxt - z)
    r = zs[:, 0:512]
    k = zs[:, 512:1024]
    v = zs[:, 1024:1536]
    wl = zs[:, 1536:1664]
    al = zs[:, 1664:1792]
    gl = zs[:, 1792:1920]
    seg = seg_ref[...]
    wlog = -_softplus(-(w0_ref[...] + _dot(jnp.tanh(wl).astype(BF16), w2_ref[...]))) - 0.5
    lw = -jnp.exp(wlog)
    a = _sigmoid(a0_ref[...] + _dot(al.astype(BF16), a2_ref[...]))
    g = _dot(_sigmoid(gl).astype(BF16), g2_ref[...])
    kk = k * kk_ref[...]
    kk = kk / jnp.maximum(jnp.sqrt(_dot_x(kk * kk, seg)), 1e-12)
    bonus = _dot_x(r * k * rk_ref[...], seg) * v
    ka = ka_ref[...]
    o_ref[:, 0:512] = r
    o_ref[:, 512:1024] = v
    o_ref[:, 1024:1536] = kk
    o_ref[:, 1536:2048] = bonus
    o_ref[:, 2048:2560] = g
    for d in range(2):
        ad = a[:, d * 512:(d + 1) * 512]
        base = 2560 + d * 1536
        o_ref[:, base:base + 512] = lw[:, d * 512:(d + 1) * 512]
        o_ref[:, base + 512:base + 1024] = kk * ad
        o_ref[:, base + 1024:base + 1536] = k * (1.0 + (ad - 1.0) * ka)


A_PREP_COLS = 2560 + 2 * 1536


def rwkv_prepare(z, mu, w2bd, a2bd, g2, w0, a0, k_k, k_a, r_k):
    t = z.shape[0]
    n_tiles = t // TQ
    tpb = ROWS // TQ
    full = lambda shape: pl.BlockSpec(shape, lambda i: (0,) * len(shape))
    return pl.pallas_call(
        functools.partial(_rwkv_prep_body, tpb=tpb),
        grid=(n_tiles,),
        in_specs=_halo_specs(2048, 0, n_tiles) + [
            full((2, 2048)), full((128, 1024)), full((128, 1024)), full((128, 512)),
            full((1, 1024)), full((1, 1024)), full((1, 512)), full((1, 512)), full((1, 512)),
            full((512, 512))],
        out_specs=pl.BlockSpec((TQ, A_PREP_COLS), lambda i: (i, 0)),
        out_shape=jax.ShapeDtypeStruct((t, A_PREP_COLS), F32),
        compiler_params=_cp("arbitrary"),
        name="rwkv_prepare",
    )(z, z, z, mu, w2bd, a2bd, g2, w0, a0, k_k, k_a, r_k, _seg_ones(512, HEAD))


def _rwkv_scan_body(r_ref, v_ref, kk_ref, lw_ref, b_ref, ke_ref, *rest, reverse, last, nbb):
    if last:
        y0_ref, bonus_ref, g_ref, gng_ref, gnb_ref, seg_ref, o_ref, s_ref = rest
    else:
        o_ref, s_ref = rest
    c = CHUNK
    nh = A_WIDTH // HEAD

    @pl.when(pl.program_id(1) == 0)
    def _():
        s_ref[...] = jnp.zeros_like(s_ref)

    ti = _iota((c, c), 0)
    si = _iota((c, c), 1)
    before = (si > ti) if reverse else (si < ti)
    upto = jnp.logical_or(before, si == ti)
    tri = jnp.where(upto, 1.0, 0.0).astype(BF16)
    eye = jnp.where(si == ti, 1.0, 0.0)
    pair = (ti >> 1) == (si >> 1)
    off_masks = []
    lg = 1
    while (2 << lg) <= c:
        off_masks.append(jnp.logical_and((ti >> (lg + 1)) == (si >> (lg + 1)), (ti >> lg) != (si >> lg)))
        lg += 1
    ti2 = _iota((c, 2 * c), 0)
    si2 = _iota((c, 2 * c), 1) % c
    upto2 = (si2 >= ti2) if reverse else (si2 <= ti2)

    chains = [(bi, h) for bi in range(nbb) for h in range(nh)]
    cols = {}
    gammas = []
    for bi in range(nbb):
        lw = lw_ref[bi]
        lw_hi, lw_lo = _split(lw)
        cs = _dot(tri, lw_hi) + _dot(tri, lw_lo)
        tot = jnp.sum(lw, axis=0, keepdims=True)
        e_neg = jnp.exp(-cs)
        e_hat = jnp.exp(tot - cs)
        gammas.append(jnp.exp(tot))
        b = b_ref[bi]
        ke = ke_ref[bi]
        mats = dict(r=r_ref[bi] * jnp.exp(cs), a=-kk_ref[bi] * jnp.exp(cs - lw), b=b * e_neg, k=ke * e_neg,
                    bh=b * e_hat, kh=ke * e_hat, v=v_ref[bi])
        for name, m in mats.items():
            for h in range(nh):
                cols[name, bi, h] = m[:, h * HEAD:(h + 1) * HEAD].astype(BF16)

    def per_chain(fn):
        return [fn(i, bi, h) for i, (bi, h) in enumerate(chains)]

    lab = per_chain(lambda i, bi, h: jnp.where(before, _dot_nt(cols['a', bi, h], cols['b', bi, h]), 0.0))
    lak = per_chain(lambda i, bi, h: jnp.where(before, _dot_nt(cols['a', bi, h], cols['k', bi, h]), 0.0))
    m = per_chain(lambda i, bi, h: jnp.where(
        upto2, _dot_nt(cols['r', bi, h], jnp.concatenate([cols['b', bi, h], cols['k', bi, h]], axis=0)),
        0.0).astype(BF16))
    tinv = per_chain(lambda i, bi, h: eye + jnp.where(pair, lab[i], 0.0))
    for off_mask in off_masks:
        tb = [t.astype(BF16) for t in tinv]
        x = per_chain(lambda i, bi, h: _dot(jnp.where(off_mask, lab[i], 0.0).astype(BF16), tb[i]))
        tinv = per_chain(lambda i, bi, h: tinv[i] + _dot(tb[i], x[i].astype(BF16)))
    lv = per_chain(lambda i, bi, h: _dot(lak[i].astype(BF16), cols['v', bi, h]))
    tb = [t.astype(BF16) for t in tinv]
    w1 = per_chain(lambda i, bi, h: _dot(tb[i], cols['a', bi, h]))
    u2 = per_chain(lambda i, bi, h: _dot(tb[i], lv[i].astype(BF16)))
    s0 = per_chain(lambda i, bi, h: s_ref[bi, h])
    s0b = [s.astype(BF16) for s in s0]
    u = per_chain(lambda i, bi, h: _dot_nt(w1[i].astype(BF16), s0b[i]) + u2[i])
    uv = per_chain(lambda i, bi, h: jnp.concatenate([u[i].astype(BF16), cols['v', bi, h]], axis=0))
    ys = per_chain(lambda i, bi, h: _dot_nt(cols['r', bi, h], s0b[i]) + _dot(m[i], uv[i]))
    for i, (bi, h) in enumerate(chains):
        bkh = jnp.concatenate([cols['bh', bi, h], cols['kh', bi, h]], axis=0)
        s_ref[bi, h] = s0[i] * gammas[bi][:, h * HEAD:(h + 1) * HEAD] + _dot_tn(uv[i], bkh)
    for bi in range(nbb):
        y = jnp.concatenate(ys[bi * nh:(bi + 1) * nh], axis=1)
        if last:
            y = y + y0_ref[bi]
            seg = seg_ref[...]
            inv = 1.0 / HEAD
            mean = _dot_x(y, seg) * inv
            yc = y - mean
            var = _dot_x(yc * yc, seg) * inv
            yn = yc * lax.rsqrt(var + A_GN_EPS) * gng_ref[...] + gnb_ref[...]
            o_ref[bi] = (yn + bonus_ref[bi]) * g_ref[bi]
        else:
            o_ref[bi] = y


def rwkv_scan(prep, direction, y0=None, gn_g=None, gn_b=None, *, nbb=2):
    t = prep.shape[0]
    nb = t // ROWS
    cpb = ROWS // CHUNK
    cctx = CTX // CHUNK
    reverse = direction == 1
    last = y0 is not None
    prep3 = prep.reshape(nb, ROWS, prep.shape[1])

    def blk(p):
        if reverse:
            return jnp.where(p < cctx, cctx - 1 - p, cpb - 1 + cctx - p)
        return p

    def col(cb):
        return pl.BlockSpec((nbb, CHUNK, BLK), lambda g, p: (g, blk(p), cb))

    base = 5 + 3 * direction
    in_specs = [col(0), col(1), col(2), col(base), col(base + 1), col(base + 2)]
    args = [prep3] * 6
    if last:
        in_specs += [col(0), col(3), col(4),
                     pl.BlockSpec((1, BLK), lambda g, p: (0, 0)), pl.BlockSpec((1, BLK), lambda g, p: (0, 0)),
                     pl.BlockSpec((BLK, BLK), lambda g, p: (0, 0))]
        args += [y0.reshape(nb, ROWS, BLK), prep3, prep3, gn_g, gn_b, _seg_ones(512, HEAD)]
    out = pl.pallas_call(
        functools.partial(_rwkv_scan_body, reverse=reverse, last=last, nbb=nbb),
        grid=(nb // nbb, cpb),
        in_specs=in_specs,
        out_specs=col(0),
        out_shape=jax.ShapeDtypeStruct((nb, ROWS, BLK), F32),
        scratch_shapes=[pltpu.VMEM((nbb, A_WIDTH // HEAD, HEAD, HEAD), F32)],
        compiler_params=_cp("arbitrary", "arbitrary"),
        name="rwkv_scan_%d" % direction,
    )(*args)
    return out.reshape(t, BLK)


def _rope(x, cos, sin):
    lane = _iota(x.shape, 1)
    partner = jnp.where((lane % HEAD) < HEAD // 2, pltpu.roll(x, 128 - HEAD // 2, 1), pltpu.roll(x, HEAD // 2, 1))
    return x * cos + partner * sin


def _head_rms(x, gain, seg):
    ms = _dot_x(x * x, seg) * (1.0 / HEAD)
    return x * lax.rsqrt(ms + NORM_EPS) * gain


def _diff_body(q_ref, k_ref, v_ref, cos_ref, sin_ref, qn_ref, kn_ref, lam_ref, sub_ref, seg_ref,
               o_ref, kh_ref, vh_ref, *, lam_init, nq):
    iq = pl.program_id(2)
    seg = seg_ref[...]

    @pl.when(iq == 0)
    def _():
        kh = _rope(_head_rms(k_ref[...], kn_ref[...], seg), cos_ref[...], sin_ref[...])
        kh_ref[...] = kh.astype(BF16)
        vh_ref[...] = v_ref[...].astype(BF16)

    lv = lam_ref[...]
    lam = (jnp.exp(jnp.sum(lv[0:1] * lv[1:2], axis=1, keepdims=True))
           - jnp.exp(jnp.sum(lv[2:3] * lv[3:4], axis=1, keepdims=True)) + lam_init)
    r0 = pl.multiple_of(iq * TQ, TQ)
    q = _rope(_head_rms(q_ref[...], qn_ref[...], seg), cos_ref[pl.ds(r0, TQ), :], sin_ref[pl.ds(r0, TQ), :])
    q = q * (HEAD ** -0.5 * math.log2(math.e))
    lane = _iota(q.shape, 1)
    q1 = jnp.where(lane < HEAD, q, 0.0).astype(BF16)
    q2 = jnp.where(lane >= HEAD, q, 0.0).astype(BF16)

    def attend(kh, vh):
        def soft_v(qs):
            s = _dot_nt(qs, kh)
            e = jnp.exp2(s - jnp.max(s, axis=-1, keepdims=True))
            return _dot(e.astype(BF16), vh) / jnp.sum(e, axis=-1, keepdims=True)
        o = soft_v(q1) - lam * soft_v(q2)
        ms = jnp.mean(o * o, axis=-1, keepdims=True)
        o_ref[...] = o * lax.rsqrt(ms + NORM_EPS) * sub_ref[...] * (1.0 - lam_init)

    @pl.when(iq == 0)
    def _():
        attend(kh_ref[0:CTX, :], vh_ref[0:CTX, :])

    @pl.when(iq != 0)
    def _():
        attend(kh_ref[...], vh_ref[...])


def diff_attention(z, cos, sin, qn, kn, lam_vecs, subln, lam_init):
    t = z.shape[0]
    nb = t // ROWS
    nq = ROWS // TQ
    cb = BLK // 128
    full = lambda shape: pl.BlockSpec(shape, lambda b, h, i: (0,) * len(shape))
    return pl.pallas_call(
        functools.partial(_diff_body, lam_init=lam_init, nq=nq),
        grid=(nb, B_HEADS, nq),
        in_specs=[pl.BlockSpec((TQ, 128), lambda b, h, i: (b * nq + i, 4 * cb + h)),
                  pl.BlockSpec((ROWS, 128), lambda b, h, i: (b, 5 * cb + h)),
                  pl.BlockSpec((ROWS, 128), lambda b, h, i: (b, 6 * cb + h)),
                  full((ROWS, 128)), full((ROWS, 128)), full((1, 128)), full((1, 128)),
                  full((4, HEAD)), full((1, 128)), full((128, 128))],
        out_specs=pl.BlockSpec((TQ, 128), lambda b, h, i: (b * nq + i, h)),
        out_shape=jax.ShapeDtypeStruct((t, BLK), F32),
        scratch_shapes=[pltpu.VMEM((ROWS, 128), BF16), pltpu.VMEM((ROWS, 128), BF16)],
        compiler_params=_cp("arbitrary", "arbitrary", "arbitrary"),
        name="diff_attention",
    )(z, z, z, cos, sin, qn, kn, lam_vecs, subln, _seg_ones(128, HEAD))


def rope_tables():
    t = np.arange(SEQ)
    quarter = HEAD // 4
    inv = (1.0 / (ROPE_BASE ** (np.arange(quarter, dtype=np.float32) / quarter))).astype(np.float32)
    row = (t // GRID_W).astype(np.float32)[:, None] * inv
    col = (t % GRID_W).astype(np.float32)[:, None] * inv
    ang = np.concatenate([row, col], axis=-1)
    cos = np.concatenate([np.ones((CTX, HEAD // 2), np.float32), np.cos(ang)], axis=0)
    sin = np.concatenate([np.zeros((CTX, HEAD // 2), np.float32), np.sin(ang)], axis=0)
    cos = np.tile(cos, (1, 4))
    sin = np.tile(np.concatenate([-sin, sin], axis=1), (1, 2))
    return jnp.asarray(cos, F32), jnp.asarray(sin, F32)


def _rpb_body(r_ref, e_ref, o_ref):
    o_ref[...] = _dot_x(r_ref[...], e_ref[...])


def rpb_tables(rpb):
    depth = rpb.shape[0]
    nr, nc = 2 * NA_ROWS - 1, 2 * NA_COLS - 1
    q = np.arange(GRID_W)
    dc = np.clip(q[None, :] - q[:, None] + NA_COLS - 1, 0, nc - 1)
    onehot = (np.arange(32)[:, None, None] == dc[None]).astype(np.float32).reshape(32, GRID_W * GRID_W)
    r2 = jnp.pad(rpb.reshape(depth * C_HEADS * nr, nc), ((0, 0), (0, 32 - nc)))
    rows = r2.shape[0]
    toe = pl.pallas_call(
        _rpb_body,
        grid=(1,),
        in_specs=[pl.BlockSpec((rows, 32), lambda i: (0, 0)), pl.BlockSpec((32, GRID_W * GRID_W), lambda i: (0, 0))],
        out_specs=pl.BlockSpec((rows, GRID_W * GRID_W), lambda i: (0, 0)),
        out_shape=jax.ShapeDtypeStruct((rows, GRID_W * GRID_W), F32),
        compiler_params=_cp("arbitrary"),
        name="rpb_tables",
    )(r2, jnp.asarray(onehot, BF16))
    toe = toe.reshape(depth, C_HEADS, nr, GRID_W, GRID_W)
    cases = [toe[:, :, NA_ROWS - 1 - d:2 * NA_ROWS - 1 - d] for d in range(NA_ROWS)]
    tab = jnp.stack(cases, axis=1)
    tab = jnp.transpose(tab, (0, 1, 2, 4, 3, 5))
    return tab.reshape(depth, NA_ROWS, C_HEADS, GRID_W, NA_ROWS * GRID_W)


def _na_body(q_ref, k_ref, v_ref, bias_ref, qn_ref, kn_ref, seg_ref, o_ref, kh_ref, vh_ref, *, nctx):
    step = pl.program_id(1)
    seg = seg_ref[...]

    @pl.when(step == 0)
    def _():
        kh_ref[...] = _head_rms(k_ref[...], kn_ref[...], seg).astype(BF16)
        vh_ref[...] = v_ref[...].astype(BF16)

    q = _head_rms(q_ref[...], qn_ref[...], seg) * (HEAD ** -0.5)
    lane = _iota((GRID_W, 128), 1)
    low = lane < HEAD

    hs = list(range(C_HEADS))
    sls = [slice((h // 2) * 128, (h // 2 + 1) * 128) for h in hs]
    qs = [jnp.where(low if h % 2 == 0 else jnp.logical_not(low), q[:, sls[h]], 0.0).astype(BF16) for h in hs]

    def store(outs):
        for hp in range(C_HEADS // 2):
            o_ref[:, sls[2 * hp]] = jnp.where(low, outs[2 * hp], outs[2 * hp + 1])

    @pl.when(step < nctx)
    def _():
        s = [_dot_nt(qs[h], kh_ref[0:CTX, sls[h]]) for h in hs]
        e = [jnp.exp(s[h] - jnp.max(s[h], axis=-1, keepdims=True)) for h in hs]
        p = [(e[h] / jnp.sum(e[h], axis=-1, keepdims=True)).astype(BF16) for h in hs]
        store([_dot(p[h], vh_ref[0:CTX, sls[h]]) for h in hs])

    @pl.when(step >= nctx)
    def _():
        rows = SEQ // GRID_W
        r = step - nctx
        rs = jnp.clip(r - NA_ROWS // 2, 0, rows - NA_ROWS)
        k0 = pl.multiple_of(CTX + rs * GRID_W, GRID_W)
        nloc = NA_ROWS * GRID_W
        qi = _iota((GRID_W, nloc), 0)
        kc = _iota((GRID_W, nloc), 1) % GRID_W
        cstart = jnp.clip(qi - NA_COLS // 2, 0, GRID_W - NA_COLS)
        ok = jnp.logical_and(kc >= cstart, kc < cstart + NA_COLS)
        s_loc = [jnp.where(ok, _dot_nt(qs[h], kh_ref[pl.ds(k0, nloc), sls[h]]) + bias_ref[0, h], -jnp.inf)
                 for h in hs]
        s_ctx = [_dot_nt(qs[h], kh_ref[0:CTX, sls[h]]) for h in hs]
        mx = [jnp.maximum(jnp.max(s_loc[h], axis=-1, keepdims=True), jnp.max(s_ctx[h], axis=-1, keepdims=True))
              for h in hs]
        e_loc = [jnp.exp(s_loc[h] - mx[h]) for h in hs]
        e_ctx = [jnp.exp(s_ctx[h] - mx[h]) for h in hs]
        den = [jnp.sum(e_loc[h], axis=-1, keepdims=True) + jnp.sum(e_ctx[h], axis=-1, keepdims=True) for h in hs]
        store([_dot((e_loc[h] / den[h]).astype(BF16), vh_ref[pl.ds(k0, nloc), sls[h]])
               + _dot((e_ctx[h] / den[h]).astype(BF16), vh_ref[0:CTX, sls[h]]) for h in hs])


def na_attention(z, bias, qn, kn, layer):
    t = z.shape[0]
    nb = t // ROWS
    steps = ROWS // GRID_W
    nctx = CTX // GRID_W
    rows = SEQ // GRID_W

    def case(i):
        r = jnp.maximum(i - nctx, 0)
        return r - jnp.clip(r - NA_ROWS // 2, 0, rows - NA_ROWS)

    full = lambda shape: pl.BlockSpec(shape, lambda b, i: (0,) * len(shape))
    return pl.pallas_call(
        functools.partial(_na_body, nctx=nctx),
        grid=(nb, steps),
        in_specs=[pl.BlockSpec((GRID_W, BLK), lambda b, i: (b * steps + i, 7)),
                  pl.BlockSpec((ROWS, BLK), lambda b, i: (b, 8)),
                  pl.BlockSpec((ROWS, BLK), lambda b, i: (b, 9)),
                  pl.BlockSpec((None, 1, C_HEADS, GRID_W, NA_ROWS * GRID_W), lambda b, i: (layer, case(i), 0, 0, 0)),
                  full((1, BLK)), full((1, BLK)), full((BLK, BLK))],
        out_specs=pl.BlockSpec((GRID_W, BLK), lambda b, i: (b * steps + i, 0)),
        out_shape=jax.ShapeDtypeStruct((t, BLK), F32),
        scratch_shapes=[pltpu.VMEM((ROWS, BLK), BF16), pltpu.VMEM((ROWS, BLK), BF16)],
        compiler_params=_cp("arbitrary", "arbitrary"),
        name="na_attention",
    )(z, z, z, bias, qn, kn, _seg_ones(512, HEAD))


def _lru_body(x_ref, xp_ref, xn_ref, gate_ref, cw_ref, cb_ref, wa_ref, ba_ref, wx_ref, bx_ref, lam_ref,
              *rest, reverse, last, tpb):
    if last:
        h0_ref, o_ref, carry_ref = rest
    else:
        o_ref, carry_ref = rest
    p = pl.program_id(0) % tpb
    has_prev, has_next = _tile_flags(_lru_tile(p, tpb, reverse), tpb)

    @pl.when(p == 0)
    def _():
        carry_ref[...] = jnp.zeros_like(carry_ref)

    x = x_ref[...]
    n = x.shape[0]
    xm1 = _shift_rows(x, xp_ref[7:8, :], 1, has_prev)
    xp1 = _shift_rows(x, xn_ref[0:1, :], -1, has_next)
    rows = _iota((n, 1), 0)
    nx = jnp.where(has_next, xn_ref[0:2, :], 0.0)
    xp2 = pltpu.roll(x, n - 2, 0)
    xp2 = jnp.where(rows == n - 2, nx[0:1, :], jnp.where(rows == n - 1, nx[1:2, :], xp2))
    xr = (cw_ref[0:1, :] * xm1 + cw_ref[1:2, :] * x + cw_ref[2:3, :] * xp1 + cw_ref[3:4, :] * xp2) + cb_ref[...]
    xb = xr.astype(BF16)
    rg = _sigmoid(_dot(xb, wa_ref[...]) + ba_ref[...])
    ig = _sigmoid(_dot(xb, wx_ref[...]) + bx_ref[...])
    log_a = -RG_C * _softplus(-lam_ref[...]) * rg
    a = jnp.exp(log_a)
    b = jnp.sqrt(jnp.maximum(1.0 - jnp.exp(2.0 * log_a), 0.0)) * ig * xr
    k = 1
    while k < n:
        if reverse:
            keep = rows < n - k
            a_s = pltpu.roll(a, n - k, 0)
            b_s = pltpu.roll(b, n - k, 0)
        else:
            keep = rows >= k
            a_s = pltpu.roll(a, k, 0)
            b_s = pltpu.roll(b, k, 0)
        b = jnp.where(keep, a * b_s + b, b)
        a = jnp.where(keep, a * a_s, a)
        k *= 2
    h = a * carry_ref[...] + b
    carry_ref[...] = h[0:1, :] if reverse else h[n - 1:n, :]
    if last:
        g = gate_ref[...]
        gelu = 0.5 * g * (1.0 + jnp.tanh(math.sqrt(2.0 / math.pi) * (g + 0.044715 * g * g * g)))
        o_ref[...] = (h0_ref[...] + h) * gelu
    else:
        o_ref[...] = h


def _lru_tile(p, tpb, reverse):
    cctx = CTX // TQ
    if reverse:
        return jnp.where(p < cctx, cctx - 1 - p, tpb - 1 + cctx - p)
    return p


def rglru(z, direction, conv_w, conv_b, wa_bd, ba, wx_bd, bx, lam, h0=None):
    t = z.shape[0]
    n_tiles = t // TQ
    tpb = ROWS // TQ
    reverse = direction == 1
    last = h0 is not None

    def blk(i):
        return (i // tpb) * tpb + _lru_tile(i % tpb, tpb, reverse)

    r8 = TQ // 8
    full = lambda shape: pl.BlockSpec(shape, lambda i: (0,) * len(shape))
    in_specs = [pl.BlockSpec((TQ, BLK), lambda i: (blk(i), 10)),
                pl.BlockSpec((8, BLK), lambda i: (jnp.maximum(blk(i) * r8 - 1, 0), 10)),
                pl.BlockSpec((8, BLK), lambda i: (jnp.minimum((blk(i) + 1) * r8, n_tiles * r8 - 1), 10)),
                pl.BlockSpec((TQ, BLK), lambda i: (blk(i), 11)),
                full((D_CONV, BLK)), full((1, BLK)), full((BLK, BLK)), full((1, BLK)),
                full((BLK, BLK)), full((1, BLK)), full((1, BLK))]
    args = [z, z, z, z, conv_w, conv_b, wa_bd, ba, wx_bd, bx, lam]
    if last:
        in_specs.append(pl.BlockSpec((TQ, BLK), lambda i: (blk(i), 0)))
        args.append(h0)

    return pl.pallas_call(
        functools.partial(_lru_body, reverse=reverse, last=last, tpb=tpb),
        grid=(n_tiles,),
        in_specs=in_specs,
        out_specs=pl.BlockSpec((TQ, BLK), lambda i: (blk(i), 0)),
        out_shape=jax.ShapeDtypeStruct((t, BLK), F32),
        scratch_shapes=[pltpu.VMEM((1, BLK), F32)],
        compiler_params=_cp("arbitrary"),
        name="rglru_%d" % direction,
    )(*args)


def _merge_body(hn_ref, wg_ref, ba_ref, bb_ref, bc_ref, bd_ref, wb_ref, y_ref, *, tn):
    g = _dot(hn_ref[...], wg_ref[...])
    acc = None
    for j, br in enumerate((ba_ref, bb_ref, bc_ref, bd_ref)):
        term = _sigmoid(g[:, j * tn:(j + 1) * tn]) * _dot(br[...].astype(BF16), wb_ref[j])
        acc = term if acc is None else acc + term
    y_ref[...] = acc.astype(BF16)


def merge_branches(hn, wg, branches, wb, *, tm=768, tn=256):
    t, d = hn.shape
    br_spec = pl.BlockSpec((tm, BLK), lambda i, j: (i, 0))
    return pl.pallas_call(
        functools.partial(_merge_body, tn=tn),
        grid=(t // tm, d // tn),
        in_specs=[pl.BlockSpec((tm, d), lambda i, j: (i, 0)),
                  pl.BlockSpec((None, d, N_BRANCH * tn), lambda i, j: (j, 0, 0)),
                  br_spec, br_spec, br_spec, br_spec,
                  pl.BlockSpec((N_BRANCH, BLK, tn), lambda i, j: (0, 0, j))],
        out_specs=pl.BlockSpec((tm, tn), lambda i, j: (i, j)),
        out_shape=jax.ShapeDtypeStruct((t, d), BF16),
        compiler_params=_cp("arbitrary", "arbitrary"),
        name="merge_branches",
    )(hn, wg, *branches, wb)


def _post_body(y_ref, wo_ref, x_ref, mod_ref, g2_ref, rw_ref, rb_ref,
               xo_ref, h2_ref, e_ref, w_ref, rank_ref, cnt_ref, base_ref, *, tm, tpb):
    i = pl.program_id(0)
    row0 = (i % tpb) * tm

    @pl.when(i == 0)
    def _():
        base_ref[...] = jnp.zeros_like(base_ref)

    x = x_ref[...] + _mod_rows(mod_ref, 2, row0, tm) * _dot(y_ref[...], wo_ref[...])
    xo_ref[...] = x
    h = _norm_mod(x, g2_ref[...], mod_ref, 3, row0, tm)
    h2_ref[...] = h
    hh, hl = _split(h)
    logits = _dot_nt(rw_ref[0], hh) + _dot_nt(rw_ref[0], hl) + _dot_nt(rw_ref[1], hh)
    s = _sigmoid(logits)
    sb = s + rb_ref[...]
    per = N_EXPERTS // N_GROUPS
    srow = [s[e:e + 1, :] for e in range(N_EXPERTS)]
    brow = [sb[e:e + 1, :] for e in range(N_EXPERTS)]
    gscore = []
    for g in range(N_GROUPS):
        best = None
        for a in range(per):
            for b in range(a + 1, per):
                pair = brow[per * g + a] + brow[per * g + b]
                best = pair if best is None else jnp.maximum(best, pair)
        gscore.append(best)
    gbest = jnp.zeros((1, tm), jnp.int32)
    bscore = gscore[0]
    for g in range(1, N_GROUPS):
        better = gscore[g] > bscore
        gbest = jnp.where(better, g, gbest)
        bscore = jnp.where(better, gscore[g], bscore)

    def pick(rows, i):
        out = rows[i]
        for g in range(1, N_GROUPS):
            out = jnp.where(gbest == g, rows[per * g + i], out)
        return out

    cand = [pick(brow, i) for i in range(per)]
    cval = [pick(srow, i) for i in range(per)]
    i1 = jnp.zeros((1, tm), jnp.int32)
    m1 = cand[0]
    w1 = cval[0]
    for i_ in range(1, per):
        better = cand[i_] > m1
        i1 = jnp.where(better, i_, i1)
        m1 = jnp.where(better, cand[i_], m1)
        w1 = jnp.where(better, cval[i_], w1)
    i2 = jnp.full((1, tm), -1, jnp.int32)
    m2 = jnp.full((1, tm), -jnp.inf, F32)
    w2 = jnp.zeros((1, tm), F32)
    for i_ in range(per):
        better = jnp.logical_and(i1 != i_, cand[i_] > m2)
        i2 = jnp.where(better, i_, i2)
        m2 = jnp.where(better, cand[i_], m2)
        w2 = jnp.where(better, cval[i_], w2)
    e1 = per * gbest + i1
    e2 = per * gbest + i2
    wsum = w1 + w2
    e_ref[0:1, :] = e1
    e_ref[1:2, :] = e2
    w_ref[0:1, :] = w1 / wsum
    w_ref[1:2, :] = w2 / wsum
    er = _iota((N_EXPERTS, tm), 0)
    oh1 = jnp.where(er == e1, 1.0, 0.0)
    oh2 = jnp.where(er == e2, 1.0, 0.0)
    upper = jnp.where(_iota((tm, tm), 0) < _iota((tm, tm), 1), 1.0, 0.0).astype(BF16)
    ex1 = _dot(oh1.astype(BF16), upper)
    ex2 = _dot(oh2.astype(BF16), upper)
    tot1 = jnp.sum(oh1, axis=1, keepdims=True)
    tot2 = jnp.sum(oh2, axis=1, keepdims=True)
    base = base_ref[...]
    rank_ref[0:1, :] = jnp.sum(oh1 * (base + ex1), axis=0, keepdims=True).astype(jnp.int32)
    rank_ref[1:2, :] = jnp.sum(oh2 * (base + tot1 + ex2), axis=0, keepdims=True).astype(jnp.int32)
    base = base + tot1 + tot2
    base_ref[...] = base
    cnt_ref[...] = jnp.broadcast_to(base, cnt_ref.shape)


def post_attention(y, w_out, x, mod, g2, rw, rb, *, tm=256):
    t, d = x.shape
    tpb = ROWS // tm
    full = lambda shape: pl.BlockSpec(shape, lambda i: (0,) * len(shape))
    row = pl.BlockSpec((tm, d), lambda i: (i, 0))
    tok = pl.BlockSpec((2, tm), lambda i: (0, i))
    return pl.pallas_call(
        functools.partial(_post_body, tm=tm, tpb=tpb),
        grid=(t // tm,),
        in_specs=[row, full((d, d)), row, pl.BlockSpec((1, 2, 6, d), lambda i: (i // tpb, 0, 0, 0)),
                  full((1, d)), full((2, N_EXPERTS, d)), full((N_EXPERTS, 1))],
        out_specs=[row, row, tok, tok, tok, full((N_EXPERTS, 128))],
        out_shape=[jax.ShapeDtypeStruct((t, d), F32), jax.ShapeDtypeStruct((t, d), F32),
                   jax.ShapeDtypeStruct((2, t), jnp.int32), jax.ShapeDtypeStruct((2, t), F32),
                   jax.ShapeDtypeStruct((2, t), jnp.int32), jax.ShapeDtypeStruct((N_EXPERTS, 128), F32)],
        scratch_shapes=[pltpu.VMEM((N_EXPERTS, 1), F32)],
        compiler_params=_cp("arbitrary"),
        name="post_attention",
    )(y, w_out, x, mod, g2, rw, rb)


def _row_copy(src, dst, sem):
    return pltpu.make_async_copy(src, dst, sem)


def _dispatch_body(pos_ref, h_ref, xs_in_ref, xs_ref, sem, *, tm):
    del xs_in_ref

    def start(t, carry):
        for k in range(2):
            p = pos_ref[0, 0, k * tm + t]
            _row_copy(h_ref.at[pl.ds(t, 1)], xs_ref.at[pl.ds(p, 1)], sem).start()
        return carry

    lax.fori_loop(0, tm, start, 0, unroll=8)
    for k in range(2):
        _row_copy(h_ref, xs_ref.at[pl.ds(0, tm)], sem).wait()


def moe_dispatch(pos3, h2, n_rows, *, tm=256):
    t, d = h2.shape
    xs0 = jnp.zeros((n_rows, d), F32)
    return pl.pallas_call(
        functools.partial(_dispatch_body, tm=tm),
        grid=(t // tm,),
        in_specs=[pl.BlockSpec((1, 1, 2 * tm), lambda i: (i, 0, 0), memory_space=pltpu.SMEM),
                  pl.BlockSpec((tm, d), lambda i: (i, 0)),
                  pl.BlockSpec(memory_space=pl.ANY)],
        out_specs=pl.BlockSpec(memory_space=pl.ANY),
        out_shape=jax.ShapeDtypeStruct((n_rows, d), F32),
        scratch_shapes=[pltpu.SemaphoreType.DMA(())],
        input_output_aliases={2: 0},
        compiler_params=_cp("arbitrary"),
        name="moe_dispatch",
    )(pos3, h2, xs0)


def _ffn_body(te_ref, nv_ref, x_ref, wg_ref, wu_ref, wd_ref, y_ref):
    del te_ref
    i = pl.program_id(0)

    @pl.when(i < nv_ref[0])
    def _():
        x = x_ref[...].astype(BF16)
        g = _dot(x, wg_ref[...])
        u = _dot(x, wu_ref[...])
        y_ref[...] = _dot((g * _sigmoid(g) * u).astype(BF16), wd_ref[...])

    @pl.when(i >= nv_ref[0])
    def _():
        y_ref[...] = jnp.zeros_like(y_ref)


def moe_ffn(tile_expert, n_valid, xs, wg, wu, wd):
    n_rows, d = xs.shape
    ff = wg.shape[-1]
    n_tiles = n_rows // FFN_TM
    grid_spec = pltpu.PrefetchScalarGridSpec(
        num_scalar_prefetch=2,
        grid=(n_tiles,),
        in_specs=[pl.BlockSpec((FFN_TM, d), lambda i, te, nv: (jnp.minimum(i, nv[0] - 1), 0)),
                  pl.BlockSpec((None, d, ff), lambda i, te, nv: (te[i], 0, 0)),
                  pl.BlockSpec((None, d, ff), lambda i, te, nv: (te[i], 0, 0)),
                  pl.BlockSpec((None, ff, d), lambda i, te, nv: (te[i], 0, 0))],
        out_specs=pl.BlockSpec((FFN_TM, d), lambda i, te, nv: (i, 0)))
    return pl.pallas_call(
        _ffn_body,
        grid_spec=grid_spec,
        out_shape=jax.ShapeDtypeStruct((n_rows, d), F32),
        compiler_params=_cp("arbitrary"),
        name="moe_ffn",
    )(tile_expert, n_valid, xs, wg, wu, wd)


def _combine_body(pos_ref, ys_ref, x_ref, w_ref, mod_ref, o_ref, buf_ref, sem, *, tm, tpb):
    def start(t, carry):
        for k in range(2):
            p = pos_ref[0, 0, k * tm + t]
            _row_copy(ys_ref.at[pl.ds(p, 1)], buf_ref.at[k, pl.ds(t, 1)], sem).start()
        return carry

    lax.fori_loop(0, tm, start, 0, unroll=8)
    for k in range(2):
        _row_copy(ys_ref.at[pl.ds(0, tm)], buf_ref.at[k], sem).wait()
    w = w_ref[...]
    y = w[:, 0:1] * buf_ref[0] + w[:, 1:2] * buf_ref[1]
    row0 = (pl.program_id(0) % tpb) * tm
    o_ref[...] = x_ref[...] + _mod_rows(mod_ref, 5, row0, tm) * y


def moe_combine(pos3, ys, x, wt, mod, *, tm=256):
    t, d = x.shape
    tpb = ROWS // tm
    row = pl.BlockSpec((tm, d), lambda i: (i, 0))
    return pl.pallas_call(
        functools.partial(_combine_body, tm=tm, tpb=tpb),
        grid=(t // tm,),
        in_specs=[pl.BlockSpec((1, 1, 2 * tm), lambda i: (i, 0, 0), memory_space=pltpu.SMEM),
                  pl.BlockSpec(memory_space=pl.ANY), row,
                  pl.BlockSpec((tm, 2), lambda i: (i, 0)),
                  pl.BlockSpec((1, 2, 6, d), lambda i: (i // tpb, 0, 0, 0))],
        out_specs=row,
        out_shape=jax.ShapeDtypeStruct((t, d), F32),
        scratch_shapes=[pltpu.VMEM((2, tm, d), F32), pltpu.SemaphoreType.DMA(())],
        compiler_params=_cp("arbitrary"),
        name="moe_combine",
    )(pos3, ys, x, wt, mod)


def moe_layout(e, rank, counts, *, tm=256):
    t = e.shape[1]
    n_rows_max = -(-(2 * t + N_EXPERTS * (FFN_TM - 1)) // FFN_TM) * FFN_TM
    n_tiles = n_rows_max // FFN_TM
    cnt = counts[:, 0].astype(jnp.int32)
    gsz = ((cnt + FFN_TM - 1) // FFN_TM) * FFN_TM
    end = jnp.cumsum(gsz)
    off = end - gsz
    onehot = e[:, :, None] == jnp.arange(N_EXPERTS, dtype=jnp.int32)
    pos = rank + jnp.sum(jnp.where(onehot, off, 0), axis=-1)
    pos3 = pos.reshape(2, t // tm, tm).transpose(1, 0, 2).reshape(t // tm, 1, 2 * tm)
    n_valid = (end[-1] // FFN_TM).astype(jnp.int32)
    starts = jnp.minimum(jnp.arange(n_tiles, dtype=jnp.int32), n_valid - 1) * FFN_TM
    tile_expert = jnp.sum(starts[:, None] >= end[None, :], axis=1).astype(jnp.int32)
    return pos3, tile_expert, n_valid.reshape(1), n_rows_max


def _block_diag(w):
    n, bi, bj = w.shape[-3:]
    eye = jnp.eye(n, dtype=w.dtype)
    out = w[..., :, :, None, :] * eye[:, None, :, None]
    return out.reshape(w.shape[:-3] + (n * bi, n * bj))


def kernel(x, c, ctx, c_ctx, w_mod, b_mod, norm1_g, norm2_g, w_in, a_mu, a_w0, a_w2, a_a0, a_a2, a_g2, a_kk, a_ka,
           a_rk, a_gn_g, a_gn_b, b_qn, b_kn, b_lam, b_subln, c_qn, c_kn, c_rpb, d_conv_w, d_conv_b, d_wa, d_ba,
           d_wx, d_bx, d_lam, w_branch, w_out, router_w, router_b, e_gate, e_up, e_down):
    nb = x.shape[0]
    d = D_MODEL
    t = nb * ROWS
    xs = jnp.concatenate([ctx, x], axis=1).reshape(t, d)

    cc = jnp.concatenate([c, c_ctx[None], jnp.zeros((-(nb + 1) % 8, d), F32)], axis=0)
    mods = mod_vectors(cc, w_mod, b_mod)
    mod_l = mods[:, :nb].reshape(DEPTH, nb, 1, 6, d)
    mod_c = jnp.broadcast_to(mods[:, nb].reshape(DEPTH, 1, 1, 6, d), (DEPTH, nb, 1, 6, d))
    mod = jnp.concatenate([mod_l, mod_c], axis=2)

    o_b, o_c, o_d, o_g = A_COLS, A_COLS + 1536, A_COLS + 3072, A_COLS + 3072 + 1024
    w_mix = jnp.concatenate([w_in[:, :, :o_b], jnp.zeros((DEPTH, d, 2048 - A_COLS), F32), w_in[:, :, o_b:o_g]],
                            axis=-1).astype(BF16)
    tn_m = 256
    w_gate = w_in[:, :, o_g:].astype(BF16).reshape(DEPTH, d, N_BRANCH, d // tn_m, tn_m)
    w_gate = jnp.transpose(w_gate, (0, 3, 1, 2, 4)).reshape(DEPTH, d // tn_m, d, N_BRANCH * tn_m)
    mu = jnp.pad(a_mu, ((0, 0), (0, 0), (0, 2048 - A_COLS)))
    w2bd = _block_diag(a_w2).astype(BF16)
    a2bd = _block_diag(a_a2).astype(BF16)
    g2 = a_g2.astype(BF16)
    wa_bd = _block_diag(d_wa).astype(BF16)
    wx_bd = _block_diag(d_wx).astype(BF16)
    wb = w_branch.astype(BF16)
    wo = w_out.astype(BF16)
    rw = jnp.stack(_split(router_w.T), axis=0)
    rb = router_b.reshape(N_EXPERTS, 1)
    eg, eu, ed = e_gate.astype(BF16), e_up.astype(BF16), e_down.astype(BF16)
    cos, sin = rope_tables()
    bias = rpb_tables(c_rpb)
    qn_c = jnp.tile(c_qn, (1, C_HEADS)).reshape(DEPTH, 1, BLK)
    kn_c = jnp.tile(c_kn, (1, C_HEADS)).reshape(DEPTH, 1, BLK)

    for l in range(DEPTH):
        lam_init = 0.8 - 0.6 * math.exp(-0.3 * l)
        z, hn = in_projection(xs, mod[l], norm1_g[l].reshape(1, d), w_mix[l])
        prep = rwkv_prepare(z, mu[l], w2bd[l], a2bd[l], g2[l], a_w0[l].reshape(1, 1024), a_a0[l].reshape(1, 1024),
                            a_kk[l].reshape(1, BLK), a_ka[l].reshape(1, BLK), a_rk[l].reshape(1, BLK))
        ya = rwkv_scan(prep, 0)
        oa = rwkv_scan(prep, 1, ya, a_gn_g[l].reshape(1, BLK), a_gn_b[l].reshape(1, BLK))
        ob = diff_attention(z, cos, sin, b_qn[l].reshape(1, 128), b_kn[l].reshape(1, 128), b_lam[l],
                            b_subln[l].reshape(1, 128), lam_init)
        oc = na_attention(z, bias, qn_c[l], kn_c[l], l)
        lru = [(d_conv_w[l], d_conv_b[l].reshape(1, BLK), wa_bd[l, k], d_ba[l, k].reshape(1, BLK), wx_bd[l, k],
                d_bx[l, k].reshape(1, BLK), d_lam[l, k].reshape(1, BLK)) for k in range(2)]
        hd = rglru(z, 0, *lru[0])
        od = rglru(z, 1, *lru[1], h0=hd)
        y = merge_branches(hn, w_gate[l], (oa, ob, oc, od), wb[l], tn=tn_m)
        xs, h2, e, wts, rank, counts = post_attention(y, wo[l], xs, mod[l], norm2_g[l].reshape(1, d), rw, rb)
        pos3, tile_expert, n_valid, n_rows = moe_layout(e, rank, counts)
        xg = moe_dispatch(pos3, h2, n_rows)
        yg = moe_ffn(tile_expert, n_valid, xg, eg[l], eu[l], ed[l])
        xs = moe_combine(pos3, yg, xs, wts.T, mod[l])
    return xs.reshape(nb, ROWS, d)[:, CTX:]
```

```python
import functools
import math

import numpy as np
import jax
import jax.numpy as jnp
from jax import lax
from jax.experimental import pallas as pl
from jax.experimental.pallas import tpu as pltpu

F32 = jnp.float32
BF16 = jnp.bfloat16

D_MODEL = 2048
DEPTH = 4
GRID_W = 64
CTX = 256
SEQ = 2048
ROWS = CTX + SEQ
HEAD = 64
A_WIDTH = 512
A_COLS = 1920
A_GN_EPS = 64e-5
B_HEADS = 4
C_HEADS = 8
NA_ROWS = 8
NA_COLS = 16
D_WIDTH = 512
D_CONV = 4
RG_C = 8.0
N_BRANCH = 4
N_EXPERTS = 16
N_GROUPS = 4
D_FF = 1024
ROPE_BASE = 10000.0
NORM_EPS = 1e-6
Z_COLS = 6144
BLK = 512

VMEM_LIMIT = 56 * 1024 * 1024

CHUNK = 64
TQ = 256
FFN_TM = 512


def _cp(*sem):
    return pltpu.CompilerParams(dimension_semantics=sem, vmem_limit_bytes=VMEM_LIMIT)


def _dot(a, b):
    return jnp.dot(a, b, preferred_element_type=F32)


def _dot_nt(a, b):
    return lax.dot_general(a, b, (((1,), (1,)), ((), ())), preferred_element_type=F32)


def _dot_tn(a, b):
    return lax.dot_general(a, b, (((0,), (0,)), ((), ())), preferred_element_type=F32)


def _split(x):
    hi = x.astype(BF16)
    lo = (x - hi.astype(F32)).astype(BF16)
    return hi, lo


def _dot_x(x, m):
    hi, lo = _split(x)
    return _dot(hi, m) + _dot(lo, m)


def _sigmoid(x):
    return 1.0 / (1.0 + jnp.exp(-x))


def _softplus(x):
    return jnp.maximum(x, 0.0) + jnp.log(1.0 + jnp.exp(-jnp.abs(x)))


def _iota(shape, dim):
    return lax.broadcasted_iota(jnp.int32, shape, dim)


def _seg_ones(width, seg):
    r = np.arange(width)
    return jnp.asarray((r[:, None] // seg) == (r[None, :] // seg), dtype=BF16)


def _mod_body(c_ref, w_ref, b_ref, o_ref):
    c = c_ref[...]
    s = (c * _sigmoid(c)).astype(BF16)
    o_ref[...] = _dot(s, w_ref[...].astype(BF16)) + b_ref[...]


def mod_vectors(cc, w_mod, b_mod):
    depth, d, n = w_mod.shape
    r = cc.shape[0]
    tn = 1536
    return pl.pallas_call(
        _mod_body,
        grid=(depth, n // tn),
        in_specs=[pl.BlockSpec((r, d), lambda l, j: (0, 0)),
                  pl.BlockSpec((None, d, tn), lambda l, j: (l, 0, j)),
                  pl.BlockSpec((None, 1, tn), lambda l, j: (l, 0, j))],
        out_specs=pl.BlockSpec((None, r, tn), lambda l, j: (l, 0, j)),
        out_shape=jax.ShapeDtypeStruct((depth, r, n), F32),
        compiler_params=_cp("arbitrary", "arbitrary"),
        name="mod_vectors",
    )(cc, w_mod, b_mod.reshape(depth, 1, n))


def _mod_rows(mod_ref, idx, row0, tm):
    rows = row0 + _iota((tm, 1), 0)
    return jnp.where(rows < CTX, mod_ref[0, 1, idx:idx + 1, :], mod_ref[0, 0, idx:idx + 1, :])


def _norm_mod(x, g, mod_ref, shift_idx, row0, tm):
    ms = jnp.mean(x * x, axis=-1, keepdims=True)
    y = x * lax.rsqrt(ms + NORM_EPS) * g
    return y * (1.0 + _mod_rows(mod_ref, shift_idx + 1, row0, tm)) + _mod_rows(mod_ref, shift_idx, row0, tm)


def _inproj_body(x_ref, mod_ref, g_ref, w_ref, z_ref, hn_ref, *, tm, tpb, slab):
    i = pl.program_id(0)

    @pl.when(pl.program_id(1) == 0)
    def _():
        for r in range(0, tm, slab):
            row0 = (i % tpb) * tm + r
            hn_ref[r:r + slab, :] = _norm_mod(x_ref[r:r + slab, :], g_ref[...], mod_ref, 0, row0, slab).astype(BF16)

    z_ref[...] = _dot(hn_ref[...], w_ref[...])


def in_projection(x, mod, g, w, *, tm=1152, tn=512):
    t, d = x.shape
    n = w.shape[1]
    tpb = ROWS // tm
    return pl.pallas_call(
        functools.partial(_inproj_body, tm=tm, tpb=tpb, slab=128),
        grid=(t // tm, n // tn),
        in_specs=[pl.BlockSpec((tm, d), lambda i, j: (i, 0)),
                  pl.BlockSpec((1, 2, 6, d), lambda i, j: (i // tpb, 0, 0, 0)),
                  pl.BlockSpec((1, d), lambda i, j: (0, 0)),
                  pl.BlockSpec((d, tn), lambda i, j: (0, j))],
        out_specs=[pl.BlockSpec((tm, tn), lambda i, j: (i, j)),
                   pl.BlockSpec((tm, d), lambda i, j: (i, 0))],
        out_shape=[jax.ShapeDtypeStruct((t, n), F32), jax.ShapeDtypeStruct((t, d), BF16)],
        compiler_params=_cp("arbitrary", "arbitrary"),
        name="in_projection",
    )(x, mod, g, w)


def _tile_flags(ti, tpb):
    ctx_tiles = CTX // TQ
    has_prev = jnp.logical_and(ti != 0, ti != ctx_tiles)
    has_next = jnp.logical_and(ti != ctx_tiles - 1, ti != tpb - 1)
    return has_prev, has_next


def _halo_specs(width, col, n_tiles):
    r8 = TQ // 8
    return [pl.BlockSpec((TQ, width), lambda i: (i, col)),
            pl.BlockSpec((8, width), lambda i: (jnp.maximum(i * r8 - 1, 0), col)),
            pl.BlockSpec((8, width), lambda i: (jnp.minimum((i + 1) * r8, n_tiles * r8 - 1), col))]


def _shift_rows(z, halo_row, k, use_halo):
    n = z.shape[0]
    rows = _iota((n, 1), 0)
    fill = jnp.where(use_halo, halo_row, 0.0)
    if k > 0:
        return jnp.where(rows < k, fill, pltpu.roll(z, k, 0))
    return jnp.where(rows >= n + k, fill, pltpu.roll(z, n + k, 0))


def _rwkv_prep_body(z_ref, zp_ref, zn_ref, mu_ref, w2_ref, a2_ref, g2_ref, w0_ref, a0_ref,
                    kk_ref, ka_ref, rk_ref, seg_ref, o_ref, *, tpb):
    has_prev, has_next = _tile_flags(pl.program_id(0) % tpb, tpb)
    z = z_ref[...]
    prev = _shift_rows(z, zp_ref[7:8, :], 1, has_prev)
    nxt = _shift_rows(z, zn_ref[0:1, :], -1, has_next)
    zs = z + mu_ref[0:1, :] * (prev - z) + mu_ref[1:2, :] * (nxt - z)
    r = zs[:, 0:512]
    k = zs[:, 512:1024]
    v = zs[:, 1024:1536]
    wl = zs[:, 1536:1664]
    al = zs[:, 1664:1792]
    gl = zs[:, 1792:1920]
    seg = seg_ref[...]
    wlog = -_softplus(-(w0_ref[...] + _dot(jnp.tanh(wl).astype(BF16), w2_ref[...]))) - 0.5
    lw = -jnp.exp(wlog)
    a = _sigmoid(a0_ref[...] + _dot(al.astype(BF16), a2_ref[...]))
    g = _dot(_sigmoid(gl).astype(BF16), g2_ref[...])
    kk = k * kk_ref[...]
    kk = kk / jnp.maximum(jnp.sqrt(_dot_x(kk * kk, seg)), 1e-12)
    bonus = _dot_x(r * k * rk_ref[...], seg) * v
    ka = ka_ref[...]
    o_ref[:, 0:512] = r
    o_ref[:, 512:1024] = v
    o_ref[:, 1024:1536] = kk
    o_ref[:, 1536:2048] = bonus
    o_ref[:, 2048:2560] = g
    for d in range(2):
        ad = a[:, d * 512:(d + 1) * 512]
        base = 2560 + d * 1536
        o_ref[:, base:base + 512] = lw[:, d * 512:(d + 1) * 512]
        o_ref[:, base + 512:base + 1024] = kk * ad
        o_ref[:, base + 1024:base + 1536] = k * (1.0 + (ad - 1.0) * ka)


A_PREP_COLS = 2560 + 2 * 1536


def rwkv_prepare(z, mu, w2bd, a2bd, g2, w0, a0, k_k, k_a, r_k):
    t = z.shape[0]
    n_tiles = t // TQ
    tpb = ROWS // TQ
    full = lambda shape: pl.BlockSpec(shape, lambda i: (0,) * len(shape))
    return pl.pallas_call(
        functools.partial(_rwkv_prep_body, tpb=tpb),
        grid=(n_tiles,),
        in_specs=_halo_specs(2048, 0, n_tiles) + [
            full((2, 2048)), full((128, 1024)), full((128, 1024)), full((128, 512)),
            full((1, 1024)), full((1, 1024)), full((1, 512)), full((1, 512)), full((1, 512)),
            full((512, 512))],
        out_specs=pl.BlockSpec((TQ, A_PREP_COLS), lambda i: (i, 0)),
        out_shape=jax.ShapeDtypeStruct((t, A_PREP_COLS), F32),
        compiler_params=_cp("arbitrary"),
        name="rwkv_prepare",
    )(z, z, z, mu, w2bd, a2bd, g2, w0, a0, k_k, k_a, r_k, _seg_ones(512, HEAD))


def _rwkv_scan_body(r_ref, v_ref, kk_ref, lw_ref, b_ref, ke_ref, *rest, reverse, last, nbb):
    if last:
        y0_ref, bonus_ref, g_ref, gng_ref, gnb_ref, seg_ref, o_ref, s_ref = rest
    else:
        o_ref, s_ref = rest
    c = CHUNK
    nh = A_WIDTH // HEAD

    @pl.when(pl.program_id(1) == 0)
    def _():
        s_ref[...] = jnp.zeros_like(s_ref)

    ti = _iota((c, c), 0)
    si = _iota((c, c), 1)
    before = (si > ti) if reverse else (si < ti)
    upto = jnp.logical_or(before, si == ti)
    tri = jnp.where(upto, 1.0, 0.0).astype(BF16)
    eye = jnp.where(si == ti, 1.0, 0.0)
    pair = (ti >> 1) == (si >> 1)
    off_masks = []
    lg = 1
    while (2 << lg) <= c:
        off_masks.append(jnp.logical_and((ti >> (lg + 1)) == (si >> (lg + 1)), (ti >> lg) != (si >> lg)))
        lg += 1
    ti2 = _iota((c, 2 * c), 0)
    si2 = _iota((c, 2 * c), 1) % c
    upto2 = (si2 >= ti2) if reverse else (si2 <= ti2)
    hi2 = _iota((c, 2 * c), 1) >= c
    before_hi = jnp.logical_and(hi2, (si2 > ti2) if reverse else (si2 < ti2))

    chains = [(bi, h) for bi in range(nbb) for h in range(nh)]
    cols = {}
    gammas = []
    for bi in range(nbb):
        lw = lw_ref[bi]
        lw_hi, lw_lo = _split(lw)
        cs = _dot(tri, lw_hi) + _dot(tri, lw_lo)
        tot = jnp.sum(lw, axis=0, keepdims=True)
        e_neg = jnp.exp(-cs)
        e_hat = jnp.exp(tot - cs)
        gammas.append(jnp.exp(tot))
        b = b_ref[bi]
        ke = ke_ref[bi]
        mats = dict(r=r_ref[bi] * jnp.exp(cs), a=-kk_ref[bi] * jnp.exp(cs - lw), b=b * e_neg, k=ke * e_neg,
                    bh=b * e_hat, kh=ke * e_hat, v=v_ref[bi])
        for name, m in mats.items():
            for h in range(nh):
                cols[name, bi, h] = m[:, h * HEAD:(h + 1) * HEAD].astype(BF16)

    def per_chain(fn):
        return [fn(i, bi, h) for i, (bi, h) in enumerate(chains)]

    prod = per_chain(lambda i, bi, h: _dot_nt(
        jnp.concatenate([cols['a', bi, h], cols['r', bi, h]], axis=0),
        jnp.concatenate([cols['b', bi, h], cols['k', bi, h]], axis=0)))
    lab = per_chain(lambda i, bi, h: jnp.where(before, prod[i][0:c, 0:c], 0.0))
    lak2 = per_chain(lambda i, bi, h: jnp.where(before_hi, prod[i][0:c, :], 0.0).astype(BF16))
    m = per_chain(lambda i, bi, h: jnp.where(upto2, prod[i][c:2 * c, :], 0.0).astype(BF16))
    tinv = per_chain(lambda i, bi, h: eye + jnp.where(pair, lab[i], 0.0))
    for off_mask in off_masks:
        tb = [t.astype(BF16) for t in tinv]
        x = per_chain(lambda i, bi, h: _dot(jnp.where(off_mask, lab[i], 0.0).astype(BF16), tb[i]))
        tinv = per_chain(lambda i, bi, h: tinv[i] + _dot(tb[i], x[i].astype(BF16)))
    vv = per_chain(lambda i, bi, h: jnp.concatenate([cols['v', bi, h], cols['v', bi, h]], axis=0))
    lv = per_chain(lambda i, bi, h: _dot(lak2[i], vv[i]))
    tb = [t.astype(BF16) for t in tinv]
    w1 = per_chain(lambda i, bi, h: _dot(tb[i], cols['a', bi, h]))
    u2 = per_chain(lambda i, bi, h: _dot(tb[i], lv[i].astype(BF16)))
    s0 = per_chain(lambda i, bi, h: s_ref[bi, h])
    s0b = [s.astype(BF16) for s in s0]
    ws = per_chain(lambda i, bi, h: _dot_nt(
        jnp.concatenate([w1[i].astype(BF16), cols['r', bi, h]], axis=0), s0b[i]))
    u = per_chain(lambda i, bi, h: ws[i][0:c] + u2[i])
    uv = per_chain(lambda i, bi, h: jnp.concatenate([u[i].astype(BF16), cols['v', bi, h]], axis=0))
    ys = per_chain(lambda i, bi, h: ws[i][c:2 * c] + _dot(m[i], uv[i]))
    for i, (bi, h) in enumerate(chains):
        bkh = jnp.concatenate([cols['bh', bi, h], cols['kh', bi, h]], axis=0)
        s_ref[bi, h] = s0[i] * gammas[bi][:, h * HEAD:(h + 1) * HEAD] + _dot_tn(uv[i], bkh)
    for bi in range(nbb):
        y = jnp.concatenate(ys[bi * nh:(bi + 1) * nh], axis=1)
        if last:
            y = y + y0_ref[bi]
            seg = seg_ref[...]
            inv = 1.0 / HEAD
            mean = _dot_x(y, seg) * inv
            yc = y - mean
            var = _dot_x(yc * yc, seg) * inv
            yn = yc * lax.rsqrt(var + A_GN_EPS) * gng_ref[...] + gnb_ref[...]
            o_ref[bi] = ((yn + bonus_ref[bi]) * g_ref[bi]).astype(o_ref.dtype)
        else:
            o_ref[bi] = y


def rwkv_scan(prep, direction, y0=None, gn_g=None, gn_b=None, *, nbb=2):
    t = prep.shape[0]
    nb = t // ROWS
    cpb = ROWS // CHUNK
    cctx = CTX // CHUNK
    reverse = direction == 1
    last = y0 is not None
    prep3 = prep.reshape(nb, ROWS, prep.shape[1])

    def blk(p):
        if reverse:
            return jnp.where(p < cctx, cctx - 1 - p, cpb - 1 + cctx - p)
        return p

    def col(cb):
        return pl.BlockSpec((nbb, CHUNK, BLK), lambda g, p: (g, blk(p), cb))

    base = 5 + 3 * direction
    in_specs = [col(0), col(1), col(2), col(base), col(base + 1), col(base + 2)]
    args = [prep3] * 6
    if last:
        in_specs += [col(0), col(3), col(4),
                     pl.BlockSpec((1, BLK), lambda g, p: (0, 0)), pl.BlockSpec((1, BLK), lambda g, p: (0, 0)),
                     pl.BlockSpec((BLK, BLK), lambda g, p: (0, 0))]
        args += [y0.reshape(nb, ROWS, BLK), prep3, prep3, gn_g, gn_b, _seg_ones(512, HEAD)]
    out = pl.pallas_call(
        functools.partial(_rwkv_scan_body, reverse=reverse, last=last, nbb=nbb),
        grid=(nb // nbb, cpb),
        in_specs=in_specs,
        out_specs=col(0),
        out_shape=jax.ShapeDtypeStruct((nb, ROWS, BLK), BF16 if last else F32),
        scratch_shapes=[pltpu.VMEM((nbb, A_WIDTH // HEAD, HEAD, HEAD), F32)],
        compiler_params=_cp("arbitrary", "arbitrary"),
        name="rwkv_scan_%d" % direction,
    )(*args)
    return out.reshape(t, BLK)


def _rope(x, cos, sin):
    lane = _iota(x.shape, 1)
    partner = jnp.where((lane % HEAD) < HEAD // 2, pltpu.roll(x, 128 - HEAD // 2, 1), pltpu.roll(x, HEAD // 2, 1))
    return x * cos + partner * sin


def _head_rms(x, gain, seg):
    ms = _dot_x(x * x, seg) * (1.0 / HEAD)
    return x * lax.rsqrt(ms + NORM_EPS) * gain


def _diff_body(q_ref, k_ref, v_ref, cos_ref, sin_ref, qn_ref, kn_ref, lam_ref, sub_ref, seg_ref,
               o_ref, kh_ref, vh_ref, *, lam_init, nq):
    iq = pl.program_id(2)
    seg = seg_ref[...]

    @pl.when(iq == 0)
    def _():
        kh = _rope(_head_rms(k_ref[...], kn_ref[...], seg), cos_ref[...], sin_ref[...])
        kh_ref[...] = kh.astype(BF16)
        vh_ref[...] = v_ref[...].astype(BF16)

    lv = lam_ref[...]
    lam = (jnp.exp(jnp.sum(lv[0:1] * lv[1:2], axis=1, keepdims=True))
           - jnp.exp(jnp.sum(lv[2:3] * lv[3:4], axis=1, keepdims=True)) + lam_init)
    r0 = pl.multiple_of(iq * TQ, TQ)
    q = _rope(_head_rms(q_ref[...], qn_ref[...], seg), cos_ref[pl.ds(r0, TQ), :], sin_ref[pl.ds(r0, TQ), :])
    q = q * (HEAD ** -0.5 * math.log2(math.e))
    lane = _iota(q.shape, 1)
    q1 = jnp.where(lane < HEAD, q, 0.0).astype(BF16)
    q2 = jnp.where(lane >= HEAD, q, 0.0).astype(BF16)

    def attend(kh, vh):
        def soft_v(qs):
            s = _dot_nt(qs, kh)
            e = jnp.exp2(s - jnp.max(s, axis=-1, keepdims=True))
            return _dot(e.astype(BF16), vh) / jnp.sum(e, axis=-1, keepdims=True)
        o = soft_v(q1) - lam * soft_v(q2)
        ms = jnp.mean(o * o, axis=-1, keepdims=True)
        o_ref[...] = (o * lax.rsqrt(ms + NORM_EPS) * sub_ref[...] * (1.0 - lam_init)).astype(o_ref.dtype)

    @pl.when(iq == 0)
    def _():
        attend(kh_ref[0:CTX, :], vh_ref[0:CTX, :])

    @pl.when(iq != 0)
    def _():
        attend(kh_ref[...], vh_ref[...])


def diff_attention(z, cos, sin, qn, kn, lam_vecs, subln, lam_init):
    t = z.shape[0]
    nb = t // ROWS
    nq = ROWS // TQ
    cb = BLK // 128
    full = lambda shape: pl.BlockSpec(shape, lambda b, h, i: (0,) * len(shape))
    return pl.pallas_call(
        functools.partial(_diff_body, lam_init=lam_init, nq=nq),
        grid=(nb, B_HEADS, nq),
        in_specs=[pl.BlockSpec((TQ, 128), lambda b, h, i: (b * nq + i, 4 * cb + h)),
                  pl.BlockSpec((ROWS, 128), lambda b, h, i: (b, 5 * cb + h)),
                  pl.BlockSpec((ROWS, 128), lambda b, h, i: (b, 6 * cb + h)),
                  full((ROWS, 128)), full((ROWS, 128)), full((1, 128)), full((1, 128)),
                  full((4, HEAD)), full((1, 128)), full((128, 128))],
        out_specs=pl.BlockSpec((TQ, 128), lambda b, h, i: (b * nq + i, h)),
        out_shape=jax.ShapeDtypeStruct((t, BLK), BF16),
        scratch_shapes=[pltpu.VMEM((ROWS, 128), BF16), pltpu.VMEM((ROWS, 128), BF16)],
        compiler_params=_cp("arbitrary", "arbitrary", "arbitrary"),
        name="diff_attention",
    )(z, z, z, cos, sin, qn, kn, lam_vecs, subln, _seg_ones(128, HEAD))


def rope_tables():
    t = np.arange(SEQ)
    quarter = HEAD // 4
    inv = (1.0 / (ROPE_BASE ** (np.arange(quarter, dtype=np.float32) / quarter))).astype(np.float32)
    row = (t // GRID_W).astype(np.float32)[:, None] * inv
    col = (t % GRID_W).astype(np.float32)[:, None] * inv
    ang = np.concatenate([row, col], axis=-1)
    cos = np.concatenate([np.ones((CTX, HEAD // 2), np.float32), np.cos(ang)], axis=0)
    sin = np.concatenate([np.zeros((CTX, HEAD // 2), np.float32), np.sin(ang)], axis=0)
    cos = np.tile(cos, (1, 4))
    sin = np.tile(np.concatenate([-sin, sin], axis=1), (1, 2))
    return jnp.asarray(cos, F32), jnp.asarray(sin, F32)


def _rpb_body(r_ref, e_ref, o_ref):
    o_ref[...] = _dot_x(r_ref[...] * math.log2(math.e), e_ref[...])


def rpb_tables(rpb):
    depth = rpb.shape[0]
    nr, nc = 2 * NA_ROWS - 1, 2 * NA_COLS - 1
    q = np.arange(GRID_W)
    dc = np.clip(q[None, :] - q[:, None] + NA_COLS - 1, 0, nc - 1)
    onehot = (np.arange(32)[:, None, None] == dc[None]).astype(np.float32).reshape(32, GRID_W * GRID_W)
    r2 = jnp.pad(rpb.reshape(depth * C_HEADS * nr, nc), ((0, 0), (0, 32 - nc)))
    rows = r2.shape[0]
    toe = pl.pallas_call(
        _rpb_body,
        grid=(1,),
        in_specs=[pl.BlockSpec((rows, 32), lambda i: (0, 0)), pl.BlockSpec((32, GRID_W * GRID_W), lambda i: (0, 0))],
        out_specs=pl.BlockSpec((rows, GRID_W * GRID_W), lambda i: (0, 0)),
        out_shape=jax.ShapeDtypeStruct((rows, GRID_W * GRID_W), F32),
        compiler_params=_cp("arbitrary"),
        name="rpb_tables",
    )(r2, jnp.asarray(onehot, BF16))
    toe = toe.reshape(depth, C_HEADS, nr, GRID_W, GRID_W)
    rows = SEQ // GRID_W
    dr = np.zeros((3, NA_QROWS, NA_UNION), np.int32)
    ok_row = np.zeros((3, NA_QROWS, NA_UNION), bool)
    for ty, g in enumerate((0, 1, rows // NA_QROWS - 1)):
        us = _na_union_start(g, rows)
        for i in range(NA_QROWS):
            r = NA_QROWS * g + i
            rs = min(max(r - NA_ROWS // 2, 0), rows - NA_ROWS)
            for j in range(NA_UNION):
                ok_row[ty, i, j] = rs <= us + j < rs + NA_ROWS
                dr[ty, i, j] = min(max(us + j - r + NA_ROWS - 1, 0), nr - 1)
    cstart = np.clip(q - NA_COLS // 2, 0, GRID_W - NA_COLS)
    ok_col = (q[None, :] >= cstart[:, None]) & (q[None, :] < cstart[:, None] + NA_COLS)
    ok = ok_row[:, :, :, None, None] & ok_col[None, None, None]
    tab = toe[:, :, jnp.asarray(dr)]
    tab = jnp.where(jnp.asarray(ok), tab, -jnp.inf)
    tab = jnp.transpose(tab, (0, 2, 1, 3, 5, 4, 6))
    return tab.reshape(depth, 3, C_HEADS, NA_QROWS * GRID_W, NA_UNION * GRID_W)


NA_QROWS = 4
NA_UNION = NA_QROWS + NA_ROWS - 1


def _na_union_start(g, rows):
    lo = NA_QROWS * g - NA_ROWS // 2
    hi = rows - NA_UNION
    if isinstance(g, int):
        return min(max(lo, 0), hi)
    return jnp.clip(lo, 0, hi)


def _na_body(q_ref, k_ref, v_ref, bias_ref, qn_ref, kn_ref, seg_ref, o_ref, kh_ref, vh_ref):
    step = pl.program_id(1)
    seg = seg_ref[...]

    @pl.when(step == 0)
    def _():
        kh_ref[...] = _head_rms(k_ref[...], kn_ref[...], seg).astype(BF16)
        vh_ref[...] = v_ref[...].astype(BF16)

    nq = NA_QROWS * GRID_W
    q = _head_rms(q_ref[...], qn_ref[...], seg) * (HEAD ** -0.5 * math.log2(math.e))
    lane = _iota((nq, 128), 1)
    low = lane < HEAD
    sls = [slice((h // 2) * 128, (h // 2 + 1) * 128) for h in range(C_HEADS)]

    def masked_q(h):
        return jnp.where(low if h % 2 == 0 else jnp.logical_not(low), q[:, sls[h]], 0.0).astype(BF16)

    def store(hs, outs):
        for j in range(0, len(hs), 2):
            o_ref[:, sls[hs[j]]] = jnp.where(low, outs[j], outs[j + 1]).astype(o_ref.dtype)

    head_groups = [list(range(g, g + 4)) for g in range(0, C_HEADS, 4)]

    @pl.when(step == 0)
    def _():
        for hs in head_groups:
            qs = [masked_q(h) for h in hs]
            s = [_dot_nt(qs[j], kh_ref[0:CTX, sls[h]]) for j, h in enumerate(hs)]
            e = [jnp.exp2(x - jnp.max(x, axis=-1, keepdims=True)) for x in s]
            store(hs, [_dot(e[j].astype(BF16), vh_ref[0:CTX, sls[h]]) / jnp.sum(e[j], axis=-1, keepdims=True)
                       for j, h in enumerate(hs)])

    @pl.when(step > 0)
    def _():
        us = _na_union_start(step - 1, SEQ // GRID_W)
        k0 = pl.multiple_of(CTX + us * GRID_W, GRID_W)
        nloc = NA_UNION * GRID_W
        for hs in head_groups:
            qs = [masked_q(h) for h in hs]
            s_loc = [_dot_nt(qs[j], kh_ref[pl.ds(k0, nloc), sls[h]]) + bias_ref[0, h] for j, h in enumerate(hs)]
            s_ctx = [_dot_nt(qs[j], kh_ref[0:CTX, sls[h]]) for j, h in enumerate(hs)]
            mx = [jnp.maximum(jnp.max(a, axis=-1, keepdims=True), jnp.max(b, axis=-1, keepdims=True))
                  for a, b in zip(s_loc, s_ctx)]
            e_loc = [jnp.exp2(a - m_) for a, m_ in zip(s_loc, mx)]
            e_ctx = [jnp.exp2(b - m_) for b, m_ in zip(s_ctx, mx)]
            den = [jnp.sum(a, axis=-1, keepdims=True) + jnp.sum(b, axis=-1, keepdims=True)
                   for a, b in zip(e_loc, e_ctx)]
            store(hs, [(_dot(e_loc[j].astype(BF16), vh_ref[pl.ds(k0, nloc), sls[h]])
                        + _dot(e_ctx[j].astype(BF16), vh_ref[0:CTX, sls[h]])) / den[j] for j, h in enumerate(hs)])


def na_attention(z, bias, qn, kn, layer):
    t = z.shape[0]
    nb = t // ROWS
    nq = NA_QROWS * GRID_W
    assert nq == CTX
    steps = ROWS // nq
    groups = steps - 1

    def layout(i):
        g = jnp.maximum(i - 1, 0)
        return jnp.where(g == 0, 0, jnp.where(g == groups - 1, 2, 1))

    full = lambda shape: pl.BlockSpec(shape, lambda b, i: (0,) * len(shape))
    return pl.pallas_call(
        _na_body,
        grid=(nb, steps),
        in_specs=[pl.BlockSpec((nq, BLK), lambda b, i: (b * steps + i, 7)),
                  pl.BlockSpec((ROWS, BLK), lambda b, i: (b, 8)),
                  pl.BlockSpec((ROWS, BLK), lambda b, i: (b, 9)),
                  pl.BlockSpec((None, 1, C_HEADS, nq, NA_UNION * GRID_W), lambda b, i: (layer, layout(i), 0, 0, 0)),
                  full((1, BLK)), full((1, BLK)), full((BLK, BLK))],
        out_specs=pl.BlockSpec((nq, BLK), lambda b, i: (b * steps + i, 0)),
        out_shape=jax.ShapeDtypeStruct((t, BLK), BF16),
        scratch_shapes=[pltpu.VMEM((ROWS, BLK), BF16), pltpu.VMEM((ROWS, BLK), BF16)],
        compiler_params=_cp("arbitrary", "arbitrary"),
        name="na_attention",
    )(z, z, z, bias, qn, kn, _seg_ones(512, HEAD))


def _lru_body(x_ref, xp_ref, xn_ref, gate_ref, cw_ref, cb_ref, wa_ref, ba_ref, wx_ref, bx_ref, lam_ref,
              *rest, reverse, last, tpb):
    if last:
        h0_ref, o_ref, carry_ref = rest
    else:
        o_ref, carry_ref = rest
    p = pl.program_id(0) % tpb
    has_prev, has_next = _tile_flags(_lru_tile(p, tpb, reverse), tpb)

    @pl.when(p == 0)
    def _():
        carry_ref[...] = jnp.zeros_like(carry_ref)

    x = x_ref[...]
    n = x.shape[0]
    xm1 = _shift_rows(x, xp_ref[7:8, :], 1, has_prev)
    xp1 = _shift_rows(x, xn_ref[0:1, :], -1, has_next)
    rows = _iota((n, 1), 0)
    nx = jnp.where(has_next, xn_ref[0:2, :], 0.0)
    xp2 = pltpu.roll(x, n - 2, 0)
    xp2 = jnp.where(rows == n - 2, nx[0:1, :], jnp.where(rows == n - 1, nx[1:2, :], xp2))
    xr = (cw_ref[0:1, :] * xm1 + cw_ref[1:2, :] * x + cw_ref[2:3, :] * xp1 + cw_ref[3:4, :] * xp2) + cb_ref[...]
    xb = xr.astype(BF16)
    rg = _sigmoid(_dot(xb, wa_ref[...]) + ba_ref[...])
    ig = _sigmoid(_dot(xb, wx_ref[...]) + bx_ref[...])
    log_a = -RG_C * _softplus(-lam_ref[...]) * rg
    a = jnp.exp(log_a)
    b = jnp.sqrt(jnp.maximum(1.0 - jnp.exp(2.0 * log_a), 0.0)) * ig * xr
    k = 1
    while k < n:
        if reverse:
            keep = rows < n - k
            a_s = pltpu.roll(a, n - k, 0)
            b_s = pltpu.roll(b, n - k, 0)
        else:
            keep = rows >= k
            a_s = pltpu.roll(a, k, 0)
            b_s = pltpu.roll(b, k, 0)
        b = jnp.where(keep, a * b_s + b, b)
        a = jnp.where(keep, a * a_s, a)
        k *= 2
    h = a * carry_ref[...] + b
    carry_ref[...] = h[0:1, :] if reverse else h[n - 1:n, :]
    if last:
        g = gate_ref[...]
        gelu = 0.5 * g * (1.0 + jnp.tanh(math.sqrt(2.0 / math.pi) * (g + 0.044715 * g * g * g)))
        o_ref[...] = ((h0_ref[...] + h) * gelu).astype(o_ref.dtype)
    else:
        o_ref[...] = h


def _lru_tile(p, tpb, reverse):
    cctx = CTX // TQ
    if reverse:
        return jnp.where(p < cctx, cctx - 1 - p, tpb - 1 + cctx - p)
    return p


def rglru(z, direction, conv_w, conv_b, wa_bd, ba, wx_bd, bx, lam, h0=None):
    t = z.shape[0]
    n_tiles = t // TQ
    tpb = ROWS // TQ
    reverse = direction == 1
    last = h0 is not None

    def blk(i):
        return (i // tpb) * tpb + _lru_tile(i % tpb, tpb, reverse)

    r8 = TQ // 8
    full = lambda shape: pl.BlockSpec(shape, lambda i: (0,) * len(shape))
    in_specs = [pl.BlockSpec((TQ, BLK), lambda i: (blk(i), 10)),
                pl.BlockSpec((8, BLK), lambda i: (jnp.maximum(blk(i) * r8 - 1, 0), 10)),
                pl.BlockSpec((8, BLK), lambda i: (jnp.minimum((blk(i) + 1) * r8, n_tiles * r8 - 1), 10)),
                pl.BlockSpec((TQ, BLK), lambda i: (blk(i), 11)),
                full((D_CONV, BLK)), full((1, BLK)), full((BLK, BLK)), full((1, BLK)),
                full((BLK, BLK)), full((1, BLK)), full((1, BLK))]
    args = [z, z, z, z, conv_w, conv_b, wa_bd, ba, wx_bd, bx, lam]
    if last:
        in_specs.append(pl.BlockSpec((TQ, BLK), lambda i: (blk(i), 0)))
        args.append(h0)

    return pl.pallas_call(
        functools.partial(_lru_body, reverse=reverse, last=last, tpb=tpb),
        grid=(n_tiles,),
        in_specs=in_specs,
        out_specs=pl.BlockSpec((TQ, BLK), lambda i: (blk(i), 0)),
        out_shape=jax.ShapeDtypeStruct((t, BLK), BF16 if last else F32),
        scratch_shapes=[pltpu.VMEM((1, BLK), F32)],
        compiler_params=_cp("arbitrary"),
        name="rglru_%d" % direction,
    )(*args)


def _merge_body(hn_ref, wg_ref, ba_ref, bb_ref, bc_ref, bd_ref, wb_ref, y_ref, *, tn):
    g = _dot(hn_ref[...], wg_ref[...])
    acc = None
    for j, br in enumerate((ba_ref, bb_ref, bc_ref, bd_ref)):
        term = _sigmoid(g[:, j * tn:(j + 1) * tn]) * _dot(br[...].astype(BF16), wb_ref[j])
        acc = term if acc is None else acc + term
    y_ref[...] = acc.astype(BF16)


def merge_branches(hn, wg, branches, wb, *, tm=768, tn=256):
    t, d = hn.shape
    br_spec = pl.BlockSpec((tm, BLK), lambda i, j: (i, 0))
    return pl.pallas_call(
        functools.partial(_merge_body, tn=tn),
        grid=(t // tm, d // tn),
        in_specs=[pl.BlockSpec((tm, d), lambda i, j: (i, 0)),
                  pl.BlockSpec((None, d, N_BRANCH * tn), lambda i, j: (j, 0, 0)),
                  br_spec, br_spec, br_spec, br_spec,
                  pl.BlockSpec((N_BRANCH, BLK, tn), lambda i, j: (0, 0, j))],
        out_specs=pl.BlockSpec((tm, tn), lambda i, j: (i, j)),
        out_shape=jax.ShapeDtypeStruct((t, d), BF16),
        compiler_params=_cp("arbitrary", "arbitrary"),
        name="merge_branches",
    )(hn, wg, *branches, wb)


def _post_body(y_ref, wo_ref, x_ref, mod_ref, g2_ref, rw_ref, rb_ref,
               xo_ref, h2_ref, e_ref, w_ref, rank_ref, cnt_ref, base_ref, *, tm, tpb):
    i = pl.program_id(0)
    row0 = (i % tpb) * tm

    @pl.when(i == 0)
    def _():
        base_ref[...] = jnp.zeros_like(base_ref)

    x = x_ref[...] + _mod_rows(mod_ref, 2, row0, tm) * _dot(y_ref[...], wo_ref[...])
    xo_ref[...] = x
    h = _norm_mod(x, g2_ref[...], mod_ref, 3, row0, tm)
    h2_ref[...] = h
    hh, hl = _split(h)
    logits = _dot_nt(rw_ref[0], hh) + _dot_nt(rw_ref[0], hl) + _dot_nt(rw_ref[1], hh)
    s = _sigmoid(logits)
    sb = s + rb_ref[...]
    per = N_EXPERTS // N_GROUPS
    srow = [s[e:e + 1, :] for e in range(N_EXPERTS)]
    brow = [sb[e:e + 1, :] for e in range(N_EXPERTS)]
    gscore = []
    for g in range(N_GROUPS):
        best = None
        for a in range(per):
            for b in range(a + 1, per):
                pair = brow[per * g + a] + brow[per * g + b]
                best = pair if best is None else jnp.maximum(best, pair)
        gscore.append(best)
    gbest = jnp.zeros((1, tm), jnp.int32)
    bscore = gscore[0]
    for g in range(1, N_GROUPS):
        better = gscore[g] > bscore
        gbest = jnp.where(better, g, gbest)
        bscore = jnp.where(better, gscore[g], bscore)

    def pick(rows, i):
        out = rows[i]
        for g in range(1, N_GROUPS):
            out = jnp.where(gbest == g, rows[per * g + i], out)
        return out

    cand = [pick(brow, i) for i in range(per)]
    cval = [pick(srow, i) for i in range(per)]
    i1 = jnp.zeros((1, tm), jnp.int32)
    m1 = cand[0]
    w1 = cval[0]
    for i_ in range(1, per):
        better = cand[i_] > m1
        i1 = jnp.where(better, i_, i1)
        m1 = jnp.where(better, cand[i_], m1)
        w1 = jnp.where(better, cval[i_], w1)
    i2 = jnp.full((1, tm), -1, jnp.int32)
    m2 = jnp.full((1, tm), -jnp.inf, F32)
    w2 = jnp.zeros((1, tm), F32)
    for i_ in range(per):
        better = jnp.logical_and(i1 != i_, cand[i_] > m2)
        i2 = jnp.where(better, i_, i2)
        m2 = jnp.where(better, cand[i_], m2)
        w2 = jnp.where(better, cval[i_], w2)
    e1 = per * gbest + i1
    e2 = per * gbest + i2
    wsum = w1 + w2
    e_ref[0:1, :] = e1
    e_ref[1:2, :] = e2
    w_ref[0:1, :] = w1 / wsum
    w_ref[1:2, :] = w2 / wsum
    er = _iota((N_EXPERTS, tm), 0)
    oh1 = jnp.where(er == e1, 1.0, 0.0)
    oh2 = jnp.where(er == e2, 1.0, 0.0)
    upper = jnp.where(_iota((tm, tm), 0) < _iota((tm, tm), 1), 1.0, 0.0).astype(BF16)
    ex1 = _dot(oh1.astype(BF16), upper)
    ex2 = _dot(oh2.astype(BF16), upper)
    tot1 = jnp.sum(oh1, axis=1, keepdims=True)
    tot2 = jnp.sum(oh2, axis=1, keepdims=True)
    base = base_ref[...]
    rank_ref[0:1, :] = jnp.sum(oh1 * (base + ex1), axis=0, keepdims=True).astype(jnp.int32)
    rank_ref[1:2, :] = jnp.sum(oh2 * (base + tot1 + ex2), axis=0, keepdims=True).astype(jnp.int32)
    base = base + tot1 + tot2
    base_ref[...] = base
    cnt_ref[...] = jnp.broadcast_to(base, cnt_ref.shape)


def post_attention(y, w_out, x, mod, g2, rw, rb, *, tm=384):
    t, d = x.shape
    tpb = ROWS // tm
    full = lambda shape: pl.BlockSpec(shape, lambda i: (0,) * len(shape))
    row = pl.BlockSpec((tm, d), lambda i: (i, 0))
    tok = pl.BlockSpec((2, tm), lambda i: (0, i))
    return pl.pallas_call(
        functools.partial(_post_body, tm=tm, tpb=tpb),
        grid=(t // tm,),
        in_specs=[row, full((d, d)), row, pl.BlockSpec((1, 2, 6, d), lambda i: (i // tpb, 0, 0, 0)),
                  full((1, d)), full((2, N_EXPERTS, d)), full((N_EXPERTS, 1))],
        out_specs=[row, row, tok, tok, tok, full((N_EXPERTS, 128))],
        out_shape=[jax.ShapeDtypeStruct((t, d), F32), jax.ShapeDtypeStruct((t, d), F32),
                   jax.ShapeDtypeStruct((2, t), jnp.int32), jax.ShapeDtypeStruct((2, t), F32),
                   jax.ShapeDtypeStruct((2, t), jnp.int32), jax.ShapeDtypeStruct((N_EXPERTS, 128), F32)],
        scratch_shapes=[pltpu.VMEM((N_EXPERTS, 1), F32)],
        compiler_params=_cp("arbitrary"),
        name="post_attention",
    )(y, w_out, x, mod, g2, rw, rb)


def _row_copy(src, dst, sem):
    return pltpu.make_async_copy(src, dst, sem)


def _dispatch_body(pos_ref, h_ref, xs_in_ref, xs_ref, sem, *, tm):
    del xs_in_ref

    def start(t, carry):
        for k in range(2):
            p = pos_ref[0, 0, k * tm + t]
            _row_copy(h_ref.at[pl.ds(t, 1)], xs_ref.at[pl.ds(p, 1)], sem).start(priority=k)
        return carry

    lax.fori_loop(0, tm, start, 0, unroll=8)
    for k in range(2):
        _row_copy(h_ref, xs_ref.at[pl.ds(0, tm)], sem).wait()


def moe_dispatch(pos3, h2, n_rows, *, tm=256):
    t, d = h2.shape
    xs0 = jnp.zeros((n_rows, d), F32)
    return pl.pallas_call(
        functools.partial(_dispatch_body, tm=tm),
        grid=(t // tm,),
        in_specs=[pl.BlockSpec((1, 1, 2 * tm), lambda i: (i, 0, 0), memory_space=pltpu.SMEM),
                  pl.BlockSpec((tm, d), lambda i: (i, 0)),
                  pl.BlockSpec(memory_space=pl.ANY)],
        out_specs=pl.BlockSpec(memory_space=pl.ANY),
        out_shape=jax.ShapeDtypeStruct((n_rows, d), F32),
        scratch_shapes=[pltpu.SemaphoreType.DMA(())],
        input_output_aliases={2: 0},
        compiler_params=_cp("arbitrary"),
        name="moe_dispatch",
    )(pos3, h2, xs0)


def _ffn_body(te_ref, nv_ref, x_ref, wg_ref, wu_ref, wd_ref, y_ref):
    del te_ref
    i = pl.program_id(0)

    @pl.when(i < nv_ref[0])
    def _():
        x = x_ref[...].astype(BF16)
        g = _dot(x, wg_ref[...])
        u = _dot(x, wu_ref[...])
        y_ref[...] = _dot((g * _sigmoid(g) * u).astype(BF16), wd_ref[...])

    @pl.when(i >= nv_ref[0])
    def _():
        y_ref[...] = jnp.zeros_like(y_ref)


def moe_ffn(tile_expert, n_valid, xs, wg, wu, wd):
    n_rows, d = xs.shape
    ff = wg.shape[-1]
    n_tiles = n_rows // FFN_TM
    grid_spec = pltpu.PrefetchScalarGridSpec(
        num_scalar_prefetch=2,
        grid=(n_tiles,),
        in_specs=[pl.BlockSpec((FFN_TM, d), lambda i, te, nv: (jnp.minimum(i, nv[0] - 1), 0)),
                  pl.BlockSpec((None, d, ff), lambda i, te, nv: (te[i], 0, 0)),
                  pl.BlockSpec((None, d, ff), lambda i, te, nv: (te[i], 0, 0)),
                  pl.BlockSpec((None, ff, d), lambda i, te, nv: (te[i], 0, 0))],
        out_specs=pl.BlockSpec((FFN_TM, d), lambda i, te, nv: (i, 0)))
    return pl.pallas_call(
        _ffn_body,
        grid_spec=grid_spec,
        out_shape=jax.ShapeDtypeStruct((n_rows, d), F32),
        compiler_params=_cp("arbitrary"),
        name="moe_ffn",
    )(tile_expert, n_valid, xs, wg, wu, wd)


def _combine_body(pos_ref, ys_ref, x_ref, w_ref, mod_ref, o_ref, buf_ref, sem, *, tm, tpb):
    def start(t, carry):
        for k in range(2):
            p = pos_ref[0, 0, k * tm + t]
            _row_copy(ys_ref.at[pl.ds(p, 1)], buf_ref.at[k, pl.ds(t, 1)], sem).start(priority=k)
        return carry

    lax.fori_loop(0, tm, start, 0, unroll=8)
    for k in range(2):
        _row_copy(ys_ref.at[pl.ds(0, tm)], buf_ref.at[k], sem).wait()
    w = w_ref[...]
    y = w[:, 0:1] * buf_ref[0] + w[:, 1:2] * buf_ref[1]
    row0 = (pl.program_id(0) % tpb) * tm
    o_ref[...] = x_ref[...] + _mod_rows(mod_ref, 5, row0, tm) * y


def moe_combine(pos3, ys, x, wt, mod, *, tm=256):
    t, d = x.shape
    tpb = ROWS // tm
    row = pl.BlockSpec((tm, d), lambda i: (i, 0))
    return pl.pallas_call(
        functools.partial(_combine_body, tm=tm, tpb=tpb),
        grid=(t // tm,),
        in_specs=[pl.BlockSpec((1, 1, 2 * tm), lambda i: (i, 0, 0), memory_space=pltpu.SMEM),
                  pl.BlockSpec(memory_space=pl.ANY), row,
                  pl.BlockSpec((tm, 2), lambda i: (i, 0)),
                  pl.BlockSpec((1, 2, 6, d), lambda i: (i // tpb, 0, 0, 0))],
        out_specs=row,
        out_shape=jax.ShapeDtypeStruct((t, d), F32),
        scratch_shapes=[pltpu.VMEM((2, tm, d), F32), pltpu.SemaphoreType.DMA(())],
        compiler_params=_cp("arbitrary"),
        name="moe_combine",
    )(pos3, ys, x, wt, mod)


def moe_layout(e, rank, counts, *, tm=256):
    t = e.shape[1]
    n_rows_max = -(-(2 * t + N_EXPERTS * (FFN_TM - 1)) // FFN_TM) * FFN_TM
    n_tiles = n_rows_max // FFN_TM
    cnt = counts[:, 0].astype(jnp.int32)
    gsz = ((cnt + FFN_TM - 1) // FFN_TM) * FFN_TM
    end = jnp.cumsum(gsz)
    off = end - gsz
    onehot = e[:, :, None] == jnp.arange(N_EXPERTS, dtype=jnp.int32)
    pos = rank + jnp.sum(jnp.where(onehot, off, 0), axis=-1)
    pos3 = pos.reshape(2, t // tm, tm).transpose(1, 0, 2).reshape(t // tm, 1, 2 * tm)
    n_valid = (end[-1] // FFN_TM).astype(jnp.int32)
    starts = jnp.minimum(jnp.arange(n_tiles, dtype=jnp.int32), n_valid - 1) * FFN_TM
    tile_expert = jnp.sum(starts[:, None] >= end[None, :], axis=1).astype(jnp.int32)
    return pos3, tile_expert, n_valid.reshape(1), n_rows_max


def _block_diag(w):
    n, bi, bj = w.shape[-3:]
    eye = jnp.eye(n, dtype=w.dtype)
    out = w[..., :, :, None, :] * eye[:, None, :, None]
    return out.reshape(w.shape[:-3] + (n * bi, n * bj))


def kernel(x, c, ctx, c_ctx, w_mod, b_mod, norm1_g, norm2_g, w_in, a_mu, a_w0, a_w2, a_a0, a_a2, a_g2, a_kk, a_ka,
           a_rk, a_gn_g, a_gn_b, b_qn, b_kn, b_lam, b_subln, c_qn, c_kn, c_rpb, d_conv_w, d_conv_b, d_wa, d_ba,
           d_wx, d_bx, d_lam, w_branch, w_out, router_w, router_b, e_gate, e_up, e_down):
    nb = x.shape[0]
    d = D_MODEL
    t = nb * ROWS
    xs = jnp.concatenate([ctx, x], axis=1).reshape(t, d)

    cc = jnp.concatenate([c, c_ctx[None], jnp.zeros((-(nb + 1) % 8, d), F32)], axis=0)
    mods = mod_vectors(cc, w_mod, b_mod)
    mod_l = mods[:, :nb].reshape(DEPTH, nb, 1, 6, d)
    mod_c = jnp.broadcast_to(mods[:, nb].reshape(DEPTH, 1, 1, 6, d), (DEPTH, nb, 1, 6, d))
    mod = jnp.concatenate([mod_l, mod_c], axis=2)

    o_b, o_c, o_d, o_g = A_COLS, A_COLS + 1536, A_COLS + 3072, A_COLS + 3072 + 1024
    w_mix = jnp.concatenate([w_in[:, :, :o_b], jnp.zeros((DEPTH, d, 2048 - A_COLS), F32), w_in[:, :, o_b:o_g]],
                            axis=-1).astype(BF16)
    tn_m = 256
    w_gate = w_in[:, :, o_g:].astype(BF16).reshape(DEPTH, d, N_BRANCH, d // tn_m, tn_m)
    w_gate = jnp.transpose(w_gate, (0, 3, 1, 2, 4)).reshape(DEPTH, d // tn_m, d, N_BRANCH * tn_m)
    mu = jnp.pad(a_mu, ((0, 0), (0, 0), (0, 2048 - A_COLS)))
    w2bd = _block_diag(a_w2).astype(BF16)
    a2bd = _block_diag(a_a2).astype(BF16)
    g2 = a_g2.astype(BF16)
    wa_bd = _block_diag(d_wa).astype(BF16)
    wx_bd = _block_diag(d_wx).astype(BF16)
    wb = w_branch.astype(BF16)
    wo = w_out.astype(BF16)
    rw = jnp.stack(_split(router_w.T), axis=0)
    rb = router_b.reshape(N_EXPERTS, 1)
    eg, eu, ed = e_gate.astype(BF16), e_up.astype(BF16), e_down.astype(BF16)
    cos, sin = rope_tables()
    bias = rpb_tables(c_rpb)
    qn_c = jnp.tile(c_qn, (1, C_HEADS)).reshape(DEPTH, 1, BLK)
    kn_c = jnp.tile(c_kn, (1, C_HEADS)).reshape(DEPTH, 1, BLK)

    for l in range(DEPTH):
        lam_init = 0.8 - 0.6 * math.exp(-0.3 * l)
        z, hn = in_projection(xs, mod[l], norm1_g[l].reshape(1, d), w_mix[l])
        prep = rwkv_prepare(z, mu[l], w2bd[l], a2bd[l], g2[l], a_w0[l].reshape(1, 1024), a_a0[l].reshape(1, 1024),
                            a_kk[l].reshape(1, BLK), a_ka[l].reshape(1, BLK), a_rk[l].reshape(1, BLK))
        ya = rwkv_scan(prep, 0)
        oa = rwkv_scan(prep, 1, ya, a_gn_g[l].reshape(1, BLK), a_gn_b[l].reshape(1, BLK))
        ob = diff_attention(z, cos, sin, b_qn[l].reshape(1, 128), b_kn[l].reshape(1, 128), b_lam[l],
                            b_subln[l].reshape(1, 128), lam_init)
        oc = na_attention(z, bias, qn_c[l], kn_c[l], l)
        lru = [(d_conv_w[l], d_conv_b[l].reshape(1, BLK), wa_bd[l, k], d_ba[l, k].reshape(1, BLK), wx_bd[l, k],
                d_bx[l, k].reshape(1, BLK), d_lam[l, k].reshape(1, BLK)) for k in range(2)]
        hd = rglru(z, 0, *lru[0])
        od = rglru(z, 1, *lru[1], h0=hd)
        y = merge_branches(hn, w_gate[l], (oa, ob, oc, od), wb[l], tn=tn_m)
        xs, h2, e, wts, rank, counts = post_attention(y, wo[l], xs, mod[l], norm2_g[l].reshape(1, d), rw, rb)
        pos3, tile_expert, n_valid, n_rows = moe_layout(e, rank, counts)
        xg = moe_dispatch(pos3, h2, n_rows)
        yg = moe_ffn(tile_expert, n_valid, xg, eg[l], eu[l], ed[l])
        xs = moe_combine(pos3, yg, xs, wts.T, mod[l])
    return xs.reshape(nb, ROWS, d)[:, CTX:]
```

```python
import functools
import math

import numpy as np
import jax
import jax.numpy as jnp
from jax import lax
from jax.experimental import pallas as pl
from jax.experimental.pallas import tpu as pltpu

F32 = jnp.float32
BF16 = jnp.bfloat16

D_MODEL = 2048
DEPTH = 4
GRID_W = 64
CTX = 256
SEQ = 2048
ROWS = CTX + SEQ
HEAD = 64
A_WIDTH = 512
A_COLS = 1920
A_GN_EPS = 64e-5
B_HEADS = 4
C_HEADS = 8
NA_ROWS = 8
NA_COLS = 16
D_WIDTH = 512
D_CONV = 4
RG_C = 8.0
N_BRANCH = 4
N_EXPERTS = 16
N_GROUPS = 4
D_FF = 1024
ROPE_BASE = 10000.0
NORM_EPS = 1e-6
Z_COLS = 6144
BLK = 512

VMEM_LIMIT = 56 * 1024 * 1024

CHUNK = 64
TQ = 256
FFN_TM = 512


def _cp(*sem):
    return pltpu.CompilerParams(dimension_semantics=sem, vmem_limit_bytes=VMEM_LIMIT)


def _dot(a, b):
    return jnp.dot(a, b, preferred_element_type=F32)


def _dot_nt(a, b):
    return lax.dot_general(a, b, (((1,), (1,)), ((), ())), preferred_element_type=F32)


def _dot_tn(a, b):
    return lax.dot_general(a, b, (((0,), (0,)), ((), ())), preferred_element_type=F32)


def _split(x):
    hi = x.astype(BF16)
    lo = (x - hi.astype(F32)).astype(BF16)
    return hi, lo


def _dot_x(x, m):
    hi, lo = _split(x)
    return _dot(hi, m) + _dot(lo, m)


def _sigmoid(x):
    return 1.0 / (1.0 + jnp.exp(-x))


def _softplus(x):
    return jnp.maximum(x, 0.0) + jnp.log(1.0 + jnp.exp(-jnp.abs(x)))


def _iota(shape, dim):
    return lax.broadcasted_iota(jnp.int32, shape, dim)


def _seg_ones(width, seg):
    r = np.arange(width)
    return jnp.asarray((r[:, None] // seg) == (r[None, :] // seg), dtype=BF16)


def _mod_body(c_ref, w_ref, b_ref, o_ref):
    c = c_ref[...]
    s = (c * _sigmoid(c)).astype(BF16)
    o_ref[...] = _dot(s, w_ref[...].astype(BF16)) + b_ref[...]


def mod_vectors(cc, w_mod, b_mod):
    depth, d, n = w_mod.shape
    r = cc.shape[0]
    tn = 1536
    return pl.pallas_call(
        _mod_body,
        grid=(depth, n // tn),
        in_specs=[pl.BlockSpec((r, d), lambda l, j: (0, 0)),
                  pl.BlockSpec((None, d, tn), lambda l, j: (l, 0, j)),
                  pl.BlockSpec((None, 1, tn), lambda l, j: (l, 0, j))],
        out_specs=pl.BlockSpec((None, r, tn), lambda l, j: (l, 0, j)),
        out_shape=jax.ShapeDtypeStruct((depth, r, n), F32),
        compiler_params=_cp("arbitrary", "arbitrary"),
        name="mod_vectors",
    )(cc, w_mod, b_mod.reshape(depth, 1, n))


def _mod_rows(mod_ref, idx, row0, tm):
    rows = row0 + _iota((tm, 1), 0)
    return jnp.where(rows < CTX, mod_ref[0, 1, idx:idx + 1, :], mod_ref[0, 0, idx:idx + 1, :])


def _norm_mod(x, g, mod_ref, shift_idx, row0, tm):
    ms = jnp.mean(x * x, axis=-1, keepdims=True)
    y = x * lax.rsqrt(ms + NORM_EPS) * g
    return y * (1.0 + _mod_rows(mod_ref, shift_idx + 1, row0, tm)) + _mod_rows(mod_ref, shift_idx, row0, tm)


def _inproj_body(x_ref, mod_ref, g_ref, w_ref, z_ref, hn_ref, *, tm, tpb, slab):
    i = pl.program_id(0)

    @pl.when(pl.program_id(1) == 0)
    def _():
        for r in range(0, tm, slab):
            row0 = (i % tpb) * tm + r
            hn = _norm_mod(x_ref[r:r + slab, :], g_ref[...], mod_ref, 0, row0, slab).astype(BF16)
            hn_ref[r:r + slab, :] = hn
            z_ref[r:r + slab, :] = _dot(hn, w_ref[...])

    @pl.when(pl.program_id(1) != 0)
    def _():
        z_ref[...] = _dot(hn_ref[...], w_ref[...])


def in_projection(x, mod, g, w, layer, *, tm=1152, tn=512):
    t, d = x.shape
    n = w.shape[2]
    tpb = ROWS // tm
    return pl.pallas_call(
        functools.partial(_inproj_body, tm=tm, tpb=tpb, slab=384),
        grid=(t // tm, n // tn),
        in_specs=[pl.BlockSpec((tm, d), lambda i, j: (i, 0)),
                  pl.BlockSpec((1, 2, 6, d), lambda i, j: (i // tpb, 0, 0, 0)),
                  pl.BlockSpec((1, d), lambda i, j: (0, 0)),
                  pl.BlockSpec((None, d, tn), lambda i, j: (layer, 0, j))],
        out_specs=[pl.BlockSpec((tm, tn), lambda i, j: (i, j)),
                   pl.BlockSpec((tm, d), lambda i, j: (i, 0))],
        out_shape=[jax.ShapeDtypeStruct((t, n), F32), jax.ShapeDtypeStruct((t, d), BF16)],
        compiler_params=_cp("arbitrary", "arbitrary"),
        name="in_projection",
    )(x, mod, g, w)


def _tile_flags(ti, tpb):
    ctx_tiles = CTX // TQ
    has_prev = jnp.logical_and(ti != 0, ti != ctx_tiles)
    has_next = jnp.logical_and(ti != ctx_tiles - 1, ti != tpb - 1)
    return has_prev, has_next


def _halo_specs(width, col, n_tiles):
    r8 = TQ // 8
    return [pl.BlockSpec((TQ, width), lambda i: (i, col)),
            pl.BlockSpec((8, width), lambda i: (jnp.maximum(i * r8 - 1, 0), col)),
            pl.BlockSpec((8, width), lambda i: (jnp.minimum((i + 1) * r8, n_tiles * r8 - 1), col))]


def _shift_rows(z, halo_row, k, use_halo):
    n = z.shape[0]
    rows = _iota((n, 1), 0)
    fill = jnp.where(use_halo, halo_row, 0.0)
    if k > 0:
        return jnp.where(rows < k, fill, pltpu.roll(z, k, 0))
    return jnp.where(rows >= n + k, fill, pltpu.roll(z, n + k, 0))


def _rwkv_prep_body(z_ref, zp_ref, zn_ref, mu_ref, w2_ref, a2_ref, g2_ref, w0_ref, a0_ref,
                    kk_ref, ka_ref, rk_ref, seg_ref, o_ref, *, tpb):
    has_prev, has_next = _tile_flags(pl.program_id(0) % tpb, tpb)
    z = z_ref[...]
    prev = _shift_rows(z, zp_ref[7:8, :], 1, has_prev)
    nxt = _shift_rows(z, zn_ref[0:1, :], -1, has_next)
    zs = z + mu_ref[0:1, :] * (prev - z) + mu_ref[1:2, :] * (nxt - z)
    r = zs[:, 0:512]
    k = zs[:, 512:1024]
    v = zs[:, 1024:1536]
    wl = zs[:, 1536:1664]
    al = zs[:, 1664:1792]
    gl = zs[:, 1792:1920]
    seg = seg_ref[...]
    wlog = -_softplus(-(w0_ref[...] + _dot(jnp.tanh(wl).astype(BF16), w2_ref[...]))) - 0.5
    lw = -jnp.exp(wlog)
    a = _sigmoid(a0_ref[...] + _dot(al.astype(BF16), a2_ref[...]))
    g = _dot(_sigmoid(gl).astype(BF16), g2_ref[...])
    kk = k * kk_ref[...]
    kk = kk / jnp.maximum(jnp.sqrt(_dot_x(kk * kk, seg)), 1e-12)
    bonus = _dot_x(r * k * rk_ref[...], seg) * v
    ka = ka_ref[...]
    o_ref[:, 0:512] = r
    o_ref[:, 512:1024] = v
    o_ref[:, 1024:1536] = kk
    o_ref[:, 1536:2048] = bonus
    o_ref[:, 2048:2560] = g
    for d in range(2):
        ad = a[:, d * 512:(d + 1) * 512]
        base = 2560 + d * 1536
        o_ref[:, base:base + 512] = lw[:, d * 512:(d + 1) * 512]
        o_ref[:, base + 512:base + 1024] = kk * ad
        o_ref[:, base + 1024:base + 1536] = k * (1.0 + (ad - 1.0) * ka)


A_PREP_COLS = 2560 + 2 * 1536


def rwkv_prepare(z, mu, w2bd, a2bd, g2, w0, a0, k_k, k_a, r_k):
    t = z.shape[0]
    n_tiles = t // TQ
    tpb = ROWS // TQ
    full = lambda shape: pl.BlockSpec(shape, lambda i: (0,) * len(shape))
    return pl.pallas_call(
        functools.partial(_rwkv_prep_body, tpb=tpb),
        grid=(n_tiles,),
        in_specs=_halo_specs(2048, 0, n_tiles) + [
            full((2, 2048)), full((128, 1024)), full((128, 1024)), full((128, 512)),
            full((1, 1024)), full((1, 1024)), full((1, 512)), full((1, 512)), full((1, 512)),
            full((512, 512))],
        out_specs=pl.BlockSpec((TQ, A_PREP_COLS), lambda i: (i, 0)),
        out_shape=jax.ShapeDtypeStruct((t, A_PREP_COLS), F32),
        compiler_params=_cp("arbitrary"),
        name="rwkv_prepare",
    )(z, z, z, mu, w2bd, a2bd, g2, w0, a0, k_k, k_a, r_k, _seg_ones(512, HEAD))


def _rwkv_scan_body(r_ref, v_ref, kk_ref, lw_ref, b_ref, ke_ref, *rest, reverse, last, nbb):
    if last:
        y0_ref, bonus_ref, g_ref, gng_ref, gnb_ref, seg_ref, o_ref, s_ref = rest
    else:
        o_ref, s_ref = rest
    c = CHUNK
    nh = A_WIDTH // HEAD

    @pl.when(pl.program_id(1) == 0)
    def _():
        s_ref[...] = jnp.zeros_like(s_ref)

    ti = _iota((c, c), 0)
    si = _iota((c, c), 1)
    before = (si > ti) if reverse else (si < ti)
    upto = jnp.logical_or(before, si == ti)
    tri = jnp.where(upto, 1.0, 0.0).astype(BF16)
    eye = jnp.where(si == ti, 1.0, 0.0)
    pair = (ti >> 1) == (si >> 1)
    off_masks = []
    lg = 1
    while (2 << lg) <= c:
        off_masks.append(jnp.logical_and((ti >> (lg + 1)) == (si >> (lg + 1)), (ti >> lg) != (si >> lg)))
        lg += 1
    ti2 = _iota((c, 2 * c), 0)
    si2 = _iota((c, 2 * c), 1) % c
    upto2 = (si2 >= ti2) if reverse else (si2 <= ti2)
    hi2 = _iota((c, 2 * c), 1) >= c
    before_hi = jnp.logical_and(hi2, (si2 > ti2) if reverse else (si2 < ti2))

    chains = [(bi, h) for bi in range(nbb) for h in range(nh)]
    cols = {}
    gammas = []
    for bi in range(nbb):
        lw = lw_ref[bi]
        lw_hi, lw_lo = _split(lw)
        cs = _dot(tri, lw_hi) + _dot(tri, lw_lo)
        tot = jnp.sum(lw, axis=0, keepdims=True)
        e_neg = jnp.exp(-cs)
        e_hat = jnp.exp(tot - cs)
        gammas.append(jnp.exp(tot))
        b = b_ref[bi]
        ke = ke_ref[bi]
        mats = dict(r=r_ref[bi] * jnp.exp(cs), a=-kk_ref[bi] * jnp.exp(cs - lw), b=b * e_neg, k=ke * e_neg,
                    bh=b * e_hat, kh=ke * e_hat, v=v_ref[bi])
        for name, m in mats.items():
            for h in range(nh):
                cols[name, bi, h] = m[:, h * HEAD:(h + 1) * HEAD].astype(BF16)

    def per_chain(fn):
        return [fn(i, bi, h) for i, (bi, h) in enumerate(chains)]

    prod = per_chain(lambda i, bi, h: _dot_nt(
        jnp.concatenate([cols['a', bi, h], cols['r', bi, h]], axis=0),
        jnp.concatenate([cols['b', bi, h], cols['k', bi, h]], axis=0)))
    lab = per_chain(lambda i, bi, h: jnp.where(before, prod[i][0:c, 0:c], 0.0))
    lak2 = per_chain(lambda i, bi, h: jnp.where(before_hi, prod[i][0:c, :], 0.0).astype(BF16))
    m = per_chain(lambda i, bi, h: jnp.where(upto2, prod[i][c:2 * c, :], 0.0).astype(BF16))
    tinv = per_chain(lambda i, bi, h: eye + jnp.where(pair, lab[i], 0.0))
    for off_mask in off_masks:
        tb = [t.astype(BF16) for t in tinv]
        x = per_chain(lambda i, bi, h: _dot(jnp.where(off_mask, lab[i], 0.0).astype(BF16), tb[i]))
        tinv = per_chain(lambda i, bi, h: tinv[i] + _dot(tb[i], x[i].astype(BF16)))
    vv = per_chain(lambda i, bi, h: jnp.concatenate([cols['v', bi, h], cols['v', bi, h]], axis=0))
    lv = per_chain(lambda i, bi, h: _dot(lak2[i], vv[i]))
    tb = [t.astype(BF16) for t in tinv]
    w1 = per_chain(lambda i, bi, h: _dot(tb[i], cols['a', bi, h]))
    u2 = per_chain(lambda i, bi, h: _dot(tb[i], lv[i].astype(BF16)))
    s0 = per_chain(lambda i, bi, h: s_ref[bi, h])
    s0b = [s.astype(BF16) for s in s0]
    ws = per_chain(lambda i, bi, h: _dot_nt(
        jnp.concatenate([w1[i].astype(BF16), cols['r', bi, h]], axis=0), s0b[i]))
    u = per_chain(lambda i, bi, h: ws[i][0:c] + u2[i])
    uv = per_chain(lambda i, bi, h: jnp.concatenate([u[i].astype(BF16), cols['v', bi, h]], axis=0))
    ys = per_chain(lambda i, bi, h: ws[i][c:2 * c] + _dot(m[i], uv[i]))
    for i, (bi, h) in enumerate(chains):
        bkh = jnp.concatenate([cols['bh', bi, h], cols['kh', bi, h]], axis=0)
        s_ref[bi, h] = s0[i] * gammas[bi][:, h * HEAD:(h + 1) * HEAD] + _dot_tn(uv[i], bkh)
    for bi in range(nbb):
        y = jnp.concatenate(ys[bi * nh:(bi + 1) * nh], axis=1)
        if last:
            y = y + y0_ref[bi]
            seg = seg_ref[...]
            inv = 1.0 / HEAD
            mean = _dot_x(y, seg) * inv
            yc = y - mean
            var = _dot_x(yc * yc, seg) * inv
            yn = yc * lax.rsqrt(var + A_GN_EPS) * gng_ref[...] + gnb_ref[...]
            o_ref[bi] = ((yn + bonus_ref[bi]) * g_ref[bi]).astype(o_ref.dtype)
        else:
            o_ref[bi] = y


def rwkv_scan(prep, direction, y0=None, gn_g=None, gn_b=None, *, nbb=2):
    t = prep.shape[0]
    nb = t // ROWS
    cpb = ROWS // CHUNK
    cctx = CTX // CHUNK
    reverse = direction == 1
    last = y0 is not None
    prep3 = prep.reshape(nb, ROWS, prep.shape[1])

    def blk(p):
        if reverse:
            return jnp.where(p < cctx, cctx - 1 - p, cpb - 1 + cctx - p)
        return p

    def col(cb):
        return pl.BlockSpec((nbb, CHUNK, BLK), lambda g, p: (g, blk(p), cb))

    base = 5 + 3 * direction
    in_specs = [col(0), col(1), col(2), col(base), col(base + 1), col(base + 2)]
    args = [prep3] * 6
    if last:
        in_specs += [col(0), col(3), col(4),
                     pl.BlockSpec((1, BLK), lambda g, p: (0, 0)), pl.BlockSpec((1, BLK), lambda g, p: (0, 0)),
                     pl.BlockSpec((BLK, BLK), lambda g, p: (0, 0))]
        args += [y0.reshape(nb, ROWS, BLK), prep3, prep3, gn_g, gn_b, _seg_ones(512, HEAD)]
    out = pl.pallas_call(
        functools.partial(_rwkv_scan_body, reverse=reverse, last=last, nbb=nbb),
        grid=(nb // nbb, cpb),
        in_specs=in_specs,
        out_specs=col(0),
        out_shape=jax.ShapeDtypeStruct((nb, ROWS, BLK), BF16 if last else F32),
        scratch_shapes=[pltpu.VMEM((nbb, A_WIDTH // HEAD, HEAD, HEAD), F32)],
        compiler_params=_cp("arbitrary", "arbitrary"),
        name="rwkv_scan_%d" % direction,
    )(*args)
    return out.reshape(t, BLK)


def _rope(x, cos, sin):
    lane = _iota(x.shape, 1)
    partner = jnp.where((lane % HEAD) < HEAD // 2, pltpu.roll(x, 128 - HEAD // 2, 1), pltpu.roll(x, HEAD // 2, 1))
    return x * cos + partner * sin


def _head_rms(x, gain, seg):
    ms = _dot_x(x * x, seg) * (1.0 / HEAD)
    return x * lax.rsqrt(ms + NORM_EPS) * gain


def _diff_body(q_ref, k_ref, v_ref, cos_ref, sin_ref, qn_ref, kn_ref, lam_ref, sub_ref, seg_ref,
               o_ref, kh_ref, vh_ref, *, lam_init, nq):
    iq = pl.program_id(2)
    seg = seg_ref[...]

    @pl.when(iq == 0)
    def _():
        kh = _rope(_head_rms(k_ref[...], kn_ref[...], seg), cos_ref[...], sin_ref[...])
        kh_ref[...] = kh.astype(BF16)
        vh_ref[:, 0:128] = v_ref[...].astype(BF16)
        vh_ref[:, 128:256] = jnp.ones((ROWS, 128), BF16)

    lv = lam_ref[...]
    lam = (jnp.exp(jnp.sum(lv[0:1] * lv[1:2], axis=1, keepdims=True))
           - jnp.exp(jnp.sum(lv[2:3] * lv[3:4], axis=1, keepdims=True)) + lam_init)
    r0 = pl.multiple_of(iq * TQ, TQ)
    q = _rope(_head_rms(q_ref[...], qn_ref[...], seg), cos_ref[pl.ds(r0, TQ), :], sin_ref[pl.ds(r0, TQ), :])
    q = q * (HEAD ** -0.5 * math.log2(math.e))
    lane = _iota(q.shape, 1)
    q1 = jnp.where(lane < HEAD, q, 0.0).astype(BF16)
    q2 = jnp.where(lane >= HEAD, q, 0.0).astype(BF16)

    def attend(kh, vh):
        s = [_dot_nt(qs, kh) for qs in (q1, q2)]
        e = [jnp.exp2(x - jnp.max(x, axis=-1, keepdims=True)) for x in s]
        pv = [_dot(x.astype(BF16), vh) for x in e]
        o = pv[0][:, 0:128] / pv[0][:, 128:256] - lam * (pv[1][:, 0:128] / pv[1][:, 128:256])
        ms = jnp.mean(o * o, axis=-1, keepdims=True)
        o_ref[...] = (o * lax.rsqrt(ms + NORM_EPS) * sub_ref[...] * (1.0 - lam_init)).astype(o_ref.dtype)

    @pl.when(iq == 0)
    def _():
        attend(kh_ref[0:CTX, :], vh_ref[0:CTX, :])

    @pl.when(iq != 0)
    def _():
        attend(kh_ref[...], vh_ref[...])


def diff_attention(z, cos, sin, qn, kn, lam_vecs, subln, lam_init):
    t = z.shape[0]
    nb = t // ROWS
    nq = ROWS // TQ
    cb = BLK // 128
    full = lambda shape: pl.BlockSpec(shape, lambda b, h, i: (0,) * len(shape))
    return pl.pallas_call(
        functools.partial(_diff_body, lam_init=lam_init, nq=nq),
        grid=(nb, B_HEADS, nq),
        in_specs=[pl.BlockSpec((TQ, 128), lambda b, h, i: (b * nq + i, 4 * cb + h)),
                  pl.BlockSpec((ROWS, 128), lambda b, h, i: (b, 5 * cb + h)),
                  pl.BlockSpec((ROWS, 128), lambda b, h, i: (b, 6 * cb + h)),
                  full((ROWS, 128)), full((ROWS, 128)), full((1, 128)), full((1, 128)),
                  full((4, HEAD)), full((1, 128)), full((128, 128))],
        out_specs=pl.BlockSpec((TQ, 128), lambda b, h, i: (b * nq + i, h)),
        out_shape=jax.ShapeDtypeStruct((t, BLK), BF16),
        scratch_shapes=[pltpu.VMEM((ROWS, 128), BF16), pltpu.VMEM((ROWS, 256), BF16)],
        compiler_params=_cp("arbitrary", "arbitrary", "arbitrary"),
        name="diff_attention",
    )(z, z, z, cos, sin, qn, kn, lam_vecs, subln, _seg_ones(128, HEAD))


def rope_tables():
    t = np.arange(SEQ)
    quarter = HEAD // 4
    inv = (1.0 / (ROPE_BASE ** (np.arange(quarter, dtype=np.float32) / quarter))).astype(np.float32)
    row = (t // GRID_W).astype(np.float32)[:, None] * inv
    col = (t % GRID_W).astype(np.float32)[:, None] * inv
    ang = np.concatenate([row, col], axis=-1)
    cos = np.concatenate([np.ones((CTX, HEAD // 2), np.float32), np.cos(ang)], axis=0)
    sin = np.concatenate([np.zeros((CTX, HEAD // 2), np.float32), np.sin(ang)], axis=0)
    cos = np.tile(cos, (1, 4))
    sin = np.tile(np.concatenate([-sin, sin], axis=1), (1, 2))
    return jnp.asarray(cos, F32), jnp.asarray(sin, F32)


def _rpb_body(r_ref, e_ref, o_ref):
    o_ref[...] = _dot_x(r_ref[...] * math.log2(math.e), e_ref[...])


def rpb_tables(rpb):
    depth = rpb.shape[0]
    nr, nc = 2 * NA_ROWS - 1, 2 * NA_COLS - 1
    q = np.arange(GRID_W)
    dc = np.clip(q[None, :] - q[:, None] + NA_COLS - 1, 0, nc - 1)
    onehot = (np.arange(32)[:, None, None] == dc[None]).astype(np.float32).reshape(32, GRID_W * GRID_W)
    r2 = jnp.pad(rpb.reshape(depth * C_HEADS * nr, nc), ((0, 0), (0, 32 - nc)))
    rows = r2.shape[0]
    toe = pl.pallas_call(
        _rpb_body,
        grid=(1,),
        in_specs=[pl.BlockSpec((rows, 32), lambda i: (0, 0)), pl.BlockSpec((32, GRID_W * GRID_W), lambda i: (0, 0))],
        out_specs=pl.BlockSpec((rows, GRID_W * GRID_W), lambda i: (0, 0)),
        out_shape=jax.ShapeDtypeStruct((rows, GRID_W * GRID_W), F32),
        compiler_params=_cp("arbitrary"),
        name="rpb_tables",
    )(r2, jnp.asarray(onehot, BF16))
    toe = toe.reshape(depth, C_HEADS, nr, GRID_W, GRID_W)
    rows = SEQ // GRID_W
    dr = np.zeros((3, NA_QROWS, NA_UNION), np.int32)
    ok_row = np.zeros((3, NA_QROWS, NA_UNION), bool)
    for ty, g in enumerate((0, 1, rows // NA_QROWS - 1)):
        us = _na_union_start(g, rows)
        for i in range(NA_QROWS):
            r = NA_QROWS * g + i
            rs = min(max(r - NA_ROWS // 2, 0), rows - NA_ROWS)
            for j in range(NA_UNION):
                ok_row[ty, i, j] = rs <= us + j < rs + NA_ROWS
                dr[ty, i, j] = min(max(us + j - r + NA_ROWS - 1, 0), nr - 1)
    cstart = np.clip(q - NA_COLS // 2, 0, GRID_W - NA_COLS)
    ok_col = (q[None, :] >= cstart[:, None]) & (q[None, :] < cstart[:, None] + NA_COLS)
    ok = ok_row[:, :, :, None, None] & ok_col[None, None, None]
    tab = toe[:, :, jnp.asarray(dr)]
    tab = jnp.where(jnp.asarray(ok), tab, -jnp.inf)
    tab = jnp.transpose(tab, (0, 2, 1, 3, 5, 4, 6))
    return tab.reshape(depth, 3, C_HEADS, NA_QROWS * GRID_W, NA_UNION * GRID_W)


NA_QROWS = 4
NA_UNION = NA_QROWS + NA_ROWS - 1


def _na_union_start(g, rows):
    lo = NA_QROWS * g - NA_ROWS // 2
    hi = rows - NA_UNION
    if isinstance(g, int):
        return min(max(lo, 0), hi)
    return jnp.clip(lo, 0, hi)


def _na_body(q_ref, k_ref, v_ref, bias_ref, qn_ref, kn_ref, seg_ref, o_ref, kh_ref, vh_ref):
    step = pl.program_id(1)
    seg = seg_ref[...]

    @pl.when(step == 0)
    def _():
        kh_ref[...] = _head_rms(k_ref[...], kn_ref[...], seg).astype(BF16)
        vh_ref[...] = v_ref[...].astype(BF16)

    nq = NA_QROWS * GRID_W
    q = _head_rms(q_ref[...], qn_ref[...], seg) * (HEAD ** -0.5 * math.log2(math.e))
    lane = _iota((nq, 128), 1)
    low = lane < HEAD
    sls = [slice((h // 2) * 128, (h // 2 + 1) * 128) for h in range(C_HEADS)]

    def masked_q(h):
        return jnp.where(low if h % 2 == 0 else jnp.logical_not(low), q[:, sls[h]], 0.0).astype(BF16)

    def store(hs, outs):
        for j in range(0, len(hs), 2):
            o_ref[:, sls[hs[j]]] = jnp.where(low, outs[j], outs[j + 1]).astype(o_ref.dtype)

    head_groups = [list(range(g, g + 4)) for g in range(0, C_HEADS, 4)]

    @pl.when(step == 0)
    def _():
        for hs in head_groups:
            qs = [masked_q(h) for h in hs]
            s = [_dot_nt(qs[j], kh_ref[0:CTX, sls[h]]) for j, h in enumerate(hs)]
            e = [jnp.exp2(x - jnp.max(x, axis=-1, keepdims=True)) for x in s]
            store(hs, [_dot(e[j].astype(BF16), vh_ref[0:CTX, sls[h]]) / jnp.sum(e[j], axis=-1, keepdims=True)
                       for j, h in enumerate(hs)])

    @pl.when(step > 0)
    def _():
        us = _na_union_start(step - 1, SEQ // GRID_W)
        k0 = pl.multiple_of(CTX + us * GRID_W, GRID_W)
        nloc = NA_UNION * GRID_W
        for hs in head_groups:
            qs = [masked_q(h) for h in hs]
            s_loc = [_dot_nt(qs[j], kh_ref[pl.ds(k0, nloc), sls[h]]) + bias_ref[0, h] for j, h in enumerate(hs)]
            s_ctx = [_dot_nt(qs[j], kh_ref[0:CTX, sls[h]]) for j, h in enumerate(hs)]
            mx = [jnp.maximum(jnp.max(a, axis=-1, keepdims=True), jnp.max(b, axis=-1, keepdims=True))
                  for a, b in zip(s_loc, s_ctx)]
            e_loc = [jnp.exp2(a - m_) for a, m_ in zip(s_loc, mx)]
            e_ctx = [jnp.exp2(b - m_) for b, m_ in zip(s_ctx, mx)]
            den = [jnp.sum(a, axis=-1, keepdims=True) + jnp.sum(b, axis=-1, keepdims=True)
                   for a, b in zip(e_loc, e_ctx)]
            store(hs, [(_dot(e_loc[j].astype(BF16), vh_ref[pl.ds(k0, nloc), sls[h]])
                        + _dot(e_ctx[j].astype(BF16), vh_ref[0:CTX, sls[h]])) / den[j] for j, h in enumerate(hs)])


def na_attention(z, bias, qn, kn, layer):
    t = z.shape[0]
    nb = t // ROWS
    nq = NA_QROWS * GRID_W
    assert nq == CTX
    steps = ROWS // nq
    groups = steps - 1

    def layout(i):
        g = jnp.maximum(i - 1, 0)
        return jnp.where(g == 0, 0, jnp.where(g == groups - 1, 2, 1))

    full = lambda shape: pl.BlockSpec(shape, lambda b, i: (0,) * len(shape))
    return pl.pallas_call(
        _na_body,
        grid=(nb, steps),
        in_specs=[pl.BlockSpec((nq, BLK), lambda b, i: (b * steps + i, 7)),
                  pl.BlockSpec((ROWS, BLK), lambda b, i: (b, 8)),
                  pl.BlockSpec((ROWS, BLK), lambda b, i: (b, 9)),
                  pl.BlockSpec((None, 1, C_HEADS, nq, NA_UNION * GRID_W), lambda b, i: (layer, layout(i), 0, 0, 0)),
                  full((1, BLK)), full((1, BLK)), full((BLK, BLK))],
        out_specs=pl.BlockSpec((nq, BLK), lambda b, i: (b * steps + i, 0)),
        out_shape=jax.ShapeDtypeStruct((t, BLK), BF16),
        scratch_shapes=[pltpu.VMEM((ROWS, BLK), BF16), pltpu.VMEM((ROWS, BLK), BF16)],
        compiler_params=_cp("arbitrary", "arbitrary"),
        name="na_attention",
    )(z, z, z, bias, qn, kn, _seg_ones(512, HEAD))


def _lru_body(x_ref, xp_ref, xn_ref, gate_ref, cw_ref, cb_ref, wa_ref, ba_ref, wx_ref, bx_ref, lam_ref,
              *rest, reverse, last, tpb):
    if last:
        h0_ref, o_ref, carry_ref = rest
    else:
        o_ref, carry_ref = rest
    p = pl.program_id(0) % tpb
    has_prev, has_next = _tile_flags(_lru_tile(p, tpb, reverse), tpb)

    @pl.when(p == 0)
    def _():
        carry_ref[...] = jnp.zeros_like(carry_ref)

    x = x_ref[...]
    n = x.shape[0]
    xm1 = _shift_rows(x, xp_ref[7:8, :], 1, has_prev)
    xp1 = _shift_rows(x, xn_ref[0:1, :], -1, has_next)
    rows = _iota((n, 1), 0)
    nx = jnp.where(has_next, xn_ref[0:2, :], 0.0)
    xp2 = pltpu.roll(x, n - 2, 0)
    xp2 = jnp.where(rows == n - 2, nx[0:1, :], jnp.where(rows == n - 1, nx[1:2, :], xp2))
    xr = (cw_ref[0:1, :] * xm1 + cw_ref[1:2, :] * x + cw_ref[2:3, :] * xp1 + cw_ref[3:4, :] * xp2) + cb_ref[...]
    xb = xr.astype(BF16)
    rg = _sigmoid(_dot(xb, wa_ref[...]) + ba_ref[...])
    ig = _sigmoid(_dot(xb, wx_ref[...]) + bx_ref[...])
    log_a = -RG_C * _softplus(-lam_ref[...]) * rg
    a = jnp.exp(log_a)
    b = jnp.sqrt(jnp.maximum(1.0 - jnp.exp(2.0 * log_a), 0.0)) * ig * xr
    k = 1
    while k < n:
        if reverse:
            keep = rows < n - k
            a_s = pltpu.roll(a, n - k, 0)
            b_s = pltpu.roll(b, n - k, 0)
        else:
            keep = rows >= k
            a_s = pltpu.roll(a, k, 0)
            b_s = pltpu.roll(b, k, 0)
        b = jnp.where(keep, a * b_s + b, b)
        a = jnp.where(keep, a * a_s, a)
        k *= 2
    h = a * carry_ref[...] + b
    carry_ref[...] = h[0:1, :] if reverse else h[n - 1:n, :]
    if last:
        g = gate_ref[...]
        gelu = 0.5 * g * (1.0 + jnp.tanh(math.sqrt(2.0 / math.pi) * (g + 0.044715 * g * g * g)))
        o_ref[...] = ((h0_ref[...] + h) * gelu).astype(o_ref.dtype)
    else:
        o_ref[...] = h


def _lru_tile(p, tpb, reverse):
    cctx = CTX // TQ
    if reverse:
        return jnp.where(p < cctx, cctx - 1 - p, tpb - 1 + cctx - p)
    return p


def rglru(z, direction, conv_w, conv_b, wa_bd, ba, wx_bd, bx, lam, h0=None):
    t = z.shape[0]
    n_tiles = t // TQ
    tpb = ROWS // TQ
    reverse = direction == 1
    last = h0 is not None

    def blk(i):
        return (i // tpb) * tpb + _lru_tile(i % tpb, tpb, reverse)

    r8 = TQ // 8
    full = lambda shape: pl.BlockSpec(shape, lambda i: (0,) * len(shape))
    in_specs = [pl.BlockSpec((TQ, BLK), lambda i: (blk(i), 10)),
                pl.BlockSpec((8, BLK), lambda i: (jnp.maximum(blk(i) * r8 - 1, 0), 10)),
                pl.BlockSpec((8, BLK), lambda i: (jnp.minimum((blk(i) + 1) * r8, n_tiles * r8 - 1), 10)),
                pl.BlockSpec((TQ, BLK), lambda i: (blk(i), 11)),
                full((D_CONV, BLK)), full((1, BLK)), full((BLK, BLK)), full((1, BLK)),
                full((BLK, BLK)), full((1, BLK)), full((1, BLK))]
    args = [z, z, z, z, conv_w, conv_b, wa_bd, ba, wx_bd, bx, lam]
    if last:
        in_specs.append(pl.BlockSpec((TQ, BLK), lambda i: (blk(i), 0)))
        args.append(h0)

    return pl.pallas_call(
        functools.partial(_lru_body, reverse=reverse, last=last, tpb=tpb),
        grid=(n_tiles,),
        in_specs=in_specs,
        out_specs=pl.BlockSpec((TQ, BLK), lambda i: (blk(i), 0)),
        out_shape=jax.ShapeDtypeStruct((t, BLK), BF16 if last else F32),
        scratch_shapes=[pltpu.VMEM((1, BLK), F32)],
        compiler_params=_cp("arbitrary"),
        name="rglru_%d" % direction,
    )(*args)


def _merge_body(hn_ref, wg_ref, ba_ref, bb_ref, bc_ref, bd_ref, wb_ref, y_ref, *, tn):
    g = _dot(hn_ref[...], wg_ref[...])
    acc = None
    for j, br in enumerate((ba_ref, bb_ref, bc_ref, bd_ref)):
        term = _sigmoid(g[:, j * tn:(j + 1) * tn]) * _dot(br[...].astype(BF16), wb_ref[j])
        acc = term if acc is None else acc + term
    y_ref[...] = acc.astype(BF16)


def merge_branches(hn, wg, branches, wb, layer, *, tm=768, tn=256):
    t, d = hn.shape
    br_spec = pl.BlockSpec((tm, BLK), lambda i, j: (i, 0))
    return pl.pallas_call(
        functools.partial(_merge_body, tn=tn),
        grid=(t // tm, d // tn),
        in_specs=[pl.BlockSpec((tm, d), lambda i, j: (i, 0)),
                  pl.BlockSpec((None, None, d, N_BRANCH * tn), lambda i, j: (layer, j, 0, 0)),
                  br_spec, br_spec, br_spec, br_spec,
                  pl.BlockSpec((None, N_BRANCH, BLK, tn), lambda i, j: (layer, 0, 0, j))],
        out_specs=pl.BlockSpec((tm, tn), lambda i, j: (i, j)),
        out_shape=jax.ShapeDtypeStruct((t, d), BF16),
        compiler_params=_cp("arbitrary", "arbitrary"),
        name="merge_branches",
    )(hn, wg, *branches, wb)


def _post_body(y_ref, wo_ref, x_ref, mod_ref, g2_ref, rw_ref, rb_ref,
               xo_ref, h2_ref, e_ref, w_ref, rank_ref, cnt_ref, base_ref, *, tm, tpb):
    i = pl.program_id(0)
    row0 = (i % tpb) * tm

    @pl.when(i == 0)
    def _():
        base_ref[...] = jnp.zeros_like(base_ref)

    x = x_ref[...] + _mod_rows(mod_ref, 2, row0, tm) * _dot(y_ref[...], wo_ref[...])
    xo_ref[...] = x
    h = _norm_mod(x, g2_ref[...], mod_ref, 3, row0, tm)
    h2_ref[...] = h
    hh, hl = _split(h)
    logits = _dot_nt(rw_ref[0], hh) + _dot_nt(rw_ref[0], hl) + _dot_nt(rw_ref[1], hh)
    s = _sigmoid(logits)
    sb = s + rb_ref[...]
    per = N_EXPERTS // N_GROUPS
    srow = [s[e:e + 1, :] for e in range(N_EXPERTS)]
    brow = [sb[e:e + 1, :] for e in range(N_EXPERTS)]
    gscore = []
    for g in range(N_GROUPS):
        best = None
        for a in range(per):
            for b in range(a + 1, per):
                pair = brow[per * g + a] + brow[per * g + b]
                best = pair if best is None else jnp.maximum(best, pair)
        gscore.append(best)
    gbest = jnp.zeros((1, tm), jnp.int32)
    bscore = gscore[0]
    for g in range(1, N_GROUPS):
        better = gscore[g] > bscore
        gbest = jnp.where(better, g, gbest)
        bscore = jnp.where(better, gscore[g], bscore)

    def pick(rows, i):
        out = rows[i]
        for g in range(1, N_GROUPS):
            out = jnp.where(gbest == g, rows[per * g + i], out)
        return out

    cand = [pick(brow, i) for i in range(per)]
    cval = [pick(srow, i) for i in range(per)]
    i1 = jnp.zeros((1, tm), jnp.int32)
    m1 = cand[0]
    w1 = cval[0]
    for i_ in range(1, per):
        better = cand[i_] > m1
        i1 = jnp.where(better, i_, i1)
        m1 = jnp.where(better, cand[i_], m1)
        w1 = jnp.where(better, cval[i_], w1)
    i2 = jnp.full((1, tm), -1, jnp.int32)
    m2 = jnp.full((1, tm), -jnp.inf, F32)
    w2 = jnp.zeros((1, tm), F32)
    for i_ in range(per):
        better = jnp.logical_and(i1 != i_, cand[i_] > m2)
        i2 = jnp.where(better, i_, i2)
        m2 = jnp.where(better, cand[i_], m2)
        w2 = jnp.where(better, cval[i_], w2)
    e1 = per * gbest + i1
    e2 = per * gbest + i2
    wsum = w1 + w2
    e_ref[0:1, :] = e1
    e_ref[1:2, :] = e2
    w_ref[0:1, :] = w1 / wsum
    w_ref[1:2, :] = w2 / wsum
    er = _iota((N_EXPERTS, tm), 0)
    oh1 = jnp.where(er == e1, 1.0, 0.0)
    oh2 = jnp.where(er == e2, 1.0, 0.0)
    upper = jnp.where(_iota((tm, tm), 0) < _iota((tm, tm), 1), 1.0, 0.0).astype(BF16)
    ex1 = _dot(oh1.astype(BF16), upper)
    ex2 = _dot(oh2.astype(BF16), upper)
    tot1 = jnp.sum(oh1, axis=1, keepdims=True)
    tot2 = jnp.sum(oh2, axis=1, keepdims=True)
    base = base_ref[...]
    rank_ref[0:1, :] = jnp.sum(oh1 * (base + ex1), axis=0, keepdims=True).astype(jnp.int32)
    rank_ref[1:2, :] = jnp.sum(oh2 * (base + tot1 + ex2), axis=0, keepdims=True).astype(jnp.int32)
    base = base + tot1 + tot2
    base_ref[...] = base
    cnt_ref[...] = jnp.broadcast_to(base, cnt_ref.shape)


def post_attention(y, w_out, x, mod, g2, rw, rb, layer, *, tm=384):
    t, d = x.shape
    tpb = ROWS // tm
    full = lambda shape: pl.BlockSpec(shape, lambda i: (0,) * len(shape))
    row = pl.BlockSpec((tm, d), lambda i: (i, 0))
    tok = pl.BlockSpec((2, tm), lambda i: (0, i))
    return pl.pallas_call(
        functools.partial(_post_body, tm=tm, tpb=tpb),
        grid=(t // tm,),
        in_specs=[row, pl.BlockSpec((None, d, d), lambda i: (layer, 0, 0)), row,
                  pl.BlockSpec((1, 2, 6, d), lambda i: (i // tpb, 0, 0, 0)),
                  full((1, d)), full((2, N_EXPERTS, d)), full((N_EXPERTS, 1))],
        out_specs=[row, row, tok, tok, tok, full((N_EXPERTS, 128))],
        out_shape=[jax.ShapeDtypeStruct((t, d), F32), jax.ShapeDtypeStruct((t, d), F32),
                   jax.ShapeDtypeStruct((2, t), jnp.int32), jax.ShapeDtypeStruct((2, t), F32),
                   jax.ShapeDtypeStruct((2, t), jnp.int32), jax.ShapeDtypeStruct((N_EXPERTS, 128), F32)],
        scratch_shapes=[pltpu.VMEM((N_EXPERTS, 1), F32)],
        compiler_params=_cp("arbitrary"),
        name="post_attention",
    )(y, w_out, x, mod, g2, rw, rb)


def _row_copy(src, dst, sem):
    return pltpu.make_async_copy(src, dst, sem)


def _dispatch_body(pos_ref, h_ref, xs_in_ref, xs_ref, sem, *, tm):
    del xs_in_ref

    def start(t, carry):
        for k in range(2):
            p = pos_ref[0, 0, k * tm + t]
            _row_copy(h_ref.at[pl.ds(t, 1)], xs_ref.at[pl.ds(p, 1)], sem).start(priority=k)
        return carry

    lax.fori_loop(0, tm, start, 0, unroll=8)
    for k in range(2):
        _row_copy(h_ref, xs_ref.at[pl.ds(0, tm)], sem).wait()


def moe_dispatch(pos3, h2, n_rows, *, tm=256):
    t, d = h2.shape
    xs0 = jnp.zeros((n_rows, d), F32)
    return pl.pallas_call(
        functools.partial(_dispatch_body, tm=tm),
        grid=(t // tm,),
        in_specs=[pl.BlockSpec((1, 1, 2 * tm), lambda i: (i, 0, 0), memory_space=pltpu.SMEM),
                  pl.BlockSpec((tm, d), lambda i: (i, 0)),
                  pl.BlockSpec(memory_space=pl.ANY)],
        out_specs=pl.BlockSpec(memory_space=pl.ANY),
        out_shape=jax.ShapeDtypeStruct((n_rows, d), F32),
        scratch_shapes=[pltpu.SemaphoreType.DMA(())],
        input_output_aliases={2: 0},
        compiler_params=_cp("arbitrary"),
        name="moe_dispatch",
    )(pos3, h2, xs0)


def _ffn_body(te_ref, nv_ref, x_ref, wg_ref, wu_ref, wd_ref, y_ref):
    del te_ref
    i = pl.program_id(0)

    @pl.when(i < nv_ref[0])
    def _():
        x = x_ref[...].astype(BF16)
        g = _dot(x, wg_ref[...])
        u = _dot(x, wu_ref[...])
        y_ref[...] = _dot((g * _sigmoid(g) * u).astype(BF16), wd_ref[...])

    @pl.when(i >= nv_ref[0])
    def _():
        y_ref[...] = jnp.zeros_like(y_ref)


def moe_ffn(tile_expert, n_valid, xs, wg, wu, wd, layer):
    n_rows, d = xs.shape
    ff = wg.shape[-1]
    n_tiles = n_rows // FFN_TM
    grid_spec = pltpu.PrefetchScalarGridSpec(
        num_scalar_prefetch=2,
        grid=(n_tiles,),
        in_specs=[pl.BlockSpec((FFN_TM, d), lambda i, te, nv: (jnp.minimum(i, nv[0] - 1), 0)),
                  pl.BlockSpec((None, None, d, ff), lambda i, te, nv: (layer, te[i], 0, 0)),
                  pl.BlockSpec((None, None, d, ff), lambda i, te, nv: (layer, te[i], 0, 0)),
                  pl.BlockSpec((None, None, ff, d), lambda i, te, nv: (layer, te[i], 0, 0))],
        out_specs=pl.BlockSpec((FFN_TM, d), lambda i, te, nv: (i, 0)))
    return pl.pallas_call(
        _ffn_body,
        grid_spec=grid_spec,
        out_shape=jax.ShapeDtypeStruct((n_rows, d), F32),
        compiler_params=_cp("arbitrary"),
        name="moe_ffn",
    )(tile_expert, n_valid, xs, wg, wu, wd)


def _combine_body(pos_ref, ys_ref, x_ref, w_ref, mod_ref, o_ref, buf_ref, sem, *, tm, tpb):
    def start(t, carry):
        for k in range(2):
            p = pos_ref[0, 0, k * tm + t]
            _row_copy(ys_ref.at[pl.ds(p, 1)], buf_ref.at[k, pl.ds(t, 1)], sem).start(priority=k)
        return carry

    lax.fori_loop(0, tm, start, 0, unroll=8)
    for k in range(2):
        _row_copy(ys_ref.at[pl.ds(0, tm)], buf_ref.at[k], sem).wait()
    w = w_ref[...]
    y = w[:, 0:1] * buf_ref[0] + w[:, 1:2] * buf_ref[1]
    row0 = (pl.program_id(0) % tpb) * tm
    o_ref[...] = x_ref[...] + _mod_rows(mod_ref, 5, row0, tm) * y


def moe_combine(pos3, ys, x, wt, mod, *, tm=256):
    t, d = x.shape
    tpb = ROWS // tm
    row = pl.BlockSpec((tm, d), lambda i: (i, 0))
    return pl.pallas_call(
        functools.partial(_combine_body, tm=tm, tpb=tpb),
        grid=(t // tm,),
        in_specs=[pl.BlockSpec((1, 1, 2 * tm), lambda i: (i, 0, 0), memory_space=pltpu.SMEM),
                  pl.BlockSpec(memory_space=pl.ANY), row,
                  pl.BlockSpec((tm, 2), lambda i: (i, 0)),
                  pl.BlockSpec((1, 2, 6, d), lambda i: (i // tpb, 0, 0, 0))],
        out_specs=row,
        out_shape=jax.ShapeDtypeStruct((t, d), F32),
        scratch_shapes=[pltpu.VMEM((2, tm, d), F32), pltpu.SemaphoreType.DMA(())],
        compiler_params=_cp("arbitrary"),
        name="moe_combine",
    )(pos3, ys, x, wt, mod)


def moe_layout(e, rank, counts, *, tm=256):
    t = e.shape[1]
    n_rows_max = -(-(2 * t + N_EXPERTS * (FFN_TM - 1)) // FFN_TM) * FFN_TM
    n_tiles = n_rows_max // FFN_TM
    cnt = counts[:, 0].astype(jnp.int32)
    gsz = ((cnt + FFN_TM - 1) // FFN_TM) * FFN_TM
    end = jnp.cumsum(gsz)
    off = end - gsz
    onehot = e[:, :, None] == jnp.arange(N_EXPERTS, dtype=jnp.int32)
    pos = rank + jnp.sum(jnp.where(onehot, off, 0), axis=-1)
    pos3 = pos.reshape(2, t // tm, tm).transpose(1, 0, 2).reshape(t // tm, 1, 2 * tm)
    n_valid = (end[-1] // FFN_TM).astype(jnp.int32)
    starts = jnp.minimum(jnp.arange(n_tiles, dtype=jnp.int32), n_valid - 1) * FFN_TM
    tile_expert = jnp.sum(starts[:, None] >= end[None, :], axis=1).astype(jnp.int32)
    return pos3, tile_expert, n_valid.reshape(1), n_rows_max


def _block_diag(w):
    n, bi, bj = w.shape[-3:]
    eye = jnp.eye(n, dtype=w.dtype)
    out = w[..., :, :, None, :] * eye[:, None, :, None]
    return out.reshape(w.shape[:-3] + (n * bi, n * bj))


def kernel(x, c, ctx, c_ctx, w_mod, b_mod, norm1_g, norm2_g, w_in, a_mu, a_w0, a_w2, a_a0, a_a2, a_g2, a_kk, a_ka,
           a_rk, a_gn_g, a_gn_b, b_qn, b_kn, b_lam, b_subln, c_qn, c_kn, c_rpb, d_conv_w, d_conv_b, d_wa, d_ba,
           d_wx, d_bx, d_lam, w_branch, w_out, router_w, router_b, e_gate, e_up, e_down):
    nb = x.shape[0]
    d = D_MODEL
    t = nb * ROWS
    xs = jnp.concatenate([ctx, x], axis=1).reshape(t, d)

    cc = jnp.concatenate([c, c_ctx[None], jnp.zeros((-(nb + 1) % 8, d), F32)], axis=0)
    mods = mod_vectors(cc, w_mod, b_mod)
    mod_l = mods[:, :nb].reshape(DEPTH, nb, 1, 6, d)
    mod_c = jnp.broadcast_to(mods[:, nb].reshape(DEPTH, 1, 1, 6, d), (DEPTH, nb, 1, 6, d))
    mod = jnp.concatenate([mod_l, mod_c], axis=2)

    o_b, o_c, o_d, o_g = A_COLS, A_COLS + 1536, A_COLS + 3072, A_COLS + 3072 + 1024
    w_mix = jnp.concatenate([w_in[:, :, :o_b], jnp.zeros((DEPTH, d, 2048 - A_COLS), F32), w_in[:, :, o_b:o_g]],
                            axis=-1).astype(BF16)
    tn_m = 256
    w_gate = w_in[:, :, o_g:].astype(BF16).reshape(DEPTH, d, N_BRANCH, d // tn_m, tn_m)
    w_gate = jnp.transpose(w_gate, (0, 3, 1, 2, 4)).reshape(DEPTH, d // tn_m, d, N_BRANCH * tn_m)
    mu = jnp.pad(a_mu, ((0, 0), (0, 0), (0, 2048 - A_COLS)))
    w2bd = _block_diag(a_w2).astype(BF16)
    a2bd = _block_diag(a_a2).astype(BF16)
    g2 = a_g2.astype(BF16)
    wa_bd = _block_diag(d_wa).astype(BF16)
    wx_bd = _block_diag(d_wx).astype(BF16)
    wb = w_branch.astype(BF16)
    wo = w_out.astype(BF16)
    rw = jnp.stack(_split(router_w.T), axis=0)
    rb = router_b.reshape(N_EXPERTS, 1)
    eg, eu, ed = e_gate.astype(BF16), e_up.astype(BF16), e_down.astype(BF16)
    cos, sin = rope_tables()
    bias = rpb_tables(c_rpb)
    qn_c = jnp.tile(c_qn, (1, C_HEADS)).reshape(DEPTH, 1, BLK)
    kn_c = jnp.tile(c_kn, (1, C_HEADS)).reshape(DEPTH, 1, BLK)

    for l in range(DEPTH):
        lam_init = 0.8 - 0.6 * math.exp(-0.3 * l)
        z, hn = in_projection(xs, mod[l], norm1_g[l].reshape(1, d), w_mix, l)
        prep = rwkv_prepare(z, mu[l], w2bd[l], a2bd[l], g2[l], a_w0[l].reshape(1, 1024), a_a0[l].reshape(1, 1024),
                            a_kk[l].reshape(1, BLK), a_ka[l].reshape(1, BLK), a_rk[l].reshape(1, BLK))
        ya = rwkv_scan(prep, 0)
        oa = rwkv_scan(prep, 1, ya, a_gn_g[l].reshape(1, BLK), a_gn_b[l].reshape(1, BLK))
        ob = diff_attention(z, cos, sin, b_qn[l].reshape(1, 128), b_kn[l].reshape(1, 128), b_lam[l],
                            b_subln[l].reshape(1, 128), lam_init)
        oc = na_attention(z, bias, qn_c[l], kn_c[l], l)
        lru = [(d_conv_w[l], d_conv_b[l].reshape(1, BLK), wa_bd[l, k], d_ba[l, k].reshape(1, BLK), wx_bd[l, k],
                d_bx[l, k].reshape(1, BLK), d_lam[l, k].reshape(1, BLK)) for k in range(2)]
        hd = rglru(z, 0, *lru[0])
        od = rglru(z, 1, *lru[1], h0=hd)
        y = merge_branches(hn, w_gate, (oa, ob, oc, od), wb, l, tn=tn_m)
        xs, h2, e, wts, rank, counts = post_attention(y, wo, xs, mod[l], norm2_g[l].reshape(1, d), rw, rb, l)
        pos3, tile_expert, n_valid, n_rows = moe_layout(e, rank, counts)
        xg = moe_dispatch(pos3, h2, n_rows)
        yg = moe_ffn(tile_expert, n_valid, xg, eg, eu, ed, l)
        xs = moe_combine(pos3, yg, xs, wts.T, mod[l])
    return xs.reshape(nb, ROWS, d)[:, CTX:]
```

```python
import functools
import math

import numpy as np
import jax
import jax.numpy as jnp
from jax import lax
from jax.experimental import pallas as pl
from jax.experimental.pallas import tpu as pltpu

F32 = jnp.float32
BF16 = jnp.bfloat16

D_MODEL = 2048
DEPTH = 4
GRID_W = 64
CTX = 256
SEQ = 2048
ROWS = CTX + SEQ
HEAD = 64
A_WIDTH = 512
A_COLS = 1920
A_GN_EPS = 64e-5
B_HEADS = 4
C_HEADS = 8
NA_ROWS = 8
NA_COLS = 16
D_WIDTH = 512
D_CONV = 4
RG_C = 8.0
N_BRANCH = 4
N_EXPERTS = 16
N_GROUPS = 4
D_FF = 1024
ROPE_BASE = 10000.0
NORM_EPS = 1e-6
Z_COLS = 6144
BLK = 512

VMEM_LIMIT = 56 * 1024 * 1024

CHUNK = 64
TQ = 256
FFN_TM = 512


def _cp(*sem):
    return pltpu.CompilerParams(dimension_semantics=sem, vmem_limit_bytes=VMEM_LIMIT)


def _dot(a, b):
    return jnp.dot(a, b, preferred_element_type=F32)


def _dot_nt(a, b):
    return lax.dot_general(a, b, (((1,), (1,)), ((), ())), preferred_element_type=F32)


def _dot_tn(a, b):
    return lax.dot_general(a, b, (((0,), (0,)), ((), ())), preferred_element_type=F32)


def _split(x):
    hi = x.astype(BF16)
    lo = (x - hi.astype(F32)).astype(BF16)
    return hi, lo


def _dot_x(x, m):
    hi, lo = _split(x)
    return _dot(hi, m) + _dot(lo, m)


def _sigmoid(x):
    return 1.0 / (1.0 + jnp.exp(-x))


def _softplus(x):
    return jnp.maximum(x, 0.0) + jnp.log(1.0 + jnp.exp(-jnp.abs(x)))


def _iota(shape, dim):
    return lax.broadcasted_iota(jnp.int32, shape, dim)


def _seg_ones(width, seg):
    r = np.arange(width)
    return jnp.asarray((r[:, None] // seg) == (r[None, :] // seg), dtype=BF16)


def _mod_body(c_ref, w_ref, b_ref, o_ref):
    c = c_ref[...]
    s = (c * _sigmoid(c)).astype(BF16)
    o_ref[...] = _dot(s, w_ref[...].astype(BF16)) + b_ref[...]


def mod_vectors(cc, w_mod, b_mod):
    depth, d, n = w_mod.shape
    r = cc.shape[0]
    tn = 1536
    return pl.pallas_call(
        _mod_body,
        grid=(depth, n // tn),
        in_specs=[pl.BlockSpec((r, d), lambda l, j: (0, 0)),
                  pl.BlockSpec((None, d, tn), lambda l, j: (l, 0, j)),
                  pl.BlockSpec((None, 1, tn), lambda l, j: (l, 0, j))],
        out_specs=pl.BlockSpec((None, r, tn), lambda l, j: (l, 0, j)),
        out_shape=jax.ShapeDtypeStruct((depth, r, n), F32),
        compiler_params=_cp("arbitrary", "arbitrary"),
        name="mod_vectors",
    )(cc, w_mod, b_mod.reshape(depth, 1, n))


def _mod_rows(mod_ref, idx, row0, tm):
    rows = row0 + _iota((tm, 1), 0)
    return jnp.where(rows < CTX, mod_ref[0, 1, idx:idx + 1, :], mod_ref[0, 0, idx:idx + 1, :])


def _norm_mod(x, g, mod_ref, shift_idx, row0, tm):
    ms = jnp.mean(x * x, axis=-1, keepdims=True)
    y = x * lax.rsqrt(ms + NORM_EPS) * g
    return y * (1.0 + _mod_rows(mod_ref, shift_idx + 1, row0, tm)) + _mod_rows(mod_ref, shift_idx, row0, tm)


def _inproj_body(x_ref, mod_ref, g_ref, w_ref, z_ref, hn_ref, *, tm, tpb, slab):
    i = pl.program_id(0)

    @pl.when(pl.program_id(1) == 0)
    def _():
        for r in range(0, tm, slab):
            row0 = (i % tpb) * tm + r
            hn = _norm_mod(x_ref[r:r + slab, :], g_ref[...], mod_ref, 0, row0, slab).astype(BF16)
            hn_ref[r:r + slab, :] = hn
            z_ref[r:r + slab, :] = _dot(hn, w_ref[...])

    @pl.when(pl.program_id(1) != 0)
    def _():
        z_ref[...] = _dot(hn_ref[...], w_ref[...])


def in_projection(x, mod, g, w, layer, *, tm=1152, tn=512):
    t, d = x.shape
    n = w.shape[2]
    tpb = ROWS // tm
    return pl.pallas_call(
        functools.partial(_inproj_body, tm=tm, tpb=tpb, slab=384),
        grid=(t // tm, n // tn),
        in_specs=[pl.BlockSpec((tm, d), lambda i, j: (i, 0)),
                  pl.BlockSpec((1, 2, 6, d), lambda i, j: (i // tpb, 0, 0, 0)),
                  pl.BlockSpec((1, d), lambda i, j: (0, 0)),
                  pl.BlockSpec((None, d, tn), lambda i, j: (layer, 0, j))],
        out_specs=[pl.BlockSpec((tm, tn), lambda i, j: (i, j)),
                   pl.BlockSpec((tm, d), lambda i, j: (i, 0))],
        out_shape=[jax.ShapeDtypeStruct((t, n), F32), jax.ShapeDtypeStruct((t, d), BF16)],
        compiler_params=_cp("arbitrary", "arbitrary"),
        name="in_projection",
    )(x, mod, g, w)


def _tile_flags(ti, tpb):
    ctx_tiles = CTX // TQ
    has_prev = jnp.logical_and(ti != 0, ti != ctx_tiles)
    has_next = jnp.logical_and(ti != ctx_tiles - 1, ti != tpb - 1)
    return has_prev, has_next


def _halo_specs(width, col, n_tiles):
    r8 = TQ // 8
    return [pl.BlockSpec((TQ, width), lambda i: (i, col)),
            pl.BlockSpec((8, width), lambda i: (jnp.maximum(i * r8 - 1, 0), col)),
            pl.BlockSpec((8, width), lambda i: (jnp.minimum((i + 1) * r8, n_tiles * r8 - 1), col))]


def _shift_rows(z, halo_row, k, use_halo):
    n = z.shape[0]
    rows = _iota((n, 1), 0)
    fill = jnp.where(use_halo, halo_row, 0.0)
    if k > 0:
        return jnp.where(rows < k, fill, pltpu.roll(z, k, 0))
    return jnp.where(rows >= n + k, fill, pltpu.roll(z, n + k, 0))


def _rwkv_prep_body(z_ref, zp_ref, zn_ref, mu_ref, w2_ref, a2_ref, g2_ref, w0_ref, a0_ref,
                    kk_ref, ka_ref, rk_ref, seg_ref, o_ref, *, tpb):
    has_prev, has_next = _tile_flags(pl.program_id(0) % tpb, tpb)
    z = z_ref[...]
    prev = _shift_rows(z, zp_ref[7:8, :], 1, has_prev)
    nxt = _shift_rows(z, zn_ref[0:1, :], -1, has_next)
    zs = z + mu_ref[0:1, :] * (prev - z) + mu_ref[1:2, :] * (nxt - z)
    r = zs[:, 0:512]
    k = zs[:, 512:1024]
    v = zs[:, 1024:1536]
    wl = zs[:, 1536:1664]
    al = zs[:, 1664:1792]
    gl = zs[:, 1792:1920]
    seg = seg_ref[...]
    wlog = -_softplus(-(w0_ref[...] + _dot(jnp.tanh(wl).astype(BF16), w2_ref[...]))) - 0.5
    lw = -jnp.exp(wlog)
    a = _sigmoid(a0_ref[...] + _dot(al.astype(BF16), a2_ref[...]))
    g = _dot(_sigmoid(gl).astype(BF16), g2_ref[...])
    kk = k * kk_ref[...]
    kk = kk / jnp.maximum(jnp.sqrt(_dot_x(kk * kk, seg)), 1e-12)
    bonus = _dot_x(r * k * rk_ref[...], seg) * v
    ka = ka_ref[...]
    o_ref[:, 0:512] = r
    o_ref[:, 512:1024] = v
    o_ref[:, 1024:1536] = kk
    o_ref[:, 1536:2048] = bonus
    o_ref[:, 2048:2560] = g
    for d in range(2):
        ad = a[:, d * 512:(d + 1) * 512]
        base = 2560 + d * 1536
        o_ref[:, base:base + 512] = lw[:, d * 512:(d + 1) * 512]
        o_ref[:, base + 512:base + 1024] = kk * ad
        o_ref[:, base + 1024:base + 1536] = k * (1.0 + (ad - 1.0) * ka)


A_PREP_COLS = 2560 + 2 * 1536


def rwkv_prepare(z, mu, w2bd, a2bd, g2, w0, a0, k_k, k_a, r_k):
    t = z.shape[0]
    n_tiles = t // TQ
    tpb = ROWS // TQ
    full = lambda shape: pl.BlockSpec(shape, lambda i: (0,) * len(shape))
    return pl.pallas_call(
        functools.partial(_rwkv_prep_body, tpb=tpb),
        grid=(n_tiles,),
        in_specs=_halo_specs(2048, 0, n_tiles) + [
            full((2, 2048)), full((128, 1024)), full((128, 1024)), full((128, 512)),
            full((1, 1024)), full((1, 1024)), full((1, 512)), full((1, 512)), full((1, 512)),
            full((512, 512))],
        out_specs=pl.BlockSpec((TQ, A_PREP_COLS), lambda i: (i, 0)),
        out_shape=jax.ShapeDtypeStruct((t, A_PREP_COLS), F32),
        compiler_params=_cp("arbitrary"),
        name="rwkv_prepare",
    )(z, z, z, mu, w2bd, a2bd, g2, w0, a0, k_k, k_a, r_k, _seg_ones(512, HEAD))


def _rwkv_scan_body(r_ref, v_ref, kk_ref, lw_ref, b_ref, ke_ref, *rest, reverse, last, nbb):
    if last:
        y0_ref, bonus_ref, g_ref, gng_ref, gnb_ref, seg_ref, o_ref, s_ref = rest
    else:
        o_ref, s_ref = rest
    c = CHUNK
    nh = A_WIDTH // HEAD

    @pl.when(pl.program_id(1) == 0)
    def _():
        s_ref[...] = jnp.zeros_like(s_ref)

    ti = _iota((c, c), 0)
    si = _iota((c, c), 1)
    before = (si > ti) if reverse else (si < ti)
    upto = jnp.logical_or(before, si == ti)
    tri = jnp.where(upto, 1.0, 0.0).astype(BF16)
    eye = jnp.where(si == ti, 1.0, 0.0)
    pair = (ti >> 1) == (si >> 1)
    off_masks = []
    lg = 1
    while (2 << lg) <= c:
        off_masks.append(jnp.logical_and((ti >> (lg + 1)) == (si >> (lg + 1)), (ti >> lg) != (si >> lg)))
        lg += 1
    ti2 = _iota((c, 2 * c), 0)
    si2 = _iota((c, 2 * c), 1) % c
    upto2 = (si2 >= ti2) if reverse else (si2 <= ti2)
    hi2 = _iota((c, 2 * c), 1) >= c
    before_hi = jnp.logical_and(hi2, (si2 > ti2) if reverse else (si2 < ti2))

    chains = [(bi, h) for bi in range(nbb) for h in range(nh)]
    cols = {}
    gammas = []
    for bi in range(nbb):
        lw = lw_ref[bi]
        lw_hi, lw_lo = _split(lw)
        cs = _dot(tri, lw_hi) + _dot(tri, lw_lo)
        tot = jnp.sum(lw, axis=0, keepdims=True)
        e_neg = jnp.exp(-cs)
        e_hat = jnp.exp(tot - cs)
        gammas.append(jnp.exp(tot))
        b = b_ref[bi]
        ke = ke_ref[bi]
        mats = dict(r=r_ref[bi] * jnp.exp(cs), a=-kk_ref[bi] * jnp.exp(cs - lw), b=b * e_neg, k=ke * e_neg,
                    bh=b * e_hat, kh=ke * e_hat, v=v_ref[bi])
        for name, m in mats.items():
            for h in range(nh):
                cols[name, bi, h] = m[:, h * HEAD:(h + 1) * HEAD].astype(BF16)

    def per_chain(fn):
        return [fn(i, bi, h) for i, (bi, h) in enumerate(chains)]

    prod = per_chain(lambda i, bi, h: _dot_nt(
        jnp.concatenate([cols['a', bi, h], cols['r', bi, h]], axis=0),
        jnp.concatenate([cols['b', bi, h], cols['k', bi, h]], axis=0)))
    lab = per_chain(lambda i, bi, h: jnp.where(before, prod[i][0:c, 0:c], 0.0))
    lak2 = per_chain(lambda i, bi, h: jnp.where(before_hi, prod[i][0:c, :], 0.0).astype(BF16))
    m = per_chain(lambda i, bi, h: jnp.where(upto2, prod[i][c:2 * c, :], 0.0).astype(BF16))
    tinv = per_chain(lambda i, bi, h: eye + jnp.where(pair, lab[i], 0.0))
    for off_mask in off_masks:
        tb = [t.astype(BF16) for t in tinv]
        x = per_chain(lambda i, bi, h: _dot(jnp.where(off_mask, lab[i], 0.0).astype(BF16), tb[i]))
        tinv = per_chain(lambda i, bi, h: tinv[i] + _dot(tb[i], x[i].astype(BF16)))
    vv = per_chain(lambda i, bi, h: jnp.concatenate([cols['v', bi, h], cols['v', bi, h]], axis=0))
    lv = per_chain(lambda i, bi, h: _dot(lak2[i], vv[i]))
    tb = [t.astype(BF16) for t in tinv]
    w1 = per_chain(lambda i, bi, h: _dot(tb[i], cols['a', bi, h]))
    u2 = per_chain(lambda i, bi, h: _dot(tb[i], lv[i].astype(BF16)))
    s0 = per_chain(lambda i, bi, h: s_ref[bi, h])
    s0b = [s.astype(BF16) for s in s0]
    ws = per_chain(lambda i, bi, h: _dot_nt(
        jnp.concatenate([w1[i].astype(BF16), cols['r', bi, h]], axis=0), s0b[i]))
    u = per_chain(lambda i, bi, h: ws[i][0:c] + u2[i])
    uv = per_chain(lambda i, bi, h: jnp.concatenate([u[i].astype(BF16), cols['v', bi, h]], axis=0))
    ys = per_chain(lambda i, bi, h: ws[i][c:2 * c] + _dot(m[i], uv[i]))
    for i, (bi, h) in enumerate(chains):
        bkh = jnp.concatenate([cols['bh', bi, h], cols['kh', bi, h]], axis=0)
        s_ref[bi, h] = s0[i] * gammas[bi][:, h * HEAD:(h + 1) * HEAD] + _dot_tn(uv[i], bkh)
    for bi in range(nbb):
        y = jnp.concatenate(ys[bi * nh:(bi + 1) * nh], axis=1)
        if last:
            y = y + y0_ref[bi]
            seg = seg_ref[...]
            inv = 1.0 / HEAD
            mean = _dot_x(y, seg) * inv
            yc = y - mean
            var = _dot_x(yc * yc, seg) * inv
            yn = yc * lax.rsqrt(var + A_GN_EPS) * gng_ref[...] + gnb_ref[...]
            o_ref[bi] = ((yn + bonus_ref[bi]) * g_ref[bi]).astype(o_ref.dtype)
        else:
            o_ref[bi] = y


def rwkv_scan(prep, direction, y0=None, gn_g=None, gn_b=None, *, nbb=2):
    t = prep.shape[0]
    nb = t // ROWS
    cpb = ROWS // CHUNK
    cctx = CTX // CHUNK
    reverse = direction == 1
    last = y0 is not None
    prep3 = prep.reshape(nb, ROWS, prep.shape[1])

    def blk(p):
        if reverse:
            return jnp.where(p < cctx, cctx - 1 - p, cpb - 1 + cctx - p)
        return p

    def col(cb):
        return pl.BlockSpec((nbb, CHUNK, BLK), lambda g, p: (g, blk(p), cb))

    base = 5 + 3 * direction
    in_specs = [col(0), col(1), col(2), col(base), col(base + 1), col(base + 2)]
    args = [prep3] * 6
    if last:
        in_specs += [col(0), col(3), col(4),
                     pl.BlockSpec((1, BLK), lambda g, p: (0, 0)), pl.BlockSpec((1, BLK), lambda g, p: (0, 0)),
                     pl.BlockSpec((BLK, BLK), lambda g, p: (0, 0))]
        args += [y0.reshape(nb, ROWS, BLK), prep3, prep3, gn_g, gn_b, _seg_ones(512, HEAD)]
    out = pl.pallas_call(
        functools.partial(_rwkv_scan_body, reverse=reverse, last=last, nbb=nbb),
        grid=(nb // nbb, cpb),
        in_specs=in_specs,
        out_specs=col(0),
        out_shape=jax.ShapeDtypeStruct((nb, ROWS, BLK), BF16 if last else F32),
        scratch_shapes=[pltpu.VMEM((nbb, A_WIDTH // HEAD, HEAD, HEAD), F32)],
        compiler_params=_cp("arbitrary", "arbitrary"),
        name="rwkv_scan_%d" % direction,
    )(*args)
    return out.reshape(t, BLK)


def _rope(x, cos, sin):
    lane = _iota(x.shape, 1)
    partner = jnp.where((lane % HEAD) < HEAD // 2, pltpu.roll(x, 128 - HEAD // 2, 1), pltpu.roll(x, HEAD // 2, 1))
    return x * cos + partner * sin


def _head_rms(x, gain, seg):
    ms = _dot_x(x * x, seg) * (1.0 / HEAD)
    return x * lax.rsqrt(ms + NORM_EPS) * gain


def _diff_body(q_ref, k_ref, v_ref, cos_ref, sin_ref, qn_ref, kn_ref, lam_ref, sub_ref, seg_ref,
               o_ref, kh_ref, vh_ref, *, lam_init, nq):
    iq = pl.program_id(2)
    seg = seg_ref[...]

    @pl.when(iq == 0)
    def _():
        kh = _rope(_head_rms(k_ref[...], kn_ref[...], seg), cos_ref[...], sin_ref[...])
        kh_ref[...] = kh.astype(BF16)
        vh_ref[:, 0:128] = v_ref[...].astype(BF16)
        vh_ref[:, 128:256] = jnp.ones((ROWS, 128), BF16)

    lv = lam_ref[...]
    lam = (jnp.exp(jnp.sum(lv[0:1] * lv[1:2], axis=1, keepdims=True))
           - jnp.exp(jnp.sum(lv[2:3] * lv[3:4], axis=1, keepdims=True)) + lam_init)
    r0 = pl.multiple_of(iq * TQ, TQ)
    q = _rope(_head_rms(q_ref[...], qn_ref[...], seg), cos_ref[pl.ds(r0, TQ), :], sin_ref[pl.ds(r0, TQ), :])
    q = q * (HEAD ** -0.5 * math.log2(math.e))
    lane = _iota(q.shape, 1)
    q1 = jnp.where(lane < HEAD, q, 0.0).astype(BF16)
    q2 = jnp.where(lane >= HEAD, q, 0.0).astype(BF16)

    def attend(kh, vh):
        s = [_dot_nt(qs, kh) for qs in (q1, q2)]
        e = [jnp.exp2(x - jnp.max(x, axis=-1, keepdims=True)) for x in s]
        pv = [_dot(x.astype(BF16), vh) for x in e]
        o = pv[0][:, 0:128] / pv[0][:, 128:256] - lam * (pv[1][:, 0:128] / pv[1][:, 128:256])
        ms = jnp.mean(o * o, axis=-1, keepdims=True)
        o_ref[...] = (o * lax.rsqrt(ms + NORM_EPS) * sub_ref[...] * (1.0 - lam_init)).astype(o_ref.dtype)

    @pl.when(iq == 0)
    def _():
        attend(kh_ref[0:CTX, :], vh_ref[0:CTX, :])

    @pl.when(iq != 0)
    def _():
        attend(kh_ref[...], vh_ref[...])


def diff_attention(z, cos, sin, qn, kn, lam_vecs, subln, lam_init):
    t = z.shape[0]
    nb = t // ROWS
    nq = ROWS // TQ
    cb = BLK // 128
    full = lambda shape: pl.BlockSpec(shape, lambda b, h, i: (0,) * len(shape))
    return pl.pallas_call(
        functools.partial(_diff_body, lam_init=lam_init, nq=nq),
        grid=(nb, B_HEADS, nq),
        in_specs=[pl.BlockSpec((TQ, 128), lambda b, h, i: (b * nq + i, 4 * cb + h)),
                  pl.BlockSpec((ROWS, 128), lambda b, h, i: (b, 5 * cb + h)),
                  pl.BlockSpec((ROWS, 128), lambda b, h, i: (b, 6 * cb + h)),
                  full((ROWS, 128)), full((ROWS, 128)), full((1, 128)), full((1, 128)),
                  full((4, HEAD)), full((1, 128)), full((128, 128))],
        out_specs=pl.BlockSpec((TQ, 128), lambda b, h, i: (b * nq + i, h)),
        out_shape=jax.ShapeDtypeStruct((t, BLK), BF16),
        scratch_shapes=[pltpu.VMEM((ROWS, 128), BF16), pltpu.VMEM((ROWS, 256), BF16)],
        compiler_params=_cp("arbitrary", "arbitrary", "arbitrary"),
        name="diff_attention",
    )(z, z, z, cos, sin, qn, kn, lam_vecs, subln, _seg_ones(128, HEAD))


def rope_tables():
    t = np.arange(SEQ)
    quarter = HEAD // 4
    inv = (1.0 / (ROPE_BASE ** (np.arange(quarter, dtype=np.float32) / quarter))).astype(np.float32)
    row = (t // GRID_W).astype(np.float32)[:, None] * inv
    col = (t % GRID_W).astype(np.float32)[:, None] * inv
    ang = np.concatenate([row, col], axis=-1)
    cos = np.concatenate([np.ones((CTX, HEAD // 2), np.float32), np.cos(ang)], axis=0)
    sin = np.concatenate([np.zeros((CTX, HEAD // 2), np.float32), np.sin(ang)], axis=0)
    cos = np.tile(cos, (1, 4))
    sin = np.tile(np.concatenate([-sin, sin], axis=1), (1, 2))
    return jnp.asarray(cos, F32), jnp.asarray(sin, F32)


def _rpb_body(r_ref, e_ref, o_ref):
    o_ref[...] = _dot_x(r_ref[...] * math.log2(math.e), e_ref[...])


def rpb_tables(rpb):
    depth = rpb.shape[0]
    nr, nc = 2 * NA_ROWS - 1, 2 * NA_COLS - 1
    q = np.arange(GRID_W)
    dc = np.clip(q[None, :] - q[:, None] + NA_COLS - 1, 0, nc - 1)
    onehot = (np.arange(32)[:, None, None] == dc[None]).astype(np.float32).reshape(32, GRID_W * GRID_W)
    r2 = jnp.pad(rpb.reshape(depth * C_HEADS * nr, nc), ((0, 0), (0, 32 - nc)))
    rows = r2.shape[0]
    toe = pl.pallas_call(
        _rpb_body,
        grid=(1,),
        in_specs=[pl.BlockSpec((rows, 32), lambda i: (0, 0)), pl.BlockSpec((32, GRID_W * GRID_W), lambda i: (0, 0))],
        out_specs=pl.BlockSpec((rows, GRID_W * GRID_W), lambda i: (0, 0)),
        out_shape=jax.ShapeDtypeStruct((rows, GRID_W * GRID_W), F32),
        compiler_params=_cp("arbitrary"),
        name="rpb_tables",
    )(r2, jnp.asarray(onehot, BF16))
    toe = toe.reshape(depth, C_HEADS, nr, GRID_W, GRID_W)
    rows = SEQ // GRID_W
    dr = np.zeros((3, NA_QROWS, NA_UNION), np.int32)
    ok_row = np.zeros((3, NA_QROWS, NA_UNION), bool)
    for ty, g in enumerate((0, 1, rows // NA_QROWS - 1)):
        us = _na_union_start(g, rows)
        for i in range(NA_QROWS):
            r = NA_QROWS * g + i
            rs = min(max(r - NA_ROWS // 2, 0), rows - NA_ROWS)
            for j in range(NA_UNION):
                ok_row[ty, i, j] = rs <= us + j < rs + NA_ROWS
                dr[ty, i, j] = min(max(us + j - r + NA_ROWS - 1, 0), nr - 1)
    cstart = np.clip(q - NA_COLS // 2, 0, GRID_W - NA_COLS)
    ok_col = (q[None, :] >= cstart[:, None]) & (q[None, :] < cstart[:, None] + NA_COLS)
    ok = ok_row[:, :, :, None, None] & ok_col[None, None, None]
    tab = toe[:, :, jnp.asarray(dr)]
    tab = jnp.where(jnp.asarray(ok), tab, -jnp.inf)
    tab = jnp.transpose(tab, (0, 2, 1, 3, 5, 4, 6))
    return tab.reshape(depth, 3, C_HEADS, NA_QROWS * GRID_W, NA_UNION * GRID_W)


NA_QROWS = 4
NA_UNION = NA_QROWS + NA_ROWS - 1


def _na_union_start(g, rows):
    lo = NA_QROWS * g - NA_ROWS // 2
    hi = rows - NA_UNION
    if isinstance(g, int):
        return min(max(lo, 0), hi)
    return jnp.clip(lo, 0, hi)


def _na_body(q_ref, k_ref, v_ref, bias_ref, qn_ref, kn_ref, seg_ref, o_ref, kh_ref, vh_ref):
    step = pl.program_id(1)
    seg = seg_ref[...]

    @pl.when(step == 0)
    def _():
        kh_ref[...] = _head_rms(k_ref[...], kn_ref[...], seg).astype(BF16)
        vh_ref[...] = v_ref[...].astype(BF16)

    nq = NA_QROWS * GRID_W
    q = _head_rms(q_ref[...], qn_ref[...], seg) * (HEAD ** -0.5 * math.log2(math.e))
    lane = _iota((nq, 128), 1)
    low = lane < HEAD
    sls = [slice((h // 2) * 128, (h // 2 + 1) * 128) for h in range(C_HEADS)]

    def masked_q(h):
        return jnp.where(low if h % 2 == 0 else jnp.logical_not(low), q[:, sls[h]], 0.0).astype(BF16)

    def store(hs, outs):
        for j in range(0, len(hs), 2):
            o_ref[:, sls[hs[j]]] = jnp.where(low, outs[j], outs[j + 1]).astype(o_ref.dtype)

    head_groups = [list(range(g, g + 4)) for g in range(0, C_HEADS, 4)]

    @pl.when(step == 0)
    def _():
        for hs in head_groups:
            qs = [masked_q(h) for h in hs]
            s = [_dot_nt(qs[j], kh_ref[0:CTX, sls[h]]) for j, h in enumerate(hs)]
            e = [jnp.exp2(x - jnp.max(x, axis=-1, keepdims=True)) for x in s]
            store(hs, [_dot(e[j].astype(BF16), vh_ref[0:CTX, sls[h]]) / jnp.sum(e[j], axis=-1, keepdims=True)
                       for j, h in enumerate(hs)])

    @pl.when(step > 0)
    def _():
        us = _na_union_start(step - 1, SEQ // GRID_W)
        k0 = pl.multiple_of(CTX + us * GRID_W, GRID_W)
        nloc = NA_UNION * GRID_W
        for hs in head_groups:
            qs = [masked_q(h) for h in hs]
            s_loc = [_dot_nt(qs[j], kh_ref[pl.ds(k0, nloc), sls[h]]) + bias_ref[0, h] for j, h in enumerate(hs)]
            s_ctx = [_dot_nt(qs[j], kh_ref[0:CTX, sls[h]]) for j, h in enumerate(hs)]
            mx = [jnp.maximum(jnp.max(a, axis=-1, keepdims=True), jnp.max(b, axis=-1, keepdims=True))
                  for a, b in zip(s_loc, s_ctx)]
            e_loc = [jnp.exp2(a - m_) for a, m_ in zip(s_loc, mx)]
            e_ctx = [jnp.exp2(b - m_) for b, m_ in zip(s_ctx, mx)]
            den = [jnp.sum(a, axis=-1, keepdims=True) + jnp.sum(b, axis=-1, keepdims=True)
                   for a, b in zip(e_loc, e_ctx)]
            store(hs, [(_dot(e_loc[j].astype(BF16), vh_ref[pl.ds(k0, nloc), sls[h]])
                        + _dot(e_ctx[j].astype(BF16), vh_ref[0:CTX, sls[h]])) / den[j] for j, h in enumerate(hs)])


def na_attention(z, bias, qn, kn, layer):
    t = z.shape[0]
    nb = t // ROWS
    nq = NA_QROWS * GRID_W
    assert nq == CTX
    steps = ROWS // nq
    groups = steps - 1

    def layout(i):
        g = jnp.maximum(i - 1, 0)
        return jnp.where(g == 0, 0, jnp.where(g == groups - 1, 2, 1))

    full = lambda shape: pl.BlockSpec(shape, lambda b, i: (0,) * len(shape))
    return pl.pallas_call(
        _na_body,
        grid=(nb, steps),
        in_specs=[pl.BlockSpec((nq, BLK), lambda b, i: (b * steps + i, 7)),
                  pl.BlockSpec((ROWS, BLK), lambda b, i: (b, 8)),
                  pl.BlockSpec((ROWS, BLK), lambda b, i: (b, 9)),
                  pl.BlockSpec((None, 1, C_HEADS, nq, NA_UNION * GRID_W), lambda b, i: (layer, layout(i), 0, 0, 0)),
                  full((1, BLK)), full((1, BLK)), full((BLK, BLK))],
        out_specs=pl.BlockSpec((nq, BLK), lambda b, i: (b * steps + i, 0)),
        out_shape=jax.ShapeDtypeStruct((t, BLK), BF16),
        scratch_shapes=[pltpu.VMEM((ROWS, BLK), BF16), pltpu.VMEM((ROWS, BLK), BF16)],
        compiler_params=_cp("arbitrary", "arbitrary"),
        name="na_attention",
    )(z, z, z, bias, qn, kn, _seg_ones(512, HEAD))


def _lru_body(x_ref, xp_ref, xn_ref, gate_ref, cw_ref, cb_ref, wa_ref, ba_ref, wx_ref, bx_ref, lam_ref,
              *rest, reverse, last, tpb):
    if last:
        h0_ref, o_ref, carry_ref = rest
    else:
        o_ref, carry_ref = rest
    p = pl.program_id(0) % tpb
    has_prev, has_next = _tile_flags(_lru_tile(p, tpb, reverse), tpb)

    @pl.when(p == 0)
    def _():
        carry_ref[...] = jnp.zeros_like(carry_ref)

    x = x_ref[...]
    n = x.shape[0]
    xm1 = _shift_rows(x, xp_ref[7:8, :], 1, has_prev)
    xp1 = _shift_rows(x, xn_ref[0:1, :], -1, has_next)
    rows = _iota((n, 1), 0)
    nx = jnp.where(has_next, xn_ref[0:2, :], 0.0)
    xp2 = pltpu.roll(x, n - 2, 0)
    xp2 = jnp.where(rows == n - 2, nx[0:1, :], jnp.where(rows == n - 1, nx[1:2, :], xp2))
    xr = (cw_ref[0:1, :] * xm1 + cw_ref[1:2, :] * x + cw_ref[2:3, :] * xp1 + cw_ref[3:4, :] * xp2) + cb_ref[...]
    xb = xr.astype(BF16)
    rg = _sigmoid(_dot(xb, wa_ref[...]) + ba_ref[...])
    ig = _sigmoid(_dot(xb, wx_ref[...]) + bx_ref[...])
    log_a = -RG_C * _softplus(-lam_ref[...]) * rg
    a = jnp.exp(log_a)
    b = jnp.sqrt(jnp.maximum(1.0 - jnp.exp(2.0 * log_a), 0.0)) * ig * xr
    k = 1
    while k < n:
        if reverse:
            keep = rows < n - k
            a_s = pltpu.roll(a, n - k, 0)
            b_s = pltpu.roll(b, n - k, 0)
        else:
            keep = rows >= k
            a_s = pltpu.roll(a, k, 0)
            b_s = pltpu.roll(b, k, 0)
        b = jnp.where(keep, a * b_s + b, b)
        a = jnp.where(keep, a * a_s, a)
        k *= 2
    h = a * carry_ref[...] + b
    carry_ref[...] = h[0:1, :] if reverse else h[n - 1:n, :]
    if last:
        g = gate_ref[...]
        gelu = 0.5 * g * (1.0 + jnp.tanh(math.sqrt(2.0 / math.pi) * (g + 0.044715 * g * g * g)))
        o_ref[...] = ((h0_ref[...] + h) * gelu).astype(o_ref.dtype)
    else:
        o_ref[...] = h


def _lru_tile(p, tpb, reverse):
    cctx = CTX // TQ
    if reverse:
        return jnp.where(p < cctx, cctx - 1 - p, tpb - 1 + cctx - p)
    return p


def rglru(z, direction, conv_w, conv_b, wa_bd, ba, wx_bd, bx, lam, h0=None):
    t = z.shape[0]
    n_tiles = t // TQ
    tpb = ROWS // TQ
    reverse = direction == 1
    last = h0 is not None

    def blk(i):
        return (i // tpb) * tpb + _lru_tile(i % tpb, tpb, reverse)

    r8 = TQ // 8
    full = lambda shape: pl.BlockSpec(shape, lambda i: (0,) * len(shape))
    in_specs = [pl.BlockSpec((TQ, BLK), lambda i: (blk(i), 10)),
                pl.BlockSpec((8, BLK), lambda i: (jnp.maximum(blk(i) * r8 - 1, 0), 10)),
                pl.BlockSpec((8, BLK), lambda i: (jnp.minimum((blk(i) + 1) * r8, n_tiles * r8 - 1), 10)),
                pl.BlockSpec((TQ, BLK), lambda i: (blk(i), 11)),
                full((D_CONV, BLK)), full((1, BLK)), full((BLK, BLK)), full((1, BLK)),
                full((BLK, BLK)), full((1, BLK)), full((1, BLK))]
    args = [z, z, z, z, conv_w, conv_b, wa_bd, ba, wx_bd, bx, lam]
    if last:
        in_specs.append(pl.BlockSpec((TQ, BLK), lambda i: (blk(i), 0)))
        args.append(h0)

    return pl.pallas_call(
        functools.partial(_lru_body, reverse=reverse, last=last, tpb=tpb),
        grid=(n_tiles,),
        in_specs=in_specs,
        out_specs=pl.BlockSpec((TQ, BLK), lambda i: (blk(i), 0)),
        out_shape=jax.ShapeDtypeStruct((t, BLK), BF16 if last else F32),
        scratch_shapes=[pltpu.VMEM((1, BLK), F32)],
        compiler_params=_cp("arbitrary"),
        name="rglru_%d" % direction,
    )(*args)


def _merge_body(hn_ref, wg_ref, ba_ref, bb_ref, bc_ref, bd_ref, wb_ref, y_ref, *, tn):
    g = _dot(hn_ref[...], wg_ref[...])
    acc = None
    for j, br in enumerate((ba_ref, bb_ref, bc_ref, bd_ref)):
        term = _sigmoid(g[:, j * tn:(j + 1) * tn]) * _dot(br[...].astype(BF16), wb_ref[j])
        acc = term if acc is None else acc + term
    y_ref[...] = acc.astype(BF16)


def merge_branches(hn, wg, branches, wb, layer, *, tm=1152, tn=256):
    t, d = hn.shape
    br_spec = pl.BlockSpec((tm, BLK), lambda i, j: (i, 0))
    return pl.pallas_call(
        functools.partial(_merge_body, tn=tn),
        grid=(t // tm, d // tn),
        in_specs=[pl.BlockSpec((tm, d), lambda i, j: (i, 0)),
                  pl.BlockSpec((None, None, d, N_BRANCH * tn), lambda i, j: (layer, j, 0, 0)),
                  br_spec, br_spec, br_spec, br_spec,
                  pl.BlockSpec((None, N_BRANCH, BLK, tn), lambda i, j: (layer, 0, 0, j))],
        out_specs=pl.BlockSpec((tm, tn), lambda i, j: (i, j)),
        out_shape=jax.ShapeDtypeStruct((t, d), BF16),
        compiler_params=_cp("arbitrary", "arbitrary"),
        name="merge_branches",
    )(hn, wg, *branches, wb)


def _post_body(y_ref, wo_ref, x_ref, mod_ref, g2_ref, rw_ref, rb_ref,
               xo_ref, h2_ref, e_ref, w_ref, rank_ref, cnt_ref, base_ref, *, tm, tpb, slab):
    i = pl.program_id(0)
    row0 = (i % tpb) * tm

    @pl.when(i == 0)
    def _():
        base_ref[...] = jnp.zeros_like(base_ref)

    slabs = [slice(r, r + slab) for r in range(0, tm, slab)]
    attn = [_dot(y_ref[sl, :], wo_ref[...]) for sl in slabs]
    logit_parts = []
    for sl, a in zip(slabs, attn):
        x = x_ref[sl, :] + _mod_rows(mod_ref, 2, row0 + sl.start, slab) * a
        xo_ref[sl, :] = x
        h = _norm_mod(x, g2_ref[...], mod_ref, 3, row0 + sl.start, slab)
        hh, hl = _split(h)
        half = h.shape[1] // 2
        lo = pltpu.bitcast(hh[:, :half].astype(F32), jnp.uint32) >> 16
        hi = pltpu.bitcast(hh[:, half:].astype(F32), jnp.uint32) & jnp.uint32(0xFFFF0000)
        h2_ref[sl, :] = lo | hi
        logit_parts.append(_dot_nt(rw_ref[0], hh) + _dot_nt(rw_ref[0], hl) + _dot_nt(rw_ref[1], hh))
    logits = jnp.concatenate(logit_parts, axis=1)
    s = _sigmoid(logits)
    sb = s + rb_ref[...]
    per = N_EXPERTS // N_GROUPS
    srow = [s[e:e + 1, :] for e in range(N_EXPERTS)]
    brow = [sb[e:e + 1, :] for e in range(N_EXPERTS)]
    gscore = []
    for g in range(N_GROUPS):
        best = None
        for a in range(per):
            for b in range(a + 1, per):
                pair = brow[per * g + a] + brow[per * g + b]
                best = pair if best is None else jnp.maximum(best, pair)
        gscore.append(best)
    gbest = jnp.zeros((1, tm), jnp.int32)
    bscore = gscore[0]
    for g in range(1, N_GROUPS):
        better = gscore[g] > bscore
        gbest = jnp.where(better, g, gbest)
        bscore = jnp.where(better, gscore[g], bscore)

    def pick(rows, i):
        out = rows[i]
        for g in range(1, N_GROUPS):
            out = jnp.where(gbest == g, rows[per * g + i], out)
        return out

    cand = [pick(brow, i) for i in range(per)]
    cval = [pick(srow, i) for i in range(per)]
    i1 = jnp.zeros((1, tm), jnp.int32)
    m1 = cand[0]
    w1 = cval[0]
    for i_ in range(1, per):
        better = cand[i_] > m1
        i1 = jnp.where(better, i_, i1)
        m1 = jnp.where(better, cand[i_], m1)
        w1 = jnp.where(better, cval[i_], w1)
    i2 = jnp.full((1, tm), -1, jnp.int32)
    m2 = jnp.full((1, tm), -jnp.inf, F32)
    w2 = jnp.zeros((1, tm), F32)
    for i_ in range(per):
        better = jnp.logical_and(i1 != i_, cand[i_] > m2)
        i2 = jnp.where(better, i_, i2)
        m2 = jnp.where(better, cand[i_], m2)
        w2 = jnp.where(better, cval[i_], w2)
    e1 = per * gbest + i1
    e2 = per * gbest + i2
    wsum = w1 + w2
    e_ref[0:1, :] = e1
    e_ref[1:2, :] = e2
    w_ref[0:1, :] = w1 / wsum
    w_ref[1:2, :] = w2 / wsum
    er = _iota((N_EXPERTS, tm), 0)
    oh1 = jnp.where(er == e1, 1.0, 0.0)
    oh2 = jnp.where(er == e2, 1.0, 0.0)
    upper = jnp.where(_iota((tm, tm), 0) < _iota((tm, tm), 1), 1.0, 0.0).astype(BF16)
    ex1 = _dot(oh1.astype(BF16), upper)
    ex2 = _dot(oh2.astype(BF16), upper)
    tot1 = jnp.sum(oh1, axis=1, keepdims=True)
    tot2 = jnp.sum(oh2, axis=1, keepdims=True)
    base = base_ref[...]
    rank_ref[0:1, :] = jnp.sum(oh1 * (base + ex1), axis=0, keepdims=True).astype(jnp.int32)
    rank_ref[1:2, :] = jnp.sum(oh2 * (base + tot1 + ex2), axis=0, keepdims=True).astype(jnp.int32)
    base = base + tot1 + tot2
    base_ref[...] = base
    cnt_ref[...] = jnp.broadcast_to(base, cnt_ref.shape)


def post_attention(y, w_out, x, mod, g2, rw, rb, layer, *, tm=384):
    t, d = x.shape
    tpb = ROWS // tm
    full = lambda shape: pl.BlockSpec(shape, lambda i: (0,) * len(shape))
    row = pl.BlockSpec((tm, d), lambda i: (i, 0))
    tok = pl.BlockSpec((2, tm), lambda i: (0, i))
    return pl.pallas_call(
        functools.partial(_post_body, tm=tm, tpb=tpb, slab=128),
        grid=(t // tm,),
        in_specs=[row, pl.BlockSpec((None, d, d), lambda i: (layer, 0, 0)), row,
                  pl.BlockSpec((1, 2, 6, d), lambda i: (i // tpb, 0, 0, 0)),
                  full((1, d)), full((2, N_EXPERTS, d)), full((N_EXPERTS, 1))],
        out_specs=[row, pl.BlockSpec((tm, d // 2), lambda i: (i, 0)), tok, tok, tok, full((N_EXPERTS, 128))],
        out_shape=[jax.ShapeDtypeStruct((t, d), F32), jax.ShapeDtypeStruct((t, d // 2), jnp.uint32),
                   jax.ShapeDtypeStruct((2, t), jnp.int32), jax.ShapeDtypeStruct((2, t), F32),
                   jax.ShapeDtypeStruct((2, t), jnp.int32), jax.ShapeDtypeStruct((N_EXPERTS, 128), F32)],
        scratch_shapes=[pltpu.VMEM((N_EXPERTS, 1), F32)],
        compiler_params=_cp("arbitrary"),
        name="post_attention",
    )(y, w_out, x, mod, g2, rw, rb)


def _row_copy(src, dst, sem):
    return pltpu.make_async_copy(src, dst, sem)


def _dispatch_body(pos_ref, h_ref, xs_in_ref, xs_ref, sem, *, tm):
    del xs_in_ref

    def start(t, carry):
        for k in range(2):
            p = pos_ref[0, 0, k * tm + t]
            _row_copy(h_ref.at[pl.ds(t, 1)], xs_ref.at[pl.ds(p, 1)], sem).start(priority=k)
        return carry

    lax.fori_loop(0, tm, start, 0, unroll=8)
    for k in range(2):
        _row_copy(h_ref, xs_ref.at[pl.ds(0, tm)], sem).wait()


def moe_dispatch(pos3, h2, n_rows, *, tm=256):
    t, d = h2.shape
    xs0 = jnp.zeros((n_rows, d), h2.dtype)
    return pl.pallas_call(
        functools.partial(_dispatch_body, tm=tm),
        grid=(t // tm,),
        in_specs=[pl.BlockSpec((1, 1, 2 * tm), lambda i: (i, 0, 0), memory_space=pltpu.SMEM),
                  pl.BlockSpec((tm, d), lambda i: (i, 0)),
                  pl.BlockSpec(memory_space=pl.ANY)],
        out_specs=pl.BlockSpec(memory_space=pl.ANY),
        out_shape=jax.ShapeDtypeStruct((n_rows, d), h2.dtype),
        scratch_shapes=[pltpu.SemaphoreType.DMA(())],
        input_output_aliases={2: 0},
        compiler_params=_cp("arbitrary"),
        name="moe_dispatch",
    )(pos3, h2, xs0)


def _ffn_body(te_ref, nv_ref, x_ref, wg_ref, wu_ref, wd_ref, y_ref):
    del te_ref
    i = pl.program_id(0)

    @pl.when(i < nv_ref[0])
    def _():
        p = x_ref[...]
        half = p.shape[1]
        x_lo = pltpu.bitcast(p << 16, F32).astype(BF16)
        x_hi = pltpu.bitcast(p & jnp.uint32(0xFFFF0000), F32).astype(BF16)
        g = _dot(x_lo, wg_ref[0:half, :]) + _dot(x_hi, wg_ref[half:2 * half, :])
        u = _dot(x_lo, wu_ref[0:half, :]) + _dot(x_hi, wu_ref[half:2 * half, :])
        y_ref[...] = _dot((g * _sigmoid(g) * u).astype(BF16), wd_ref[...])

    @pl.when(i >= nv_ref[0])
    def _():
        y_ref[...] = jnp.zeros_like(y_ref)


def moe_ffn(tile_expert, n_valid, xs, wg, wu, wd, layer):
    n_rows, dp = xs.shape
    d, ff = wg.shape[-2:]
    n_tiles = n_rows // FFN_TM
    grid_spec = pltpu.PrefetchScalarGridSpec(
        num_scalar_prefetch=2,
        grid=(n_tiles,),
        in_specs=[pl.BlockSpec((FFN_TM, dp), lambda i, te, nv: (jnp.minimum(i, nv[0] - 1), 0)),
                  pl.BlockSpec((None, None, d, ff), lambda i, te, nv: (layer, te[i], 0, 0)),
                  pl.BlockSpec((None, None, d, ff), lambda i, te, nv: (layer, te[i], 0, 0)),
                  pl.BlockSpec((None, None, ff, d), lambda i, te, nv: (layer, te[i], 0, 0))],
        out_specs=pl.BlockSpec((FFN_TM, d), lambda i, te, nv: (i, 0)))
    return pl.pallas_call(
        _ffn_body,
        grid_spec=grid_spec,
        out_shape=jax.ShapeDtypeStruct((n_rows, d), F32),
        compiler_params=_cp("arbitrary"),
        name="moe_ffn",
    )(tile_expert, n_valid, xs, wg, wu, wd)


def _combine_body(pos_ref, ys_ref, x_ref, w_ref, mod_ref, o_ref, buf_ref, sem, *, tm, tpb):
    def start(t, carry):
        for k in range(2):
            p = pos_ref[0, 0, k * tm + t]
            _row_copy(ys_ref.at[pl.ds(p, 1)], buf_ref.at[k, pl.ds(t, 1)], sem).start(priority=k)
        return carry

    lax.fori_loop(0, tm, start, 0, unroll=8)
    for k in range(2):
        _row_copy(ys_ref.at[pl.ds(0, tm)], buf_ref.at[k], sem).wait()
    w = w_ref[...]
    y = w[:, 0:1] * buf_ref[0] + w[:, 1:2] * buf_ref[1]
    row0 = (pl.program_id(0) % tpb) * tm
    o_ref[...] = x_ref[...] + _mod_rows(mod_ref, 5, row0, tm) * y


def moe_combine(pos3, ys, x, wt, mod, *, tm=256):
    t, d = x.shape
    tpb = ROWS // tm
    row = pl.BlockSpec((tm, d), lambda i: (i, 0))
    return pl.pallas_call(
        functools.partial(_combine_body, tm=tm, tpb=tpb),
        grid=(t // tm,),
        in_specs=[pl.BlockSpec((1, 1, 2 * tm), lambda i: (i, 0, 0), memory_space=pltpu.SMEM),
                  pl.BlockSpec(memory_space=pl.ANY), row,
                  pl.BlockSpec((tm, 2), lambda i: (i, 0)),
                  pl.BlockSpec((1, 2, 6, d), lambda i: (i // tpb, 0, 0, 0))],
        out_specs=row,
        out_shape=jax.ShapeDtypeStruct((t, d), F32),
        scratch_shapes=[pltpu.VMEM((2, tm, d), F32), pltpu.SemaphoreType.DMA(())],
        compiler_params=_cp("arbitrary"),
        name="moe_combine",
    )(pos3, ys, x, wt, mod)


def moe_layout(e, rank, counts, *, tm=256):
    t = e.shape[1]
    n_rows_max = -(-(2 * t + N_EXPERTS * (FFN_TM - 1)) // FFN_TM) * FFN_TM
    n_tiles = n_rows_max // FFN_TM
    cnt = counts[:, 0].astype(jnp.int32)
    gsz = ((cnt + FFN_TM - 1) // FFN_TM) * FFN_TM
    end = jnp.cumsum(gsz)
    off = end - gsz
    onehot = e[:, :, None] == jnp.arange(N_EXPERTS, dtype=jnp.int32)
    pos = rank + jnp.sum(jnp.where(onehot, off, 0), axis=-1)
    pos3 = pos.reshape(2, t // tm, tm).transpose(1, 0, 2).reshape(t // tm, 1, 2 * tm)
    n_valid = (end[-1] // FFN_TM).astype(jnp.int32)
    starts = jnp.minimum(jnp.arange(n_tiles, dtype=jnp.int32), n_valid - 1) * FFN_TM
    tile_expert = jnp.sum(starts[:, None] >= end[None, :], axis=1).astype(jnp.int32)
    return pos3, tile_expert, n_valid.reshape(1), n_rows_max


def _block_diag(w):
    n, bi, bj = w.shape[-3:]
    eye = jnp.eye(n, dtype=w.dtype)
    out = w[..., :, :, None, :] * eye[:, None, :, None]
    return out.reshape(w.shape[:-3] + (n * bi, n * bj))


def kernel(x, c, ctx, c_ctx, w_mod, b_mod, norm1_g, norm2_g, w_in, a_mu, a_w0, a_w2, a_a0, a_a2, a_g2, a_kk, a_ka,
           a_rk, a_gn_g, a_gn_b, b_qn, b_kn, b_lam, b_subln, c_qn, c_kn, c_rpb, d_conv_w, d_conv_b, d_wa, d_ba,
           d_wx, d_bx, d_lam, w_branch, w_out, router_w, router_b, e_gate, e_up, e_down):
    nb = x.shape[0]
    d = D_MODEL
    t = nb * ROWS
    xs = jnp.concatenate([ctx, x], axis=1).reshape(t, d)

    cc = jnp.concatenate([c, c_ctx[None], jnp.zeros((-(nb + 1) % 8, d), F32)], axis=0)
    mods = mod_vectors(cc, w_mod, b_mod)
    mod_l = mods[:, :nb].reshape(DEPTH, nb, 1, 6, d)
    mod_c = jnp.broadcast_to(mods[:, nb].reshape(DEPTH, 1, 1, 6, d), (DEPTH, nb, 1, 6, d))
    mod = jnp.concatenate([mod_l, mod_c], axis=2)

    o_b, o_c, o_d, o_g = A_COLS, A_COLS + 1536, A_COLS + 3072, A_COLS + 3072 + 1024
    w_mix = jnp.concatenate([w_in[:, :, :o_b], jnp.zeros((DEPTH, d, 2048 - A_COLS), F32), w_in[:, :, o_b:o_g]],
                            axis=-1).astype(BF16)
    tn_m = 256
    w_gate = w_in[:, :, o_g:].astype(BF16).reshape(DEPTH, d, N_BRANCH, d // tn_m, tn_m)
    w_gate = jnp.transpose(w_gate, (0, 3, 1, 2, 4)).reshape(DEPTH, d // tn_m, d, N_BRANCH * tn_m)
    mu = jnp.pad(a_mu, ((0, 0), (0, 0), (0, 2048 - A_COLS)))
    w2bd = _block_diag(a_w2).astype(BF16)
    a2bd = _block_diag(a_a2).astype(BF16)
    g2 = a_g2.astype(BF16)
    wa_bd = _block_diag(d_wa).astype(BF16)
    wx_bd = _block_diag(d_wx).astype(BF16)
    wb = w_branch.astype(BF16)
    wo = w_out.astype(BF16)
    rw = jnp.stack(_split(router_w.T), axis=0)
    rb = router_b.reshape(N_EXPERTS, 1)
    eg, eu, ed = e_gate.astype(BF16), e_up.astype(BF16), e_down.astype(BF16)
    cos, sin = rope_tables()
    bias = rpb_tables(c_rpb)
    qn_c = jnp.tile(c_qn, (1, C_HEADS)).reshape(DEPTH, 1, BLK)
    kn_c = jnp.tile(c_kn, (1, C_HEADS)).reshape(DEPTH, 1, BLK)

    for l in range(DEPTH):
        lam_init = 0.8 - 0.6 * math.exp(-0.3 * l)
        z, hn = in_projection(xs, mod[l], norm1_g[l].reshape(1, d), w_mix, l)
        prep = rwkv_prepare(z, mu[l], w2bd[l], a2bd[l], g2[l], a_w0[l].reshape(1, 1024), a_a0[l].reshape(1, 1024),
                            a_kk[l].reshape(1, BLK), a_ka[l].reshape(1, BLK), a_rk[l].reshape(1, BLK))
        ya = rwkv_scan(prep, 0)
        oa = rwkv_scan(prep, 1, ya, a_gn_g[l].reshape(1, BLK), a_gn_b[l].reshape(1, BLK))
        ob = diff_attention(z, cos, sin, b_qn[l].reshape(1, 128), b_kn[l].reshape(1, 128), b_lam[l],
                            b_subln[l].reshape(1, 128), lam_init)
        oc = na_attention(z, bias, qn_c[l], kn_c[l], l)
        lru = [(d_conv_w[l], d_conv_b[l].reshape(1, BLK), wa_bd[l, k], d_ba[l, k].reshape(1, BLK), wx_bd[l, k],
                d_bx[l, k].reshape(1, BLK), d_lam[l, k].reshape(1, BLK)) for k in range(2)]
        hd = rglru(z, 0, *lru[0])
        od = rglru(z, 1, *lru[1], h0=hd)
        y = merge_branches(hn, w_gate, (oa, ob, oc, od), wb, l, tn=tn_m)
        xs, h2, e, wts, rank, counts = post_attention(y, wo, xs, mod[l], norm2_g[l].reshape(1, d), rw, rb, l)
        pos3, tile_expert, n_valid, n_rows = moe_layout(e, rank, counts)
        xg = moe_dispatch(pos3, h2, n_rows)
        yg = moe_ffn(tile_expert, n_valid, xg, eg, eu, ed, l)
        xs = moe_combine(pos3, yg, xs, wts.T, mod[l])
    return xs.reshape(nb, ROWS, d)[:, CTX:]
```

```python
import functools
import math

import numpy as np
import jax
import jax.numpy as jnp
from jax import lax
from jax.experimental import pallas as pl
from jax.experimental.pallas import tpu as pltpu

F32 = jnp.float32
BF16 = jnp.bfloat16

D_MODEL = 2048
DEPTH = 4
GRID_W = 64
CTX = 256
SEQ = 2048
ROWS = CTX + SEQ
HEAD = 64
A_WIDTH = 512
A_COLS = 1920
A_GN_EPS = 64e-5
B_HEADS = 4
C_HEADS = 8
NA_ROWS = 8
NA_COLS = 16
D_WIDTH = 512
D_CONV = 4
RG_C = 8.0
N_BRANCH = 4
N_EXPERTS = 16
N_GROUPS = 4
D_FF = 1024
ROPE_BASE = 10000.0
NORM_EPS = 1e-6
Z_COLS = 6144
BLK = 512

VMEM_LIMIT = 56 * 1024 * 1024

CHUNK = 64
TQ = 256
FFN_TM = 512


def _cp(*sem):
    return pltpu.CompilerParams(dimension_semantics=sem, vmem_limit_bytes=VMEM_LIMIT)


def _dot(a, b):
    return jnp.dot(a, b, preferred_element_type=F32)


def _dot_nt(a, b):
    return lax.dot_general(a, b, (((1,), (1,)), ((), ())), preferred_element_type=F32)


def _dot_tn(a, b):
    return lax.dot_general(a, b, (((0,), (0,)), ((), ())), preferred_element_type=F32)


def _split(x):
    hi = x.astype(BF16)
    lo = (x - hi.astype(F32)).astype(BF16)
    return hi, lo


def _dot_x(x, m):
    hi, lo = _split(x)
    return _dot(hi, m) + _dot(lo, m)


def _sigmoid(x):
    return 1.0 / (1.0 + jnp.exp(-x))


def _softplus(x):
    return jnp.maximum(x, 0.0) + jnp.log(1.0 + jnp.exp(-jnp.abs(x)))


def _iota(shape, dim):
    return lax.broadcasted_iota(jnp.int32, shape, dim)


def _seg_ones(width, seg):
    r = np.arange(width)
    return jnp.asarray((r[:, None] // seg) == (r[None, :] // seg), dtype=BF16)


def _mod_body(c_ref, w_ref, b_ref, o_ref):
    c = c_ref[...]
    s = (c * _sigmoid(c)).astype(BF16)
    o_ref[...] = _dot(s, w_ref[...].astype(BF16)) + b_ref[...]


def mod_vectors(cc, w_mod, b_mod):
    depth, d, n = w_mod.shape
    r = cc.shape[0]
    tn = 1536
    return pl.pallas_call(
        _mod_body,
        grid=(depth, n // tn),
        in_specs=[pl.BlockSpec((r, d), lambda l, j: (0, 0)),
                  pl.BlockSpec((None, d, tn), lambda l, j: (l, 0, j)),
                  pl.BlockSpec((None, 1, tn), lambda l, j: (l, 0, j))],
        out_specs=pl.BlockSpec((None, r, tn), lambda l, j: (l, 0, j)),
        out_shape=jax.ShapeDtypeStruct((depth, r, n), F32),
        compiler_params=_cp("arbitrary", "arbitrary"),
        name="mod_vectors",
    )(cc, w_mod, b_mod.reshape(depth, 1, n))


def _mod_rows(mod_ref, idx, row0, tm):
    rows = row0 + _iota((tm, 1), 0)
    return jnp.where(rows < CTX, mod_ref[0, 1, idx:idx + 1, :], mod_ref[0, 0, idx:idx + 1, :])


def _norm_mod(x, g, mod_ref, shift_idx, row0, tm):
    ms = jnp.mean(x * x, axis=-1, keepdims=True)
    y = x * lax.rsqrt(ms + NORM_EPS) * g
    return y * (1.0 + _mod_rows(mod_ref, shift_idx + 1, row0, tm)) + _mod_rows(mod_ref, shift_idx, row0, tm)


def _inproj_body(x_ref, mod_ref, g_ref, w_ref, z_ref, hn_ref, *, tm, tpb, slab):
    i = pl.program_id(0)

    @pl.when(pl.program_id(1) == 0)
    def _():
        for r in range(0, tm, slab):
            row0 = (i % tpb) * tm + r
            hn = _norm_mod(x_ref[r:r + slab, :], g_ref[...], mod_ref, 0, row0, slab).astype(BF16)
            hn_ref[r:r + slab, :] = hn
            z_ref[r:r + slab, :] = _dot(hn, w_ref[...])

    @pl.when(pl.program_id(1) != 0)
    def _():
        z_ref[...] = _dot(hn_ref[...], w_ref[...])


def in_projection(x, mod, g, w, layer, *, tm=1152, tn=512):
    t, d = x.shape
    n = w.shape[2]
    tpb = ROWS // tm
    return pl.pallas_call(
        functools.partial(_inproj_body, tm=tm, tpb=tpb, slab=384),
        grid=(t // tm, n // tn),
        in_specs=[pl.BlockSpec((tm, d), lambda i, j: (i, 0)),
                  pl.BlockSpec((1, 2, 6, d), lambda i, j: (i // tpb, 0, 0, 0)),
                  pl.BlockSpec((1, d), lambda i, j: (0, 0)),
                  pl.BlockSpec((None, d, tn), lambda i, j: (layer, 0, j))],
        out_specs=[pl.BlockSpec((tm, tn), lambda i, j: (i, j)),
                   pl.BlockSpec((tm, d), lambda i, j: (i, 0))],
        out_shape=[jax.ShapeDtypeStruct((t, n), F32), jax.ShapeDtypeStruct((t, d), BF16)],
        compiler_params=_cp("arbitrary", "arbitrary"),
        name="in_projection",
    )(x, mod, g, w)


def _tile_flags(ti, tpb):
    ctx_tiles = CTX // TQ
    has_prev = jnp.logical_and(ti != 0, ti != ctx_tiles)
    has_next = jnp.logical_and(ti != ctx_tiles - 1, ti != tpb - 1)
    return has_prev, has_next


def _halo_specs(width, col, n_tiles):
    r8 = TQ // 8
    return [pl.BlockSpec((TQ, width), lambda i: (i, col)),
            pl.BlockSpec((8, width), lambda i: (jnp.maximum(i * r8 - 1, 0), col)),
            pl.BlockSpec((8, width), lambda i: (jnp.minimum((i + 1) * r8, n_tiles * r8 - 1), col))]


def _shift_rows(z, halo_row, k, use_halo):
    n = z.shape[0]
    rows = _iota((n, 1), 0)
    fill = jnp.where(use_halo, halo_row, 0.0)
    if k > 0:
        return jnp.where(rows < k, fill, pltpu.roll(z, k, 0))
    return jnp.where(rows >= n + k, fill, pltpu.roll(z, n + k, 0))


def _rwkv_prep_body(z_ref, zp_ref, zn_ref, mu_ref, w2_ref, a2_ref, g2_ref, w0_ref, a0_ref,
                    kk_ref, ka_ref, rk_ref, seg_ref, o_ref, *, tpb):
    has_prev, has_next = _tile_flags(pl.program_id(0) % tpb, tpb)
    z = z_ref[...]
    prev = _shift_rows(z, zp_ref[7:8, :], 1, has_prev)
    nxt = _shift_rows(z, zn_ref[0:1, :], -1, has_next)
    zs = z + mu_ref[0:1, :] * (prev - z) + mu_ref[1:2, :] * (nxt - z)
    r = zs[:, 0:512]
    k = zs[:, 512:1024]
    v = zs[:, 1024:1536]
    wl = zs[:, 1536:1664]
    al = zs[:, 1664:1792]
    gl = zs[:, 1792:1920]
    seg = seg_ref[...]
    wlog = -_softplus(-(w0_ref[...] + _dot(jnp.tanh(wl).astype(BF16), w2_ref[...]))) - 0.5
    lw = -jnp.exp(wlog)
    a = _sigmoid(a0_ref[...] + _dot(al.astype(BF16), a2_ref[...]))
    g = _dot(_sigmoid(gl).astype(BF16), g2_ref[...])
    kk = k * kk_ref[...]
    kk = kk / jnp.maximum(jnp.sqrt(_dot_x(kk * kk, seg)), 1e-12)
    bonus = _dot_x(r * k * rk_ref[...], seg) * v
    ka = ka_ref[...]
    o_ref[:, 0:512] = r
    o_ref[:, 512:1024] = v
    o_ref[:, 1024:1536] = kk
    o_ref[:, 1536:2048] = bonus
    o_ref[:, 2048:2560] = g
    for d in range(2):
        ad = a[:, d * 512:(d + 1) * 512]
        base = 2560 + d * 1536
        o_ref[:, base:base + 512] = lw[:, d * 512:(d + 1) * 512]
        o_ref[:, base + 512:base + 1024] = kk * ad
        o_ref[:, base + 1024:base + 1536] = k * (1.0 + (ad - 1.0) * ka)


A_PREP_COLS = 2560 + 2 * 1536


def rwkv_prepare(z, mu, w2bd, a2bd, g2, w0, a0, k_k, k_a, r_k):
    t = z.shape[0]
    n_tiles = t // TQ
    tpb = ROWS // TQ
    full = lambda shape: pl.BlockSpec(shape, lambda i: (0,) * len(shape))
    return pl.pallas_call(
        functools.partial(_rwkv_prep_body, tpb=tpb),
        grid=(n_tiles,),
        in_specs=_halo_specs(2048, 0, n_tiles) + [
            full((2, 2048)), full((128, 1024)), full((128, 1024)), full((128, 512)),
            full((1, 1024)), full((1, 1024)), full((1, 512)), full((1, 512)), full((1, 512)),
            full((512, 512))],
        out_specs=pl.BlockSpec((TQ, A_PREP_COLS), lambda i: (i, 0)),
        out_shape=jax.ShapeDtypeStruct((t, A_PREP_COLS), F32),
        compiler_params=_cp("arbitrary"),
        name="rwkv_prepare",
    )(z, z, z, mu, w2bd, a2bd, g2, w0, a0, k_k, k_a, r_k, _seg_ones(512, HEAD))


def _rwkv_scan_body(r_ref, v_ref, kk_ref, lw_ref, b_ref, ke_ref, *rest, reverse, last, nbb):
    if last:
        y0_ref, bonus_ref, g_ref, gng_ref, gnb_ref, seg_ref, o_ref, s_ref = rest
    else:
        o_ref, s_ref = rest
    c = CHUNK
    nh = A_WIDTH // HEAD

    @pl.when(pl.program_id(1) == 0)
    def _():
        s_ref[...] = jnp.zeros_like(s_ref)

    ti = _iota((c, c), 0)
    si = _iota((c, c), 1)
    before = (si > ti) if reverse else (si < ti)
    upto = jnp.logical_or(before, si == ti)
    tri = jnp.where(upto, 1.0, 0.0).astype(BF16)
    eye = jnp.where(si == ti, 1.0, 0.0)
    pair = (ti >> 1) == (si >> 1)
    off_masks = []
    lg = 1
    while (2 << lg) <= c:
        off_masks.append(jnp.logical_and((ti >> (lg + 1)) == (si >> (lg + 1)), (ti >> lg) != (si >> lg)))
        lg += 1
    ti2 = _iota((c, 2 * c), 0)
    si2 = _iota((c, 2 * c), 1) % c
    upto2 = (si2 >= ti2) if reverse else (si2 <= ti2)
    hi2 = _iota((c, 2 * c), 1) >= c
    before_hi = jnp.logical_and(hi2, (si2 > ti2) if reverse else (si2 < ti2))

    chains = [(bi, h) for bi in range(nbb) for h in range(nh)]
    cols = {}
    gammas = []
    for bi in range(nbb):
        lw = lw_ref[bi]
        lw_hi, lw_lo = _split(lw)
        cs = _dot(tri, lw_hi) + _dot(tri, lw_lo)
        tot = jnp.sum(lw, axis=0, keepdims=True)
        e_neg = jnp.exp(-cs)
        e_hat = jnp.exp(tot - cs)
        gammas.append(jnp.exp(tot))
        b = b_ref[bi]
        ke = ke_ref[bi]
        mats = dict(r=r_ref[bi] * jnp.exp(cs), a=-kk_ref[bi] * jnp.exp(cs - lw), b=b * e_neg, k=ke * e_neg,
                    bh=b * e_hat, kh=ke * e_hat, v=v_ref[bi])
        for name, m in mats.items():
            for h in range(nh):
                cols[name, bi, h] = m[:, h * HEAD:(h + 1) * HEAD].astype(BF16)

    def per_chain(fn):
        return [fn(i, bi, h) for i, (bi, h) in enumerate(chains)]

    prod = per_chain(lambda i, bi, h: _dot_nt(
        jnp.concatenate([cols['a', bi, h], cols['r', bi, h]], axis=0),
        jnp.concatenate([cols['b', bi, h], cols['k', bi, h]], axis=0)))
    lab = per_chain(lambda i, bi, h: jnp.where(before, prod[i][0:c, 0:c], 0.0))
    lak2 = per_chain(lambda i, bi, h: jnp.where(before_hi, prod[i][0:c, :], 0.0).astype(BF16))
    m = per_chain(lambda i, bi, h: jnp.where(upto2, prod[i][c:2 * c, :], 0.0).astype(BF16))
    tinv = per_chain(lambda i, bi, h: eye + jnp.where(pair, lab[i], 0.0))
    for off_mask in off_masks:
        tb = [t.astype(BF16) for t in tinv]
        x = per_chain(lambda i, bi, h: _dot(jnp.where(off_mask, lab[i], 0.0).astype(BF16), tb[i]))
        tinv = per_chain(lambda i, bi, h: tinv[i] + _dot(tb[i], x[i].astype(BF16)))
    vv = per_chain(lambda i, bi, h: jnp.concatenate([cols['v', bi, h], cols['v', bi, h]], axis=0))
    lv = per_chain(lambda i, bi, h: _dot(lak2[i], vv[i]))
    tb = [t.astype(BF16) for t in tinv]
    w1 = per_chain(lambda i, bi, h: _dot(tb[i], cols['a', bi, h]))
    u2 = per_chain(lambda i, bi, h: _dot(tb[i], lv[i].astype(BF16)))
    s0 = per_chain(lambda i, bi, h: s_ref[bi, h])
    s0b = [s.astype(BF16) for s in s0]
    ws = per_chain(lambda i, bi, h: _dot_nt(
        jnp.concatenate([w1[i].astype(BF16), cols['r', bi, h]], axis=0), s0b[i]))
    u = per_chain(lambda i, bi, h: ws[i][0:c] + u2[i])
    uv = per_chain(lambda i, bi, h: jnp.concatenate([u[i].astype(BF16), cols['v', bi, h]], axis=0))
    ys = per_chain(lambda i, bi, h: ws[i][c:2 * c] + _dot(m[i], uv[i]))
    for i, (bi, h) in enumerate(chains):
        bkh = jnp.concatenate([cols['bh', bi, h], cols['kh', bi, h]], axis=0)
        s_ref[bi, h] = s0[i] * gammas[bi][:, h * HEAD:(h + 1) * HEAD] + _dot_tn(uv[i], bkh)
    yb = [jnp.concatenate(ys[bi * nh:(bi + 1) * nh], axis=1) for bi in range(nbb)]
    if last:
        y = jnp.concatenate([yb[bi] + y0_ref[bi] for bi in range(nbb)], axis=0)
        seg = seg_ref[...]
        inv = 1.0 / HEAD
        mean = _dot_x(y, seg) * inv
        yc = y - mean
        var = _dot_x(yc * yc, seg) * inv
        yn = yc * lax.rsqrt(var + A_GN_EPS) * gng_ref[...] + gnb_ref[...]
        for bi in range(nbb):
            o_ref[bi] = ((yn[bi * c:(bi + 1) * c] + bonus_ref[bi]) * g_ref[bi]).astype(o_ref.dtype)
    else:
        for bi in range(nbb):
            o_ref[bi] = yb[bi]


def rwkv_scan(prep, direction, y0=None, gn_g=None, gn_b=None, *, nbb=4):
    t = prep.shape[0]
    nb = t // ROWS
    cpb = ROWS // CHUNK
    cctx = CTX // CHUNK
    reverse = direction == 1
    last = y0 is not None
    prep3 = prep.reshape(nb, ROWS, prep.shape[1])

    def blk(p):
        if reverse:
            return jnp.where(p < cctx, cctx - 1 - p, cpb - 1 + cctx - p)
        return p

    def col(cb):
        return pl.BlockSpec((nbb, CHUNK, BLK), lambda g, p: (g, blk(p), cb))

    base = 5 + 3 * direction
    in_specs = [col(0), col(1), col(2), col(base), col(base + 1), col(base + 2)]
    args = [prep3] * 6
    if last:
        in_specs += [col(0), col(3), col(4),
                     pl.BlockSpec((1, BLK), lambda g, p: (0, 0)), pl.BlockSpec((1, BLK), lambda g, p: (0, 0)),
                     pl.BlockSpec((BLK, BLK), lambda g, p: (0, 0))]
        args += [y0.reshape(nb, ROWS, BLK), prep3, prep3, gn_g, gn_b, _seg_ones(512, HEAD)]
    out = pl.pallas_call(
        functools.partial(_rwkv_scan_body, reverse=reverse, last=last, nbb=nbb),
        grid=(nb // nbb, cpb),
        in_specs=in_specs,
        out_specs=col(0),
        out_shape=jax.ShapeDtypeStruct((nb, ROWS, BLK), BF16 if last else F32),
        scratch_shapes=[pltpu.VMEM((nbb, A_WIDTH // HEAD, HEAD, HEAD), F32)],
        compiler_params=_cp("arbitrary", "arbitrary"),
        name="rwkv_scan_%d" % direction,
    )(*args)
    return out.reshape(t, BLK)


def _rope(x, cos, sin):
    lane = _iota(x.shape, 1)
    partner = jnp.where((lane % HEAD) < HEAD // 2, pltpu.roll(x, 128 - HEAD // 2, 1), pltpu.roll(x, HEAD // 2, 1))
    return x * cos + partner * sin


def _head_rms(x, gain, seg):
    ms = _dot_x(x * x, seg) * (1.0 / HEAD)
    return x * lax.rsqrt(ms + NORM_EPS) * gain


def _diff_body(q_ref, k_ref, v_ref, cos_ref, sin_ref, qn_ref, kn_ref, lam_ref, sub_ref, seg_ref,
               o_ref, kh_ref, vh_ref, *, lam_init, nq):
    iq = pl.program_id(2)
    seg = seg_ref[...]

    @pl.when(iq == 0)
    def _():
        kh = _rope(_head_rms(k_ref[...], kn_ref[...], seg), cos_ref[...], sin_ref[...])
        kh_ref[...] = kh.astype(BF16)
        vh_ref[:, 0:128] = v_ref[...].astype(BF16)
        vh_ref[:, 128:256] = jnp.ones((ROWS, 128), BF16)

    lv = lam_ref[...]
    lam = (jnp.exp(jnp.sum(lv[0:1] * lv[1:2], axis=1, keepdims=True))
           - jnp.exp(jnp.sum(lv[2:3] * lv[3:4], axis=1, keepdims=True)) + lam_init)
    r0 = pl.multiple_of(iq * TQ, TQ)
    q = _rope(_head_rms(q_ref[...], qn_ref[...], seg), cos_ref[pl.ds(r0, TQ), :], sin_ref[pl.ds(r0, TQ), :])
    q = q * (HEAD ** -0.5 * math.log2(math.e))
    lane = _iota(q.shape, 1)
    q1 = jnp.where(lane < HEAD, q, 0.0).astype(BF16)
    q2 = jnp.where(lane >= HEAD, q, 0.0).astype(BF16)

    def attend(kh, vh):
        s = [_dot_nt(qs, kh) for qs in (q1, q2)]
        e = [jnp.exp2(x - jnp.max(x, axis=-1, keepdims=True)) for x in s]
        pv = [_dot(x.astype(BF16), vh) for x in e]
        o = pv[0][:, 0:128] / pv[0][:, 128:256] - lam * (pv[1][:, 0:128] / pv[1][:, 128:256])
        ms = jnp.mean(o * o, axis=-1, keepdims=True)
        o_ref[...] = (o * lax.rsqrt(ms + NORM_EPS) * sub_ref[...] * (1.0 - lam_init)).astype(o_ref.dtype)

    @pl.when(iq == 0)
    def _():
        attend(kh_ref[0:CTX, :], vh_ref[0:CTX, :])

    @pl.when(iq != 0)
    def _():
        attend(kh_ref[...], vh_ref[...])


def diff_attention(z, cos, sin, qn, kn, lam_vecs, subln, lam_init):
    t = z.shape[0]
    nb = t // ROWS
    nq = ROWS // TQ
    cb = BLK // 128
    full = lambda shape: pl.BlockSpec(shape, lambda b, h, i: (0,) * len(shape))
    return pl.pallas_call(
        functools.partial(_diff_body, lam_init=lam_init, nq=nq),
        grid=(nb, B_HEADS, nq),
        in_specs=[pl.BlockSpec((TQ, 128), lambda b, h, i: (b * nq + i, 4 * cb + h)),
                  pl.BlockSpec((ROWS, 128), lambda b, h, i: (b, 5 * cb + h)),
                  pl.BlockSpec((ROWS, 128), lambda b, h, i: (b, 6 * cb + h)),
                  full((ROWS, 128)), full((ROWS, 128)), full((1, 128)), full((1, 128)),
                  full((4, HEAD)), full((1, 128)), full((128, 128))],
        out_specs=pl.BlockSpec((TQ, 128), lambda b, h, i: (b * nq + i, h)),
        out_shape=jax.ShapeDtypeStruct((t, BLK), BF16),
        scratch_shapes=[pltpu.VMEM((ROWS, 128), BF16), pltpu.VMEM((ROWS, 256), BF16)],
        compiler_params=_cp("arbitrary", "arbitrary", "arbitrary"),
        name="diff_attention",
    )(z, z, z, cos, sin, qn, kn, lam_vecs, subln, _seg_ones(128, HEAD))


def rope_tables():
    t = np.arange(SEQ)
    quarter = HEAD // 4
    inv = (1.0 / (ROPE_BASE ** (np.arange(quarter, dtype=np.float32) / quarter))).astype(np.float32)
    row = (t // GRID_W).astype(np.float32)[:, None] * inv
    col = (t % GRID_W).astype(np.float32)[:, None] * inv
    ang = np.concatenate([row, col], axis=-1)
    cos = np.concatenate([np.ones((CTX, HEAD // 2), np.float32), np.cos(ang)], axis=0)
    sin = np.concatenate([np.zeros((CTX, HEAD // 2), np.float32), np.sin(ang)], axis=0)
    cos = np.tile(cos, (1, 4))
    sin = np.tile(np.concatenate([-sin, sin], axis=1), (1, 2))
    return jnp.asarray(cos, F32), jnp.asarray(sin, F32)


def _rpb_body(r_ref, e_ref, o_ref):
    o_ref[...] = _dot_x(r_ref[...] * math.log2(math.e), e_ref[...])


def rpb_tables(rpb):
    depth = rpb.shape[0]
    nr, nc = 2 * NA_ROWS - 1, 2 * NA_COLS - 1
    q = np.arange(GRID_W)
    dc = np.clip(q[None, :] - q[:, None] + NA_COLS - 1, 0, nc - 1)
    onehot = (np.arange(32)[:, None, None] == dc[None]).astype(np.float32).reshape(32, GRID_W * GRID_W)
    r2 = jnp.pad(rpb.reshape(depth * C_HEADS * nr, nc), ((0, 0), (0, 32 - nc)))
    rows = r2.shape[0]
    toe = pl.pallas_call(
        _rpb_body,
        grid=(1,),
        in_specs=[pl.BlockSpec((rows, 32), lambda i: (0, 0)), pl.BlockSpec((32, GRID_W * GRID_W), lambda i: (0, 0))],
        out_specs=pl.BlockSpec((rows, GRID_W * GRID_W), lambda i: (0, 0)),
        out_shape=jax.ShapeDtypeStruct((rows, GRID_W * GRID_W), F32),
        compiler_params=_cp("arbitrary"),
        name="rpb_tables",
    )(r2, jnp.asarray(onehot, BF16))
    toe = toe.reshape(depth, C_HEADS, nr, GRID_W, GRID_W)
    rows = SEQ // GRID_W
    dr = np.zeros((3, NA_QROWS, NA_UNION), np.int32)
    ok_row = np.zeros((3, NA_QROWS, NA_UNION), bool)
    for ty, g in enumerate((0, 1, rows // NA_QROWS - 1)):
        us = _na_union_start(g, rows)
        for i in range(NA_QROWS):
            r = NA_QROWS * g + i
            rs = min(max(r - NA_ROWS // 2, 0), rows - NA_ROWS)
            for j in range(NA_UNION):
                ok_row[ty, i, j] = rs <= us + j < rs + NA_ROWS
                dr[ty, i, j] = min(max(us + j - r + NA_ROWS - 1, 0), nr - 1)
    cstart = np.clip(q - NA_COLS // 2, 0, GRID_W - NA_COLS)
    ok_col = (q[None, :] >= cstart[:, None]) & (q[None, :] < cstart[:, None] + NA_COLS)
    ok = ok_row[:, :, :, None, None] & ok_col[None, None, None]
    tab = toe[:, :, jnp.asarray(dr)]
    tab = jnp.where(jnp.asarray(ok), tab, -jnp.inf)
    tab = jnp.transpose(tab, (0, 2, 1, 3, 5, 4, 6))
    return tab.reshape(depth, 3, C_HEADS, NA_QROWS * GRID_W, NA_UNION * GRID_W)


NA_QROWS = 4
NA_UNION = NA_QROWS + NA_ROWS - 1


def _na_union_start(g, rows):
    lo = NA_QROWS * g - NA_ROWS // 2
    hi = rows - NA_UNION
    if isinstance(g, int):
        return min(max(lo, 0), hi)
    return jnp.clip(lo, 0, hi)


def _na_body(q_ref, k_ref, v_ref, bias_ref, qn_ref, kn_ref, seg_ref, o_ref, kh_ref, vh_ref):
    step = pl.program_id(1)
    seg = seg_ref[...]

    @pl.when(step == 0)
    def _():
        kh_ref[...] = _head_rms(k_ref[...], kn_ref[...], seg).astype(BF16)
        vh_ref[...] = v_ref[...].astype(BF16)

    nq = NA_QROWS * GRID_W
    q = _head_rms(q_ref[...], qn_ref[...], seg) * (HEAD ** -0.5 * math.log2(math.e))
    lane = _iota((nq, 128), 1)
    low = lane < HEAD
    sls = [slice((h // 2) * 128, (h // 2 + 1) * 128) for h in range(C_HEADS)]

    def masked_q(h):
        return jnp.where(low if h % 2 == 0 else jnp.logical_not(low), q[:, sls[h]], 0.0).astype(BF16)

    def store(hs, outs):
        for j in range(0, len(hs), 2):
            o_ref[:, sls[hs[j]]] = jnp.where(low, outs[j], outs[j + 1]).astype(o_ref.dtype)

    head_groups = [list(range(g, g + 4)) for g in range(0, C_HEADS, 4)]

    @pl.when(step == 0)
    def _():
        for hs in head_groups:
            qs = [masked_q(h) for h in hs]
            s = [_dot_nt(qs[j], kh_ref[0:CTX, sls[h]]) for j, h in enumerate(hs)]
            e = [jnp.exp2(x - jnp.max(x, axis=-1, keepdims=True)) for x in s]
            store(hs, [_dot(e[j].astype(BF16), vh_ref[0:CTX, sls[h]]) / jnp.sum(e[j], axis=-1, keepdims=True)
                       for j, h in enumerate(hs)])

    @pl.when(step > 0)
    def _():
        us = _na_union_start(step - 1, SEQ // GRID_W)
        k0 = pl.multiple_of(CTX + us * GRID_W, GRID_W)
        nloc = NA_UNION * GRID_W
        for hs in head_groups:
            qs = [masked_q(h) for h in hs]
            s_loc = [_dot_nt(qs[j], kh_ref[pl.ds(k0, nloc), sls[h]]) + bias_ref[0, h] for j, h in enumerate(hs)]
            s_ctx = [_dot_nt(qs[j], kh_ref[0:CTX, sls[h]]) for j, h in enumerate(hs)]
            mx = [jnp.maximum(jnp.max(a, axis=-1, keepdims=True), jnp.max(b, axis=-1, keepdims=True))
                  for a, b in zip(s_loc, s_ctx)]
            e_loc = [jnp.exp2(a - m_) for a, m_ in zip(s_loc, mx)]
            e_ctx = [jnp.exp2(b - m_) for b, m_ in zip(s_ctx, mx)]
            den = [jnp.sum(a, axis=-1, keepdims=True) + jnp.sum(b, axis=-1, keepdims=True)
                   for a, b in zip(e_loc, e_ctx)]
            store(hs, [(_dot(e_loc[j].astype(BF16), vh_ref[pl.ds(k0, nloc), sls[h]])
                        + _dot(e_ctx[j].astype(BF16), vh_ref[0:CTX, sls[h]])) / den[j] for j, h in enumerate(hs)])


def na_attention(z, bias, qn, kn, layer):
    t = z.shape[0]
    nb = t // ROWS
    nq = NA_QROWS * GRID_W
    assert nq == CTX
    steps = ROWS // nq
    groups = steps - 1

    def layout(i):
        g = jnp.maximum(i - 1, 0)
        return jnp.where(g == 0, 0, jnp.where(g == groups - 1, 2, 1))

    full = lambda shape: pl.BlockSpec(shape, lambda b, i: (0,) * len(shape))
    return pl.pallas_call(
        _na_body,
        grid=(nb, steps),
        in_specs=[pl.BlockSpec((nq, BLK), lambda b, i: (b * steps + i, 7)),
                  pl.BlockSpec((ROWS, BLK), lambda b, i: (b, 8)),
                  pl.BlockSpec((ROWS, BLK), lambda b, i: (b, 9)),
                  pl.BlockSpec((None, 1, C_HEADS, nq, NA_UNION * GRID_W), lambda b, i: (layer, layout(i), 0, 0, 0)),
                  full((1, BLK)), full((1, BLK)), full((BLK, BLK))],
        out_specs=pl.BlockSpec((nq, BLK), lambda b, i: (b * steps + i, 0)),
        out_shape=jax.ShapeDtypeStruct((t, BLK), BF16),
        scratch_shapes=[pltpu.VMEM((ROWS, BLK), BF16), pltpu.VMEM((ROWS, BLK), BF16)],
        compiler_params=_cp("arbitrary", "arbitrary"),
        name="na_attention",
    )(z, z, z, bias, qn, kn, _seg_ones(512, HEAD))


def _lru_body(x_ref, xp_ref, xn_ref, gate_ref, cw_ref, cb_ref, wa_ref, ba_ref, wx_ref, bx_ref, lam_ref,
              *rest, reverse, last, tpb):
    if last:
        h0_ref, o_ref, carry_ref = rest
    else:
        o_ref, carry_ref = rest
    p = pl.program_id(0) % tpb
    has_prev, has_next = _tile_flags(_lru_tile(p, tpb, reverse), tpb)

    @pl.when(p == 0)
    def _():
        carry_ref[...] = jnp.zeros_like(carry_ref)

    x = x_ref[...]
    n = x.shape[0]
    xm1 = _shift_rows(x, xp_ref[7:8, :], 1, has_prev)
    xp1 = _shift_rows(x, xn_ref[0:1, :], -1, has_next)
    rows = _iota((n, 1), 0)
    nx = jnp.where(has_next, xn_ref[0:2, :], 0.0)
    xp2 = pltpu.roll(x, n - 2, 0)
    xp2 = jnp.where(rows == n - 2, nx[0:1, :], jnp.where(rows == n - 1, nx[1:2, :], xp2))
    xr = (cw_ref[0:1, :] * xm1 + cw_ref[1:2, :] * x + cw_ref[2:3, :] * xp1 + cw_ref[3:4, :] * xp2) + cb_ref[...]
    xb = xr.astype(BF16)
    rg = _sigmoid(_dot(xb, wa_ref[...]) + ba_ref[...])
    ig = _sigmoid(_dot(xb, wx_ref[...]) + bx_ref[...])
    log_a = -RG_C * _softplus(-lam_ref[...]) * rg
    a = jnp.exp(log_a)
    b = jnp.sqrt(jnp.maximum(1.0 - jnp.exp(2.0 * log_a), 0.0)) * ig * xr
    sub = rows % 8
    k = 1
    while k < 8:
        if reverse:
            keep = sub < 8 - k
            a_s = pltpu.roll(a, n - k, 0)
            b_s = pltpu.roll(b, n - k, 0)
        else:
            keep = sub >= k
            a_s = pltpu.roll(a, k, 0)
            b_s = pltpu.roll(b, k, 0)
        b = jnp.where(keep, a * b_s + b, b)
        a = jnp.where(keep, a * a_s, a)
        k *= 2
    carry = carry_ref[...]
    order = range(n // 8 - 1, -1, -1) if reverse else range(n // 8)
    hs = [None] * (n // 8)
    for gi in order:
        hg = a[8 * gi:8 * gi + 8, :] * carry + b[8 * gi:8 * gi + 8, :]
        carry = hg[0:1, :] if reverse else hg[7:8, :]
        hs[gi] = hg
    h = jnp.concatenate(hs, axis=0)
    carry_ref[...] = carry
    if last:
        g = gate_ref[...]
        gelu = 0.5 * g * (1.0 + jnp.tanh(math.sqrt(2.0 / math.pi) * (g + 0.044715 * g * g * g)))
        o_ref[...] = ((h0_ref[...] + h) * gelu).astype(o_ref.dtype)
    else:
        o_ref[...] = h


def _lru_tile(p, tpb, reverse):
    cctx = CTX // TQ
    if reverse:
        return jnp.where(p < cctx, cctx - 1 - p, tpb - 1 + cctx - p)
    return p


def rglru(z, direction, conv_w, conv_b, wa_bd, ba, wx_bd, bx, lam, h0=None):
    t = z.shape[0]
    n_tiles = t // TQ
    tpb = ROWS // TQ
    reverse = direction == 1
    last = h0 is not None

    def blk(i):
        return (i // tpb) * tpb + _lru_tile(i % tpb, tpb, reverse)

    r8 = TQ // 8
    full = lambda shape: pl.BlockSpec(shape, lambda i: (0,) * len(shape))
    in_specs = [pl.BlockSpec((TQ, BLK), lambda i: (blk(i), 10)),
                pl.BlockSpec((8, BLK), lambda i: (jnp.maximum(blk(i) * r8 - 1, 0), 10)),
                pl.BlockSpec((8, BLK), lambda i: (jnp.minimum((blk(i) + 1) * r8, n_tiles * r8 - 1), 10)),
                pl.BlockSpec((TQ, BLK), lambda i: (blk(i), 11)),
                full((D_CONV, BLK)), full((1, BLK)), full((BLK, BLK)), full((1, BLK)),
                full((BLK, BLK)), full((1, BLK)), full((1, BLK))]
    args = [z, z, z, z, conv_w, conv_b, wa_bd, ba, wx_bd, bx, lam]
    if last:
        in_specs.append(pl.BlockSpec((TQ, BLK), lambda i: (blk(i), 0)))
        args.append(h0)

    return pl.pallas_call(
        functools.partial(_lru_body, reverse=reverse, last=last, tpb=tpb),
        grid=(n_tiles,),
        in_specs=in_specs,
        out_specs=pl.BlockSpec((TQ, BLK), lambda i: (blk(i), 0)),
        out_shape=jax.ShapeDtypeStruct((t, BLK), BF16 if last else F32),
        scratch_shapes=[pltpu.VMEM((1, BLK), F32)],
        compiler_params=_cp("arbitrary"),
        name="rglru_%d" % direction,
    )(*args)


def _merge_body(hn_ref, wg_ref, ba_ref, bb_ref, bc_ref, bd_ref, wb_ref, y_ref, *, tn):
    g = _dot(hn_ref[...], wg_ref[...])
    acc = None
    for j, br in enumerate((ba_ref, bb_ref, bc_ref, bd_ref)):
        term = _sigmoid(g[:, j * tn:(j + 1) * tn]) * _dot(br[...].astype(BF16), wb_ref[j])
        acc = term if acc is None else acc + term
    y_ref[...] = acc.astype(BF16)


def merge_branches(hn, wg, branches, wb, layer, *, tm=1152, tn=256):
    t, d = hn.shape
    br_spec = pl.BlockSpec((tm, BLK), lambda i, j: (i, 0))
    return pl.pallas_call(
        functools.partial(_merge_body, tn=tn),
        grid=(t // tm, d // tn),
        in_specs=[pl.BlockSpec((tm, d), lambda i, j: (i, 0)),
                  pl.BlockSpec((None, None, d, N_BRANCH * tn), lambda i, j: (layer, j, 0, 0)),
                  br_spec, br_spec, br_spec, br_spec,
                  pl.BlockSpec((None, N_BRANCH, BLK, tn), lambda i, j: (layer, 0, 0, j))],
        out_specs=pl.BlockSpec((tm, tn), lambda i, j: (i, j)),
        out_shape=jax.ShapeDtypeStruct((t, d), BF16),
        compiler_params=_cp("arbitrary", "arbitrary"),
        name="merge_branches",
    )(hn, wg, *branches, wb)


def _post_body(y_ref, wo_ref, x_ref, mod_ref, g2_ref, rw_ref, rb_ref,
               xo_ref, h2_ref, e_ref, w_ref, rank_ref, cnt_ref, base_ref, *, tm, tpb, slab):
    i = pl.program_id(0)
    row0 = (i % tpb) * tm

    @pl.when(i == 0)
    def _():
        base_ref[...] = jnp.zeros_like(base_ref)

    slabs = [slice(r, r + slab) for r in range(0, tm, slab)]
    attn = [_dot(y_ref[sl, :], wo_ref[...]) for sl in slabs]
    logit_parts = []
    for sl, a in zip(slabs, attn):
        x = x_ref[sl, :] + _mod_rows(mod_ref, 2, row0 + sl.start, slab) * a
        xo_ref[sl, :] = x
        h = _norm_mod(x, g2_ref[...], mod_ref, 3, row0 + sl.start, slab)
        hh, hl = _split(h)
        half = h.shape[1] // 2
        lo = pltpu.bitcast(hh[:, :half].astype(F32), jnp.uint32) >> 16
        hi = pltpu.bitcast(hh[:, half:].astype(F32), jnp.uint32) & jnp.uint32(0xFFFF0000)
        h2_ref[sl, :] = lo | hi
        logit_parts.append(_dot_nt(rw_ref[0], hh) + _dot_nt(rw_ref[0], hl) + _dot_nt(rw_ref[1], hh))
    logits = jnp.concatenate(logit_parts, axis=1)
    s = _sigmoid(logits)
    sb = s + rb_ref[...]
    per = N_EXPERTS // N_GROUPS
    srow = [s[e:e + 1, :] for e in range(N_EXPERTS)]
    brow = [sb[e:e + 1, :] for e in range(N_EXPERTS)]
    gscore = []
    for g in range(N_GROUPS):
        best = None
        for a in range(per):
            for b in range(a + 1, per):
                pair = brow[per * g + a] + brow[per * g + b]
                best = pair if best is None else jnp.maximum(best, pair)
        gscore.append(best)
    gbest = jnp.zeros((1, tm), jnp.int32)
    bscore = gscore[0]
    for g in range(1, N_GROUPS):
        better = gscore[g] > bscore
        gbest = jnp.where(better, g, gbest)
        bscore = jnp.where(better, gscore[g], bscore)

    def pick(rows, i):
        out = rows[i]
        for g in range(1, N_GROUPS):
            out = jnp.where(gbest == g, rows[per * g + i], out)
        return out

    cand = [pick(brow, i) for i in range(per)]
    cval = [pick(srow, i) for i in range(per)]
    i1 = jnp.zeros((1, tm), jnp.int32)
    m1 = cand[0]
    w1 = cval[0]
    for i_ in range(1, per):
        better = cand[i_] > m1
        i1 = jnp.where(better, i_, i1)
        m1 = jnp.where(better, cand[i_], m1)
        w1 = jnp.where(better, cval[i_], w1)
    i2 = jnp.full((1, tm), -1, jnp.int32)
    m2 = jnp.full((1, tm), -jnp.inf, F32)
    w2 = jnp.zeros((1, tm), F32)
    for i_ in range(per):
        better = jnp.logical_and(i1 != i_, cand[i_] > m2)
        i2 = jnp.where(better, i_, i2)
        m2 = jnp.where(better, cand[i_], m2)
        w2 = jnp.where(better, cval[i_], w2)
    e1 = per * gbest + i1
    e2 = per * gbest + i2
    wsum = w1 + w2
    e_ref[0:1, :] = e1
    e_ref[1:2, :] = e2
    w_ref[0:1, :] = w1 / wsum
    w_ref[1:2, :] = w2 / wsum
    er = _iota((N_EXPERTS, tm), 0)
    oh1 = jnp.where(er == e1, 1.0, 0.0)
    oh2 = jnp.where(er == e2, 1.0, 0.0)
    upper = jnp.where(_iota((tm, tm), 0) < _iota((tm, tm), 1), 1.0, 0.0).astype(BF16)
    ex1 = _dot(oh1.astype(BF16), upper)
    ex2 = _dot(oh2.astype(BF16), upper)
    tot1 = jnp.sum(oh1, axis=1, keepdims=True)
    tot2 = jnp.sum(oh2, axis=1, keepdims=True)
    base = base_ref[...]
    rank_ref[0:1, :] = jnp.sum(oh1 * (base + ex1), axis=0, keepdims=True).astype(jnp.int32)
    rank_ref[1:2, :] = jnp.sum(oh2 * (base + tot1 + ex2), axis=0, keepdims=True).astype(jnp.int32)
    base = base + tot1 + tot2
    base_ref[...] = base
    cnt_ref[...] = jnp.broadcast_to(base, cnt_ref.shape)


def post_attention(y, w_out, x, mod, g2, rw, rb, layer, *, tm=384):
    t, d = x.shape
    tpb = ROWS // tm
    full = lambda shape: pl.BlockSpec(shape, lambda i: (0,) * len(shape))
    row = pl.BlockSpec((tm, d), lambda i: (i, 0))
    tok = pl.BlockSpec((2, tm), lambda i: (0, i))
    return pl.pallas_call(
        functools.partial(_post_body, tm=tm, tpb=tpb, slab=128),
        grid=(t // tm,),
        in_specs=[row, pl.BlockSpec((None, d, d), lambda i: (layer, 0, 0)), row,
                  pl.BlockSpec((1, 2, 6, d), lambda i: (i // tpb, 0, 0, 0)),
                  full((1, d)), full((2, N_EXPERTS, d)), full((N_EXPERTS, 1))],
        out_specs=[row, pl.BlockSpec((tm, d // 2), lambda i: (i, 0)), tok, tok, tok, full((N_EXPERTS, 128))],
        out_shape=[jax.ShapeDtypeStruct((t, d), F32), jax.ShapeDtypeStruct((t, d // 2), jnp.uint32),
                   jax.ShapeDtypeStruct((2, t), jnp.int32), jax.ShapeDtypeStruct((2, t), F32),
                   jax.ShapeDtypeStruct((2, t), jnp.int32), jax.ShapeDtypeStruct((N_EXPERTS, 128), F32)],
        scratch_shapes=[pltpu.VMEM((N_EXPERTS, 1), F32)],
        compiler_params=_cp("arbitrary"),
        name="post_attention",
    )(y, w_out, x, mod, g2, rw, rb)


def _row_copy(src, dst, sem):
    return pltpu.make_async_copy(src, dst, sem)


def _dispatch_body(pos_ref, h_ref, xs_in_ref, xs_ref, sem, *, tm):
    del xs_in_ref

    def start(t, carry):
        for k in range(2):
            p = pos_ref[0, 0, k * tm + t]
            _row_copy(h_ref.at[pl.ds(t, 1)], xs_ref.at[pl.ds(p, 1)], sem).start(priority=k)
        return carry

    lax.fori_loop(0, tm, start, 0, unroll=8)
    for k in range(2):
        _row_copy(h_ref, xs_ref.at[pl.ds(0, tm)], sem).wait()


def moe_dispatch(pos3, h2, n_rows, *, tm=256):
    t, d = h2.shape
    xs0 = jnp.zeros((n_rows, d), h2.dtype)
    return pl.pallas_call(
        functools.partial(_dispatch_body, tm=tm),
        grid=(t // tm,),
        in_specs=[pl.BlockSpec((1, 1, 2 * tm), lambda i: (i, 0, 0), memory_space=pltpu.SMEM),
                  pl.BlockSpec((tm, d), lambda i: (i, 0)),
                  pl.BlockSpec(memory_space=pl.ANY)],
        out_specs=pl.BlockSpec(memory_space=pl.ANY),
        out_shape=jax.ShapeDtypeStruct((n_rows, d), h2.dtype),
        scratch_shapes=[pltpu.SemaphoreType.DMA(())],
        input_output_aliases={2: 0},
        compiler_params=_cp("arbitrary"),
        name="moe_dispatch",
    )(pos3, h2, xs0)


def _ffn_body(te_ref, nv_ref, x_ref, wg_ref, wu_ref, wd_ref, y_ref):
    del te_ref
    i = pl.program_id(0)

    @pl.when(i < nv_ref[0])
    def _():
        p = x_ref[...]
        half = p.shape[1]
        x_lo = pltpu.bitcast(p << 16, F32).astype(BF16)
        x_hi = pltpu.bitcast(p & jnp.uint32(0xFFFF0000), F32).astype(BF16)
        g = _dot(x_lo, wg_ref[0:half, :]) + _dot(x_hi, wg_ref[half:2 * half, :])
        u = _dot(x_lo, wu_ref[0:half, :]) + _dot(x_hi, wu_ref[half:2 * half, :])
        y_ref[...] = _dot((g * _sigmoid(g) * u).astype(BF16), wd_ref[...])

    @pl.when(i >= nv_ref[0])
    def _():
        y_ref[...] = jnp.zeros_like(y_ref)


def moe_ffn(tile_expert, n_valid, xs, wg, wu, wd, layer):
    n_rows, dp = xs.shape
    d, ff = wg.shape[-2:]
    n_tiles = n_rows // FFN_TM
    grid_spec = pltpu.PrefetchScalarGridSpec(
        num_scalar_prefetch=2,
        grid=(n_tiles,),
        in_specs=[pl.BlockSpec((FFN_TM, dp), lambda i, te, nv: (jnp.minimum(i, nv[0] - 1), 0)),
                  pl.BlockSpec((None, None, d, ff), lambda i, te, nv: (layer, te[i], 0, 0)),
                  pl.BlockSpec((None, None, d, ff), lambda i, te, nv: (layer, te[i], 0, 0)),
                  pl.BlockSpec((None, None, ff, d), lambda i, te, nv: (layer, te[i], 0, 0))],
        out_specs=pl.BlockSpec((FFN_TM, d), lambda i, te, nv: (i, 0)))
    return pl.pallas_call(
        _ffn_body,
        grid_spec=grid_spec,
        out_shape=jax.ShapeDtypeStruct((n_rows, d), F32),
        compiler_params=_cp("arbitrary"),
        name="moe_ffn",
    )(tile_expert, n_valid, xs, wg, wu, wd)


def _combine_body(pos_ref, ys_ref, x_ref, w_ref, mod_ref, o_ref, buf_ref, sem, *, tm, tpb):
    def start(t, carry):
        for k in range(2):
            p = pos_ref[0, 0, k * tm + t]
            _row_copy(ys_ref.at[pl.ds(p, 1)], buf_ref.at[k, pl.ds(t, 1)], sem).start(priority=k)
        return carry

    lax.fori_loop(0, tm, start, 0, unroll=8)
    for k in range(2):
        _row_copy(ys_ref.at[pl.ds(0, tm)], buf_ref.at[k], sem).wait()
    w = w_ref[...]
    y = w[:, 0:1] * buf_ref[0] + w[:, 1:2] * buf_ref[1]
    row0 = (pl.program_id(0) % tpb) * tm
    o_ref[...] = x_ref[...] + _mod_rows(mod_ref, 5, row0, tm) * y


def moe_combine(pos3, ys, x, wt, mod, *, tm=256):
    t, d = x.shape
    tpb = ROWS // tm
    row = pl.BlockSpec((tm, d), lambda i: (i, 0))
    return pl.pallas_call(
        functools.partial(_combine_body, tm=tm, tpb=tpb),
        grid=(t // tm,),
        in_specs=[pl.BlockSpec((1, 1, 2 * tm), lambda i: (i, 0, 0), memory_space=pltpu.SMEM),
                  pl.BlockSpec(memory_space=pl.ANY), row,
                  pl.BlockSpec((tm, 2), lambda i: (i, 0)),
                  pl.BlockSpec((1, 2, 6, d), lambda i: (i // tpb, 0, 0, 0))],
        out_specs=row,
        out_shape=jax.ShapeDtypeStruct((t, d), F32),
        scratch_shapes=[pltpu.VMEM((2, tm, d), F32), pltpu.SemaphoreType.DMA(())],
        compiler_params=_cp("arbitrary"),
        name="moe_combine",
    )(pos3, ys, x, wt, mod)


def moe_layout(e, rank, counts, *, tm=256):
    t = e.shape[1]
    n_rows_max = -(-(2 * t + N_EXPERTS * (FFN_TM - 1)) // FFN_TM) * FFN_TM
    n_tiles = n_rows_max // FFN_TM
    cnt = counts[:, 0].astype(jnp.int32)
    gsz = ((cnt + FFN_TM - 1) // FFN_TM) * FFN_TM
    end = jnp.cumsum(gsz)
    off = end - gsz
    onehot = e[:, :, None] == jnp.arange(N_EXPERTS, dtype=jnp.int32)
    pos = rank + jnp.sum(jnp.where(onehot, off, 0), axis=-1)
    pos3 = pos.reshape(2, t // tm, tm).transpose(1, 0, 2).reshape(t // tm, 1, 2 * tm)
    n_valid = (end[-1] // FFN_TM).astype(jnp.int32)
    starts = jnp.minimum(jnp.arange(n_tiles, dtype=jnp.int32), n_valid - 1) * FFN_TM
    tile_expert = jnp.sum(starts[:, None] >= end[None, :], axis=1).astype(jnp.int32)
    return pos3, tile_expert, n_valid.reshape(1), n_rows_max


def _block_diag(w):
    n, bi, bj = w.shape[-3:]
    eye = jnp.eye(n, dtype=w.dtype)
    out = w[..., :, :, None, :] * eye[:, None, :, None]
    return out.reshape(w.shape[:-3] + (n * bi, n * bj))


def kernel(x, c, ctx, c_ctx, w_mod, b_mod, norm1_g, norm2_g, w_in, a_mu, a_w0, a_w2, a_a0, a_a2, a_g2, a_kk, a_ka,
           a_rk, a_gn_g, a_gn_b, b_qn, b_kn, b_lam, b_subln, c_qn, c_kn, c_rpb, d_conv_w, d_conv_b, d_wa, d_ba,
           d_wx, d_bx, d_lam, w_branch, w_out, router_w, router_b, e_gate, e_up, e_down):
    nb = x.shape[0]
    d = D_MODEL
    t = nb * ROWS
    xs = jnp.concatenate([ctx, x], axis=1).reshape(t, d)

    cc = jnp.concatenate([c, c_ctx[None], jnp.zeros((-(nb + 1) % 8, d), F32)], axis=0)
    mods = mod_vectors(cc, w_mod, b_mod)
    mod_l = mods[:, :nb].reshape(DEPTH, nb, 1, 6, d)
    mod_c = jnp.broadcast_to(mods[:, nb].reshape(DEPTH, 1, 1, 6, d), (DEPTH, nb, 1, 6, d))
    mod = jnp.concatenate([mod_l, mod_c], axis=2)

    o_b, o_c, o_d, o_g = A_COLS, A_COLS + 1536, A_COLS + 3072, A_COLS + 3072 + 1024
    w_mix = jnp.concatenate([w_in[:, :, :o_b], jnp.zeros((DEPTH, d, 2048 - A_COLS), F32), w_in[:, :, o_b:o_g]],
                            axis=-1).astype(BF16)
    tn_m = 256
    w_gate = w_in[:, :, o_g:].astype(BF16).reshape(DEPTH, d, N_BRANCH, d // tn_m, tn_m)
    w_gate = jnp.transpose(w_gate, (0, 3, 1, 2, 4)).reshape(DEPTH, d // tn_m, d, N_BRANCH * tn_m)
    mu = jnp.pad(a_mu, ((0, 0), (0, 0), (0, 2048 - A_COLS)))
    w2bd = _block_diag(a_w2).astype(BF16)
    a2bd = _block_diag(a_a2).astype(BF16)
    g2 = a_g2.astype(BF16)
    wa_bd = _block_diag(d_wa).astype(BF16)
    wx_bd = _block_diag(d_wx).astype(BF16)
    wb = w_branch.astype(BF16)
    wo = w_out.astype(BF16)
    rw = jnp.stack(_split(router_w.T), axis=0)
    rb = router_b.reshape(N_EXPERTS, 1)
    eg, eu, ed = e_gate.astype(BF16), e_up.astype(BF16), e_down.astype(BF16)
    cos, sin = rope_tables()
    bias = rpb_tables(c_rpb)
    qn_c = jnp.tile(c_qn, (1, C_HEADS)).reshape(DEPTH, 1, BLK)
    kn_c = jnp.tile(c_kn, (1, C_HEADS)).reshape(DEPTH, 1, BLK)

    for l in range(DEPTH):
        lam_init = 0.8 - 0.6 * math.exp(-0.3 * l)
        z, hn = in_projection(xs, mod[l], norm1_g[l].reshape(1, d), w_mix, l)
        prep = rwkv_prepare(z, mu[l], w2bd[l], a2bd[l], g2[l], a_w0[l].reshape(1, 1024), a_a0[l].reshape(1, 1024),
                            a_kk[l].reshape(1, BLK), a_ka[l].reshape(1, BLK), a_rk[l].reshape(1, BLK))
        ya = rwkv_scan(prep, 0)
        oa = rwkv_scan(prep, 1, ya, a_gn_g[l].reshape(1, BLK), a_gn_b[l].reshape(1, BLK))
        ob = diff_attention(z, cos, sin, b_qn[l].reshape(1, 128), b_kn[l].reshape(1, 128), b_lam[l],
                            b_subln[l].reshape(1, 128), lam_init)
        oc = na_attention(z, bias, qn_c[l], kn_c[l], l)
        lru = [(d_conv_w[l], d_conv_b[l].reshape(1, BLK), wa_bd[l, k], d_ba[l, k].reshape(1, BLK), wx_bd[l, k],
                d_bx[l, k].reshape(1, BLK), d_lam[l, k].reshape(1, BLK)) for k in range(2)]
        hd = rglru(z, 0, *lru[0])
        od = rglru(z, 1, *lru[1], h0=hd)
        y = merge_branches(hn, w_gate, (oa, ob, oc, od), wb, l, tn=tn_m)
        xs, h2, e, wts, rank, counts = post_attention(y, wo, xs, mod[l], norm2_g[l].reshape(1, d), rw, rb, l)
        pos3, tile_expert, n_valid, n_rows = moe_layout(e, rank, counts)
        xg = moe_dispatch(pos3, h2, n_rows)
        yg = moe_ffn(tile_expert, n_valid, xg, eg, eu, ed, l)
        xs = moe_combine(pos3, yg, xs, wts.T, mod[l])
    return xs.reshape(nb, ROWS, d)[:, CTX:]
```

```python
import functools
import math

import numpy as np
import jax
import jax.numpy as jnp
from jax import lax
from jax.experimental import pallas as pl
from jax.experimental.pallas import tpu as pltpu

F32 = jnp.float32
BF16 = jnp.bfloat16

D_MODEL = 2048
DEPTH = 4
GRID_W = 64
CTX = 256
SEQ = 2048
ROWS = CTX + SEQ
HEAD = 64
A_WIDTH = 512
A_COLS = 1920
A_GN_EPS = 64e-5
B_HEADS = 4
C_HEADS = 8
NA_ROWS = 8
NA_COLS = 16
D_WIDTH = 512
D_CONV = 4
RG_C = 8.0
N_BRANCH = 4
N_EXPERTS = 16
N_GROUPS = 4
D_FF = 1024
ROPE_BASE = 10000.0
NORM_EPS = 1e-6
Z_COLS = 6144
BLK = 512

VMEM_LIMIT = 56 * 1024 * 1024

CHUNK = 64
TQ = 256
FFN_TM = 512


def _cp(*sem):
    return pltpu.CompilerParams(dimension_semantics=sem, vmem_limit_bytes=VMEM_LIMIT)


def _dot(a, b):
    return jnp.dot(a, b, preferred_element_type=F32)


def _dot_nt(a, b):
    return lax.dot_general(a, b, (((1,), (1,)), ((), ())), preferred_element_type=F32)


def _dot_tn(a, b):
    return lax.dot_general(a, b, (((0,), (0,)), ((), ())), preferred_element_type=F32)


def _split(x):
    hi = x.astype(BF16)
    lo = (x - hi.astype(F32)).astype(BF16)
    return hi, lo


def _dot_x(x, m):
    hi, lo = _split(x)
    return _dot(hi, m) + _dot(lo, m)


def _sigmoid(x):
    return 1.0 / (1.0 + jnp.exp(-x))


def _softplus(x):
    return jnp.maximum(x, 0.0) + jnp.log(1.0 + jnp.exp(-jnp.abs(x)))


def _iota(shape, dim):
    return lax.broadcasted_iota(jnp.int32, shape, dim)


def _seg_ones(width, seg):
    r = np.arange(width)
    return jnp.asarray((r[:, None] // seg) == (r[None, :] // seg), dtype=BF16)


def _mod_body(c_ref, w_ref, b_ref, o_ref):
    c = c_ref[...]
    s = (c * _sigmoid(c)).astype(BF16)
    o_ref[...] = _dot(s, w_ref[...].astype(BF16)) + b_ref[...]


def mod_vectors(cc, w_mod, b_mod):
    depth, d, n = w_mod.shape
    r = cc.shape[0]
    tn = 1536
    return pl.pallas_call(
        _mod_body,
        grid=(depth, n // tn),
        in_specs=[pl.BlockSpec((r, d), lambda l, j: (0, 0)),
                  pl.BlockSpec((None, d, tn), lambda l, j: (l, 0, j)),
                  pl.BlockSpec((None, 1, tn), lambda l, j: (l, 0, j))],
        out_specs=pl.BlockSpec((None, r, tn), lambda l, j: (l, 0, j)),
        out_shape=jax.ShapeDtypeStruct((depth, r, n), F32),
        compiler_params=_cp("arbitrary", "arbitrary"),
        name="mod_vectors",
    )(cc, w_mod, b_mod.reshape(depth, 1, n))


def _mod_rows(mod_ref, idx, row0, tm):
    rows = row0 + _iota((tm, 1), 0)
    return jnp.where(rows < CTX, mod_ref[0, 1, idx:idx + 1, :], mod_ref[0, 0, idx:idx + 1, :])


def _norm_mod(x, g, mod_ref, shift_idx, row0, tm):
    ms = jnp.mean(x * x, axis=-1, keepdims=True)
    y = x * lax.rsqrt(ms + NORM_EPS) * g
    return y * (1.0 + _mod_rows(mod_ref, shift_idx + 1, row0, tm)) + _mod_rows(mod_ref, shift_idx, row0, tm)


def _inproj_body(x_ref, mod_ref, g_ref, w_ref, z_ref, hn_ref, *, tm, tpb, slab):
    i = pl.program_id(0)

    @pl.when(pl.program_id(1) == 0)
    def _():
        for r in range(0, tm, slab):
            row0 = (i % tpb) * tm + r
            hn = _norm_mod(x_ref[r:r + slab, :], g_ref[...], mod_ref, 0, row0, slab).astype(BF16)
            hn_ref[r:r + slab, :] = hn
            z_ref[r:r + slab, :] = _dot(hn, w_ref[...])

    @pl.when(pl.program_id(1) != 0)
    def _():
        z_ref[...] = _dot(hn_ref[...], w_ref[...])


def in_projection(x, mod, g, w, layer, *, tm=1152, tn=768):
    t, d = x.shape
    n = w.shape[2]
    tpb = ROWS // tm
    return pl.pallas_call(
        functools.partial(_inproj_body, tm=tm, tpb=tpb, slab=384),
        grid=(t // tm, n // tn),
        in_specs=[pl.BlockSpec((tm, d), lambda i, j: (i, 0)),
                  pl.BlockSpec((1, 2, 6, d), lambda i, j: (i // tpb, 0, 0, 0)),
                  pl.BlockSpec((1, d), lambda i, j: (0, 0)),
                  pl.BlockSpec((None, d, tn), lambda i, j: (layer, 0, j))],
        out_specs=[pl.BlockSpec((tm, tn), lambda i, j: (i, j)),
                   pl.BlockSpec((tm, d), lambda i, j: (i, 0))],
        out_shape=[jax.ShapeDtypeStruct((t, n), F32), jax.ShapeDtypeStruct((t, d), BF16)],
        compiler_params=_cp("arbitrary", "arbitrary"),
        name="in_projection",
    )(x, mod, g, w)


def _tile_flags(ti, tpb):
    ctx_tiles = CTX // TQ
    has_prev = jnp.logical_and(ti != 0, ti != ctx_tiles)
    has_next = jnp.logical_and(ti != ctx_tiles - 1, ti != tpb - 1)
    return has_prev, has_next


def _halo_specs(width, col, n_tiles):
    r8 = TQ // 8
    return [pl.BlockSpec((TQ, width), lambda i: (i, col)),
            pl.BlockSpec((8, width), lambda i: (jnp.maximum(i * r8 - 1, 0), col)),
            pl.BlockSpec((8, width), lambda i: (jnp.minimum((i + 1) * r8, n_tiles * r8 - 1), col))]


def _shift_rows(z, halo_row, k, use_halo):
    n = z.shape[0]
    rows = _iota((n, 1), 0)
    fill = jnp.where(use_halo, halo_row, 0.0)
    if k > 0:
        return jnp.where(rows < k, fill, pltpu.roll(z, k, 0))
    return jnp.where(rows >= n + k, fill, pltpu.roll(z, n + k, 0))


def _rwkv_prep_body(z_ref, zp_ref, zn_ref, mu_ref, w2_ref, a2_ref, g2_ref, w0_ref, a0_ref,
                    kk_ref, ka_ref, rk_ref, seg_ref, o_ref, *, tpb):
    has_prev, has_next = _tile_flags(pl.program_id(0) % tpb, tpb)
    z = z_ref[...]
    prev = _shift_rows(z, zp_ref[7:8, :], 1, has_prev)
    nxt = _shift_rows(z, zn_ref[0:1, :], -1, has_next)
    zs = z + mu_ref[0:1, :] * (prev - z) + mu_ref[1:2, :] * (nxt - z)
    r = zs[:, 0:512]
    k = zs[:, 512:1024]
    v = zs[:, 1024:1536]
    wl = zs[:, 1536:1664]
    al = zs[:, 1664:1792]
    gl = zs[:, 1792:1920]
    seg = seg_ref[...]
    wlog = -_softplus(-(w0_ref[...] + _dot(jnp.tanh(wl).astype(BF16), w2_ref[...]))) - 0.5
    lw = -jnp.exp(wlog)
    a = _sigmoid(a0_ref[...] + _dot(al.astype(BF16), a2_ref[...]))
    g = _dot(_sigmoid(gl).astype(BF16), g2_ref[...])
    kk = k * kk_ref[...]
    kk = kk / jnp.maximum(jnp.sqrt(_dot_x(kk * kk, seg)), 1e-12)
    bonus = _dot_x(r * k * rk_ref[...], seg) * v
    ka = ka_ref[...]
    o_ref[:, 0:512] = r
    o_ref[:, 512:1024] = v
    o_ref[:, 1024:1536] = kk
    o_ref[:, 1536:2048] = bonus
    o_ref[:, 2048:2560] = g
    for d in range(2):
        ad = a[:, d * 512:(d + 1) * 512]
        base = 2560 + d * 1536
        o_ref[:, base:base + 512] = lw[:, d * 512:(d + 1) * 512]
        o_ref[:, base + 512:base + 1024] = kk * ad
        o_ref[:, base + 1024:base + 1536] = k * (1.0 + (ad - 1.0) * ka)


A_PREP_COLS = 2560 + 2 * 1536


def rwkv_prepare(z, mu, w2bd, a2bd, g2, w0, a0, k_k, k_a, r_k):
    t = z.shape[0]
    n_tiles = t // TQ
    tpb = ROWS // TQ
    full = lambda shape: pl.BlockSpec(shape, lambda i: (0,) * len(shape))
    return pl.pallas_call(
        functools.partial(_rwkv_prep_body, tpb=tpb),
        grid=(n_tiles,),
        in_specs=_halo_specs(2048, 0, n_tiles) + [
            full((2, 2048)), full((128, 1024)), full((128, 1024)), full((128, 512)),
            full((1, 1024)), full((1, 1024)), full((1, 512)), full((1, 512)), full((1, 512)),
            full((512, 512))],
        out_specs=pl.BlockSpec((TQ, A_PREP_COLS), lambda i: (i, 0)),
        out_shape=jax.ShapeDtypeStruct((t, A_PREP_COLS), F32),
        compiler_params=_cp("arbitrary"),
        name="rwkv_prepare",
    )(z, z, z, mu, w2bd, a2bd, g2, w0, a0, k_k, k_a, r_k, _seg_ones(512, HEAD))


def _rwkv_scan_body(r_ref, v_ref, kk_ref, lw_ref, b_ref, ke_ref, *rest, reverse, last, nbb):
    if last:
        y0_ref, bonus_ref, g_ref, gng_ref, gnb_ref, seg_ref, o_ref, s_ref = rest
    else:
        o_ref, s_ref = rest
    c = CHUNK
    nh = A_WIDTH // HEAD

    @pl.when(pl.program_id(1) == 0)
    def _():
        s_ref[...] = jnp.zeros_like(s_ref)

    ti = _iota((c, c), 0)
    si = _iota((c, c), 1)
    before = (si > ti) if reverse else (si < ti)
    upto = jnp.logical_or(before, si == ti)
    tri = jnp.where(upto, 1.0, 0.0).astype(BF16)
    eye = jnp.where(si == ti, 1.0, 0.0)
    pair = (ti >> 1) == (si >> 1)
    off_masks = []
    lg = 1
    while (2 << lg) <= c:
        off_masks.append(jnp.logical_and((ti >> (lg + 1)) == (si >> (lg + 1)), (ti >> lg) != (si >> lg)))
        lg += 1
    ti2 = _iota((c, 2 * c), 0)
    si2 = _iota((c, 2 * c), 1) % c
    upto2 = (si2 >= ti2) if reverse else (si2 <= ti2)
    hi2 = _iota((c, 2 * c), 1) >= c
    before_hi = jnp.logical_and(hi2, (si2 > ti2) if reverse else (si2 < ti2))

    chains = [(bi, h) for bi in range(nbb) for h in range(nh)]
    cols = {}
    gammas = []
    for bi in range(nbb):
        lw = lw_ref[bi]
        lw_hi, lw_lo = _split(lw)
        cs = _dot(tri, lw_hi) + _dot(tri, lw_lo)
        tot = jnp.sum(lw, axis=0, keepdims=True)
        e_neg = jnp.exp(-cs)
        e_hat = jnp.exp(tot - cs)
        gammas.append(jnp.exp(tot))
        b = b_ref[bi]
        ke = ke_ref[bi]
        mats = dict(r=r_ref[bi] * jnp.exp(cs), a=-kk_ref[bi] * jnp.exp(cs - lw), b=b * e_neg, k=ke * e_neg,
                    bh=b * e_hat, kh=ke * e_hat, v=v_ref[bi])
        for name, m in mats.items():
            for h in range(nh):
                cols[name, bi, h] = m[:, h * HEAD:(h + 1) * HEAD].astype(BF16)

    def per_chain(fn):
        return [fn(i, bi, h) for i, (bi, h) in enumerate(chains)]

    prod = per_chain(lambda i, bi, h: _dot_nt(
        jnp.concatenate([cols['a', bi, h], cols['r', bi, h]], axis=0),
        jnp.concatenate([cols['b', bi, h], cols['k', bi, h]], axis=0)))
    lab = per_chain(lambda i, bi, h: jnp.where(before, prod[i][0:c, 0:c], 0.0))
    lak2 = per_chain(lambda i, bi, h: jnp.where(before_hi, prod[i][0:c, :], 0.0).astype(BF16))
    m = per_chain(lambda i, bi, h: jnp.where(upto2, prod[i][c:2 * c, :], 0.0).astype(BF16))
    tinv = per_chain(lambda i, bi, h: eye + jnp.where(pair, lab[i], 0.0))
    for off_mask in off_masks:
        tb = [t.astype(BF16) for t in tinv]
        x = per_chain(lambda i, bi, h: _dot(jnp.where(off_mask, lab[i], 0.0).astype(BF16), tb[i]))
        tinv = per_chain(lambda i, bi, h: tinv[i] + _dot(tb[i], x[i].astype(BF16)))
    vv = per_chain(lambda i, bi, h: jnp.concatenate([cols['v', bi, h], cols['v', bi, h]], axis=0))
    lv = per_chain(lambda i, bi, h: _dot(lak2[i], vv[i]))
    tb = [t.astype(BF16) for t in tinv]
    w1 = per_chain(lambda i, bi, h: _dot(tb[i], cols['a', bi, h]))
    u2 = per_chain(lambda i, bi, h: _dot(tb[i], lv[i].astype(BF16)))
    s0 = per_chain(lambda i, bi, h: s_ref[bi, h])
    s0b = [s.astype(BF16) for s in s0]
    ws = per_chain(lambda i, bi, h: _dot_nt(
        jnp.concatenate([w1[i].astype(BF16), cols['r', bi, h]], axis=0), s0b[i]))
    u = per_chain(lambda i, bi, h: ws[i][0:c] + u2[i])
    uv = per_chain(lambda i, bi, h: jnp.concatenate([u[i].astype(BF16), cols['v', bi, h]], axis=0))
    ys = per_chain(lambda i, bi, h: ws[i][c:2 * c] + _dot(m[i], uv[i]))
    for i, (bi, h) in enumerate(chains):
        bkh = jnp.concatenate([cols['bh', bi, h], cols['kh', bi, h]], axis=0)
        s_ref[bi, h] = s0[i] * gammas[bi][:, h * HEAD:(h + 1) * HEAD] + _dot_tn(uv[i], bkh)
    yb = [jnp.concatenate(ys[bi * nh:(bi + 1) * nh], axis=1) for bi in range(nbb)]
    if last:
        y = jnp.concatenate([yb[bi] + y0_ref[bi] for bi in range(nbb)], axis=0)
        seg = seg_ref[...]
        inv = 1.0 / HEAD
        mean = _dot_x(y, seg) * inv
        yc = y - mean
        var = _dot_x(yc * yc, seg) * inv
        yn = yc * lax.rsqrt(var + A_GN_EPS) * gng_ref[...] + gnb_ref[...]
        for bi in range(nbb):
            o_ref[bi] = ((yn[bi * c:(bi + 1) * c] + bonus_ref[bi]) * g_ref[bi]).astype(o_ref.dtype)
    else:
        for bi in range(nbb):
            o_ref[bi] = yb[bi]


def rwkv_scan(prep, direction, y0=None, gn_g=None, gn_b=None, *, nbb=4):
    t = prep.shape[0]
    nb = t // ROWS
    cpb = ROWS // CHUNK
    cctx = CTX // CHUNK
    reverse = direction == 1
    last = y0 is not None
    prep3 = prep.reshape(nb, ROWS, prep.shape[1])

    def blk(p):
        if reverse:
            return jnp.where(p < cctx, cctx - 1 - p, cpb - 1 + cctx - p)
        return p

    def col(cb):
        return pl.BlockSpec((nbb, CHUNK, BLK), lambda g, p: (g, blk(p), cb))

    base = 5 + 3 * direction
    in_specs = [col(0), col(1), col(2), col(base), col(base + 1), col(base + 2)]
    args = [prep3] * 6
    if last:
        in_specs += [col(0), col(3), col(4),
                     pl.BlockSpec((1, BLK), lambda g, p: (0, 0)), pl.BlockSpec((1, BLK), lambda g, p: (0, 0)),
                     pl.BlockSpec((BLK, BLK), lambda g, p: (0, 0))]
        args += [y0.reshape(nb, ROWS, BLK), prep3, prep3, gn_g, gn_b, _seg_ones(512, HEAD)]
    out = pl.pallas_call(
        functools.partial(_rwkv_scan_body, reverse=reverse, last=last, nbb=nbb),
        grid=(nb // nbb, cpb),
        in_specs=in_specs,
        out_specs=col(0),
        out_shape=jax.ShapeDtypeStruct((nb, ROWS, BLK), BF16 if last else F32),
        scratch_shapes=[pltpu.VMEM((nbb, A_WIDTH // HEAD, HEAD, HEAD), F32)],
        compiler_params=_cp("arbitrary", "arbitrary"),
        name="rwkv_scan_%d" % direction,
    )(*args)
    return out.reshape(t, BLK)


def _rope(x, cos, sin):
    lane = _iota(x.shape, 1)
    partner = jnp.where((lane % HEAD) < HEAD // 2, pltpu.roll(x, 128 - HEAD // 2, 1), pltpu.roll(x, HEAD // 2, 1))
    return x * cos + partner * sin


def _head_rms(x, gain, seg):
    ms = _dot_x(x * x, seg) * (1.0 / HEAD)
    return x * lax.rsqrt(ms + NORM_EPS) * gain


def _diff_body(q_ref, k_ref, v_ref, cos_ref, sin_ref, qn_ref, kn_ref, lam_ref, sub_ref, seg_ref,
               o_ref, kh_ref, vh_ref, *, lam_init, nq):
    iq = pl.program_id(2)
    seg = seg_ref[...]

    @pl.when(iq == 0)
    def _():
        kh = _rope(_head_rms(k_ref[...], kn_ref[...], seg), cos_ref[...], sin_ref[...])
        kh_ref[...] = kh.astype(BF16)
        vh_ref[:, 0:128] = v_ref[...].astype(BF16)
        vh_ref[:, 128:256] = jnp.ones((ROWS, 128), BF16)

    lv = lam_ref[...]
    lam = (jnp.exp(jnp.sum(lv[0:1] * lv[1:2], axis=1, keepdims=True))
           - jnp.exp(jnp.sum(lv[2:3] * lv[3:4], axis=1, keepdims=True)) + lam_init)
    r0 = pl.multiple_of(iq * TQ, TQ)
    q = _rope(_head_rms(q_ref[...], qn_ref[...], seg), cos_ref[pl.ds(r0, TQ), :], sin_ref[pl.ds(r0, TQ), :])
    q = q * (HEAD ** -0.5 * math.log2(math.e))
    lane = _iota(q.shape, 1)
    q1 = jnp.where(lane < HEAD, q, 0.0).astype(BF16)
    q2 = jnp.where(lane >= HEAD, q, 0.0).astype(BF16)

    def attend(kh, vh):
        s = [_dot_nt(qs, kh) for qs in (q1, q2)]
        e = [jnp.exp2(x - jnp.max(x, axis=-1, keepdims=True)) for x in s]
        pv = [_dot(x.astype(BF16), vh) for x in e]
        o = pv[0][:, 0:128] / pv[0][:, 128:256] - lam * (pv[1][:, 0:128] / pv[1][:, 128:256])
        ms = jnp.mean(o * o, axis=-1, keepdims=True)
        o_ref[...] = (o * lax.rsqrt(ms + NORM_EPS) * sub_ref[...] * (1.0 - lam_init)).astype(o_ref.dtype)

    @pl.when(iq == 0)
    def _():
        attend(kh_ref[0:CTX, :], vh_ref[0:CTX, :])

    @pl.when(iq != 0)
    def _():
        attend(kh_ref[...], vh_ref[...])


def diff_attention(z, cos, sin, qn, kn, lam_vecs, subln, lam_init):
    t = z.shape[0]
    nb = t // ROWS
    nq = ROWS // TQ
    cb = BLK // 128
    full = lambda shape: pl.BlockSpec(shape, lambda b, h, i: (0,) * len(shape))
    return pl.pallas_call(
        functools.partial(_diff_body, lam_init=lam_init, nq=nq),
        grid=(nb, B_HEADS, nq),
        in_specs=[pl.BlockSpec((TQ, 128), lambda b, h, i: (b * nq + i, 4 * cb + h)),
                  pl.BlockSpec((ROWS, 128), lambda b, h, i: (b, 5 * cb + h)),
                  pl.BlockSpec((ROWS, 128), lambda b, h, i: (b, 6 * cb + h)),
                  full((ROWS, 128)), full((ROWS, 128)), full((1, 128)), full((1, 128)),
                  full((4, HEAD)), full((1, 128)), full((128, 128))],
        out_specs=pl.BlockSpec((TQ, 128), lambda b, h, i: (b * nq + i, h)),
        out_shape=jax.ShapeDtypeStruct((t, BLK), BF16),
        scratch_shapes=[pltpu.VMEM((ROWS, 128), BF16), pltpu.VMEM((ROWS, 256), BF16)],
        compiler_params=_cp("arbitrary", "arbitrary", "arbitrary"),
        name="diff_attention",
    )(z, z, z, cos, sin, qn, kn, lam_vecs, subln, _seg_ones(128, HEAD))


def rope_tables():
    t = np.arange(SEQ)
    quarter = HEAD // 4
    inv = (1.0 / (ROPE_BASE ** (np.arange(quarter, dtype=np.float32) / quarter))).astype(np.float32)
    row = (t // GRID_W).astype(np.float32)[:, None] * inv
    col = (t % GRID_W).astype(np.float32)[:, None] * inv
    ang = np.concatenate([row, col], axis=-1)
    cos = np.concatenate([np.ones((CTX, HEAD // 2), np.float32), np.cos(ang)], axis=0)
    sin = np.concatenate([np.zeros((CTX, HEAD // 2), np.float32), np.sin(ang)], axis=0)
    cos = np.tile(cos, (1, 4))
    sin = np.tile(np.concatenate([-sin, sin], axis=1), (1, 2))
    return jnp.asarray(cos, F32), jnp.asarray(sin, F32)


def _rpb_body(r_ref, e_ref, o_ref):
    o_ref[...] = _dot_x(r_ref[...] * math.log2(math.e), e_ref[...])


def rpb_tables(rpb):
    depth = rpb.shape[0]
    nr, nc = 2 * NA_ROWS - 1, 2 * NA_COLS - 1
    q = np.arange(GRID_W)
    dc = np.clip(q[None, :] - q[:, None] + NA_COLS - 1, 0, nc - 1)
    onehot = (np.arange(32)[:, None, None] == dc[None]).astype(np.float32).reshape(32, GRID_W * GRID_W)
    r2 = jnp.pad(rpb.reshape(depth * C_HEADS * nr, nc), ((0, 0), (0, 32 - nc)))
    rows = r2.shape[0]
    toe = pl.pallas_call(
        _rpb_body,
        grid=(1,),
        in_specs=[pl.BlockSpec((rows, 32), lambda i: (0, 0)), pl.BlockSpec((32, GRID_W * GRID_W), lambda i: (0, 0))],
        out_specs=pl.BlockSpec((rows, GRID_W * GRID_W), lambda i: (0, 0)),
        out_shape=jax.ShapeDtypeStruct((rows, GRID_W * GRID_W), F32),
        compiler_params=_cp("arbitrary"),
        name="rpb_tables",
    )(r2, jnp.asarray(onehot, BF16))
    toe = toe.reshape(depth, C_HEADS, nr, GRID_W, GRID_W)
    rows = SEQ // GRID_W
    dr = np.zeros((3, NA_QROWS, NA_UNION), np.int32)
    ok_row = np.zeros((3, NA_QROWS, NA_UNION), bool)
    for ty, g in enumerate((0, 1, rows // NA_QROWS - 1)):
        us = _na_union_start(g, rows)
        for i in range(NA_QROWS):
            r = NA_QROWS * g + i
            rs = min(max(r - NA_ROWS // 2, 0), rows - NA_ROWS)
            for j in range(NA_UNION):
                ok_row[ty, i, j] = rs <= us + j < rs + NA_ROWS
                dr[ty, i, j] = min(max(us + j - r + NA_ROWS - 1, 0), nr - 1)
    cstart = np.clip(q - NA_COLS // 2, 0, GRID_W - NA_COLS)
    ok_col = (q[None, :] >= cstart[:, None]) & (q[None, :] < cstart[:, None] + NA_COLS)
    ok = ok_row[:, :, :, None, None] & ok_col[None, None, None]
    tab = toe[:, :, jnp.asarray(dr)]
    tab = jnp.where(jnp.asarray(ok), tab, -jnp.inf)
    tab = jnp.transpose(tab, (0, 2, 1, 3, 5, 4, 6))
    return tab.reshape(depth, 3, C_HEADS, NA_QROWS * GRID_W, NA_UNION * GRID_W)


NA_QROWS = 4
NA_UNION = NA_QROWS + NA_ROWS - 1


def _na_union_start(g, rows):
    lo = NA_QROWS * g - NA_ROWS // 2
    hi = rows - NA_UNION
    if isinstance(g, int):
        return min(max(lo, 0), hi)
    return jnp.clip(lo, 0, hi)


def _na_body(q_ref, k_ref, v_ref, bias_ref, qn_ref, kn_ref, seg_ref, o_ref, kh_ref, vh_ref):
    step = pl.program_id(1)
    seg = seg_ref[...]

    @pl.when(step == 0)
    def _():
        kh_ref[...] = _head_rms(k_ref[...], kn_ref[...], seg).astype(BF16)
        vh_ref[...] = v_ref[...].astype(BF16)

    nq = NA_QROWS * GRID_W
    q = _head_rms(q_ref[...], qn_ref[...], seg) * (HEAD ** -0.5 * math.log2(math.e))
    lane = _iota((nq, 128), 1)
    low = lane < HEAD
    sls = [slice((h // 2) * 128, (h // 2 + 1) * 128) for h in range(C_HEADS)]

    def masked_q(h):
        return jnp.where(low if h % 2 == 0 else jnp.logical_not(low), q[:, sls[h]], 0.0).astype(BF16)

    def store(hs, outs):
        for j in range(0, len(hs), 2):
            o_ref[:, sls[hs[j]]] = jnp.where(low, outs[j], outs[j + 1]).astype(o_ref.dtype)

    head_groups = [list(range(g, g + 4)) for g in range(0, C_HEADS, 4)]

    @pl.when(step == 0)
    def _():
        for hs in head_groups:
            qs = [masked_q(h) for h in hs]
            s = [_dot_nt(qs[j], kh_ref[0:CTX, sls[h]]) for j, h in enumerate(hs)]
            e = [jnp.exp2(x - jnp.max(x, axis=-1, keepdims=True)) for x in s]
            store(hs, [_dot(e[j].astype(BF16), vh_ref[0:CTX, sls[h]]) / jnp.sum(e[j], axis=-1, keepdims=True)
                       for j, h in enumerate(hs)])

    @pl.when(step > 0)
    def _():
        us = _na_union_start(step - 1, SEQ // GRID_W)
        k0 = pl.multiple_of(CTX + us * GRID_W, GRID_W)
        nloc = NA_UNION * GRID_W
        for hs in head_groups:
            qs = [masked_q(h) for h in hs]
            s_loc = [_dot_nt(qs[j], kh_ref[pl.ds(k0, nloc), sls[h]]) + bias_ref[0, h] for j, h in enumerate(hs)]
            s_ctx = [_dot_nt(qs[j], kh_ref[0:CTX, sls[h]]) for j, h in enumerate(hs)]
            mx = [jnp.maximum(jnp.max(a, axis=-1, keepdims=True), jnp.max(b, axis=-1, keepdims=True))
                  for a, b in zip(s_loc, s_ctx)]
            e_loc = [jnp.exp2(a - m_) for a, m_ in zip(s_loc, mx)]
            e_ctx = [jnp.exp2(b - m_) for b, m_ in zip(s_ctx, mx)]
            den = [jnp.sum(a, axis=-1, keepdims=True) + jnp.sum(b, axis=-1, keepdims=True)
                   for a, b in zip(e_loc, e_ctx)]
            store(hs, [(_dot(e_loc[j].astype(BF16), vh_ref[pl.ds(k0, nloc), sls[h]])
                        + _dot(e_ctx[j].astype(BF16), vh_ref[0:CTX, sls[h]])) / den[j] for j, h in enumerate(hs)])


def na_attention(z, bias, qn, kn, layer):
    t = z.shape[0]
    nb = t // ROWS
    nq = NA_QROWS * GRID_W
    assert nq == CTX
    steps = ROWS // nq
    groups = steps - 1

    def layout(i):
        g = jnp.maximum(i - 1, 0)
        return jnp.where(g == 0, 0, jnp.where(g == groups - 1, 2, 1))

    full = lambda shape: pl.BlockSpec(shape, lambda b, i: (0,) * len(shape))
    return pl.pallas_call(
        _na_body,
        grid=(nb, steps),
        in_specs=[pl.BlockSpec((nq, BLK), lambda b, i: (b * steps + i, 7)),
                  pl.BlockSpec((ROWS, BLK), lambda b, i: (b, 8)),
                  pl.BlockSpec((ROWS, BLK), lambda b, i: (b, 9)),
                  pl.BlockSpec((None, 1, C_HEADS, nq, NA_UNION * GRID_W), lambda b, i: (layer, layout(i), 0, 0, 0)),
                  full((1, BLK)), full((1, BLK)), full((BLK, BLK))],
        out_specs=pl.BlockSpec((nq, BLK), lambda b, i: (b * steps + i, 0)),
        out_shape=jax.ShapeDtypeStruct((t, BLK), BF16),
        scratch_shapes=[pltpu.VMEM((ROWS, BLK), BF16), pltpu.VMEM((ROWS, BLK), BF16)],
        compiler_params=_cp("arbitrary", "arbitrary"),
        name="na_attention",
    )(z, z, z, bias, qn, kn, _seg_ones(512, HEAD))


def _lru_body(x_ref, xp_ref, xn_ref, gate_ref, cw_ref, cb_ref, wa_ref, ba_ref, wx_ref, bx_ref, lam_ref,
              *rest, reverse, last, tpb):
    if last:
        h0_ref, o_ref, carry_ref = rest
    else:
        o_ref, carry_ref = rest
    p = pl.program_id(0) % tpb
    has_prev, has_next = _tile_flags(_lru_tile(p, tpb, reverse), tpb)

    @pl.when(p == 0)
    def _():
        carry_ref[...] = jnp.zeros_like(carry_ref)

    x = x_ref[...]
    n = x.shape[0]
    xm1 = _shift_rows(x, xp_ref[7:8, :], 1, has_prev)
    xp1 = _shift_rows(x, xn_ref[0:1, :], -1, has_next)
    rows = _iota((n, 1), 0)
    nx = jnp.where(has_next, xn_ref[0:2, :], 0.0)
    xp2 = pltpu.roll(x, n - 2, 0)
    xp2 = jnp.where(rows == n - 2, nx[0:1, :], jnp.where(rows == n - 1, nx[1:2, :], xp2))
    xr = (cw_ref[0:1, :] * xm1 + cw_ref[1:2, :] * x + cw_ref[2:3, :] * xp1 + cw_ref[3:4, :] * xp2) + cb_ref[...]
    xb = xr.astype(BF16)
    rg = _sigmoid(_dot(xb, wa_ref[...]) + ba_ref[...])
    ig = _sigmoid(_dot(xb, wx_ref[...]) + bx_ref[...])
    log_a = -RG_C * _softplus(-lam_ref[...]) * rg
    a = jnp.exp(log_a)
    b = jnp.sqrt(jnp.maximum(1.0 - jnp.exp(2.0 * log_a), 0.0)) * ig * xr
    sub = rows % 8
    k = 1
    while k < 8:
        if reverse:
            keep = sub < 8 - k
            a_s = pltpu.roll(a, n - k, 0)
            b_s = pltpu.roll(b, n - k, 0)
        else:
            keep = sub >= k
            a_s = pltpu.roll(a, k, 0)
            b_s = pltpu.roll(b, k, 0)
        b = jnp.where(keep, a * b_s + b, b)
        a = jnp.where(keep, a * a_s, a)
        k *= 2
    carry = carry_ref[...]
    order = range(n // 8 - 1, -1, -1) if reverse else range(n // 8)
    hs = [None] * (n // 8)
    for gi in order:
        hg = a[8 * gi:8 * gi + 8, :] * carry + b[8 * gi:8 * gi + 8, :]
        carry = hg[0:1, :] if reverse else hg[7:8, :]
        hs[gi] = hg
    h = jnp.concatenate(hs, axis=0)
    carry_ref[...] = carry
    if last:
        g = gate_ref[...]
        gelu = 0.5 * g * (1.0 + jnp.tanh(math.sqrt(2.0 / math.pi) * (g + 0.044715 * g * g * g)))
        o_ref[...] = ((h0_ref[...] + h) * gelu).astype(o_ref.dtype)
    else:
        o_ref[...] = h


def _lru_tile(p, tpb, reverse):
    cctx = CTX // TQ
    if reverse:
        return jnp.where(p < cctx, cctx - 1 - p, tpb - 1 + cctx - p)
    return p


def rglru(z, direction, conv_w, conv_b, wa_bd, ba, wx_bd, bx, lam, h0=None):
    t = z.shape[0]
    n_tiles = t // TQ
    tpb = ROWS // TQ
    reverse = direction == 1
    last = h0 is not None

    def blk(i):
        return (i // tpb) * tpb + _lru_tile(i % tpb, tpb, reverse)

    r8 = TQ // 8
    full = lambda shape: pl.BlockSpec(shape, lambda i: (0,) * len(shape))
    in_specs = [pl.BlockSpec((TQ, BLK), lambda i: (blk(i), 10)),
                pl.BlockSpec((8, BLK), lambda i: (jnp.maximum(blk(i) * r8 - 1, 0), 10)),
                pl.BlockSpec((8, BLK), lambda i: (jnp.minimum((blk(i) + 1) * r8, n_tiles * r8 - 1), 10)),
                pl.BlockSpec((TQ, BLK), lambda i: (blk(i), 11)),
                full((D_CONV, BLK)), full((1, BLK)), full((BLK, BLK)), full((1, BLK)),
                full((BLK, BLK)), full((1, BLK)), full((1, BLK))]
    args = [z, z, z, z, conv_w, conv_b, wa_bd, ba, wx_bd, bx, lam]
    if last:
        in_specs.append(pl.BlockSpec((TQ, BLK), lambda i: (blk(i), 0)))
        args.append(h0)

    return pl.pallas_call(
        functools.partial(_lru_body, reverse=reverse, last=last, tpb=tpb),
        grid=(n_tiles,),
        in_specs=in_specs,
        out_specs=pl.BlockSpec((TQ, BLK), lambda i: (blk(i), 0)),
        out_shape=jax.ShapeDtypeStruct((t, BLK), BF16 if last else F32),
        scratch_shapes=[pltpu.VMEM((1, BLK), F32)],
        compiler_params=_cp("arbitrary"),
        name="rglru_%d" % direction,
    )(*args)


def _merge_body(hn_ref, wg_ref, ba_ref, bb_ref, bc_ref, bd_ref, wb_ref, y_ref, *, tn):
    g = _dot(hn_ref[...], wg_ref[...])
    acc = None
    for j, br in enumerate((ba_ref, bb_ref, bc_ref, bd_ref)):
        term = _sigmoid(g[:, j * tn:(j + 1) * tn]) * _dot(br[...].astype(BF16), wb_ref[j])
        acc = term if acc is None else acc + term
    y_ref[...] = acc.astype(BF16)


def merge_branches(hn, wg, branches, wb, layer, *, tm=1152, tn=256):
    t, d = hn.shape
    br_spec = pl.BlockSpec((tm, BLK), lambda i, j: (i, 0))
    return pl.pallas_call(
        functools.partial(_merge_body, tn=tn),
        grid=(t // tm, d // tn),
        in_specs=[pl.BlockSpec((tm, d), lambda i, j: (i, 0)),
                  pl.BlockSpec((None, None, d, N_BRANCH * tn), lambda i, j: (layer, j, 0, 0)),
                  br_spec, br_spec, br_spec, br_spec,
                  pl.BlockSpec((None, N_BRANCH, BLK, tn), lambda i, j: (layer, 0, 0, j))],
        out_specs=pl.BlockSpec((tm, tn), lambda i, j: (i, j)),
        out_shape=jax.ShapeDtypeStruct((t, d), BF16),
        compiler_params=_cp("arbitrary", "arbitrary"),
        name="merge_branches",
    )(hn, wg, *branches, wb)


def _post_body(y_ref, wo_ref, x_ref, mod_ref, g2_ref, rw_ref, rb_ref,
               xo_ref, h2_ref, e_ref, w_ref, rank_ref, cnt_ref, base_ref, *, tm, tpb, slab):
    i = pl.program_id(0)
    row0 = (i % tpb) * tm

    @pl.when(i == 0)
    def _():
        base_ref[...] = jnp.zeros_like(base_ref)

    slabs = [slice(r, r + slab) for r in range(0, tm, slab)]
    attn = [_dot(y_ref[sl, :], wo_ref[...]) for sl in slabs]
    logit_parts = []
    for sl, a in zip(slabs, attn):
        x = x_ref[sl, :] + _mod_rows(mod_ref, 2, row0 + sl.start, slab) * a
        xo_ref[sl, :] = x
        h = _norm_mod(x, g2_ref[...], mod_ref, 3, row0 + sl.start, slab)
        hh, hl = _split(h)
        half = h.shape[1] // 2
        lo = pltpu.bitcast(hh[:, :half].astype(F32), jnp.uint32) >> 16
        hi = pltpu.bitcast(hh[:, half:].astype(F32), jnp.uint32) & jnp.uint32(0xFFFF0000)
        h2_ref[sl, :] = lo | hi
        logit_parts.append(_dot_nt(rw_ref[0], hh) + _dot_nt(rw_ref[0], hl) + _dot_nt(rw_ref[1], hh))
    logits = jnp.concatenate(logit_parts, axis=1)
    s = _sigmoid(logits)
    sb = s + rb_ref[...]
    per = N_EXPERTS // N_GROUPS
    srow = [s[e:e + 1, :] for e in range(N_EXPERTS)]
    brow = [sb[e:e + 1, :] for e in range(N_EXPERTS)]
    gscore = []
    for g in range(N_GROUPS):
        best = None
        for a in range(per):
            for b in range(a + 1, per):
                pair = brow[per * g + a] + brow[per * g + b]
                best = pair if best is None else jnp.maximum(best, pair)
        gscore.append(best)
    gbest = jnp.zeros((1, tm), jnp.int32)
    bscore = gscore[0]
    for g in range(1, N_GROUPS):
        better = gscore[g] > bscore
        gbest = jnp.where(better, g, gbest)
        bscore = jnp.where(better, gscore[g], bscore)

    def pick(rows, i):
        out = rows[i]
        for g in range(1, N_GROUPS):
            out = jnp.where(gbest == g, rows[per * g + i], out)
        return out

    cand = [pick(brow, i) for i in range(per)]
    cval = [pick(srow, i) for i in range(per)]
    i1 = jnp.zeros((1, tm), jnp.int32)
    m1 = cand[0]
    w1 = cval[0]
    for i_ in range(1, per):
        better = cand[i_] > m1
        i1 = jnp.where(better, i_, i1)
        m1 = jnp.where(better, cand[i_], m1)
        w1 = jnp.where(better, cval[i_], w1)
    i2 = jnp.full((1, tm), -1, jnp.int32)
    m2 = jnp.full((1, tm), -jnp.inf, F32)
    w2 = jnp.zeros((1, tm), F32)
    for i_ in range(per):
        better = jnp.logical_and(i1 != i_, cand[i_] > m2)
        i2 = jnp.where(better, i_, i2)
        m2 = jnp.where(better, cand[i_], m2)
        w2 = jnp.where(better, cval[i_], w2)
    e1 = per * gbest + i1
    e2 = per * gbest + i2
    wsum = w1 + w2
    e_ref[0:1, :] = e1
    e_ref[1:2, :] = e2
    w_ref[0:1, :] = w1 / wsum
    w_ref[1:2, :] = w2 / wsum
    er = _iota((N_EXPERTS, tm), 0)
    oh1 = jnp.where(er == e1, 1.0, 0.0)
    oh2 = jnp.where(er == e2, 1.0, 0.0)
    upper = jnp.where(_iota((tm, tm), 0) < _iota((tm, tm), 1), 1.0, 0.0).astype(BF16)
    ex1 = _dot(oh1.astype(BF16), upper)
    ex2 = _dot(oh2.astype(BF16), upper)
    tot1 = jnp.sum(oh1, axis=1, keepdims=True)
    tot2 = jnp.sum(oh2, axis=1, keepdims=True)
    base = base_ref[...]
    rank_ref[0:1, :] = jnp.sum(oh1 * (base + ex1), axis=0, keepdims=True).astype(jnp.int32)
    rank_ref[1:2, :] = jnp.sum(oh2 * (base + tot1 + ex2), axis=0, keepdims=True).astype(jnp.int32)
    base = base + tot1 + tot2
    base_ref[...] = base
    cnt_ref[...] = jnp.broadcast_to(base, cnt_ref.shape)


def post_attention(y, w_out, x, mod, g2, rw, rb, layer, *, tm=384):
    t, d = x.shape
    tpb = ROWS // tm
    full = lambda shape: pl.BlockSpec(shape, lambda i: (0,) * len(shape))
    row = pl.BlockSpec((tm, d), lambda i: (i, 0))
    tok = pl.BlockSpec((2, tm), lambda i: (0, i))
    return pl.pallas_call(
        functools.partial(_post_body, tm=tm, tpb=tpb, slab=128),
        grid=(t // tm,),
        in_specs=[row, pl.BlockSpec((None, d, d), lambda i: (layer, 0, 0)), row,
                  pl.BlockSpec((1, 2, 6, d), lambda i: (i // tpb, 0, 0, 0)),
                  full((1, d)), full((2, N_EXPERTS, d)), full((N_EXPERTS, 1))],
        out_specs=[row, pl.BlockSpec((tm, d // 2), lambda i: (i, 0)), tok, tok, tok, full((N_EXPERTS, 128))],
        out_shape=[jax.ShapeDtypeStruct((t, d), F32), jax.ShapeDtypeStruct((t, d // 2), jnp.uint32),
                   jax.ShapeDtypeStruct((2, t), jnp.int32), jax.ShapeDtypeStruct((2, t), F32),
                   jax.ShapeDtypeStruct((2, t), jnp.int32), jax.ShapeDtypeStruct((N_EXPERTS, 128), F32)],
        scratch_shapes=[pltpu.VMEM((N_EXPERTS, 1), F32)],
        compiler_params=_cp("arbitrary"),
        name="post_attention",
    )(y, w_out, x, mod, g2, rw, rb)


def _row_copy(src, dst, sem):
    return pltpu.make_async_copy(src, dst, sem)


def _dispatch_body(pos_ref, h_ref, xs_in_ref, xs_ref, sem, *, tm):
    del xs_in_ref

    def start(t, carry):
        for k in range(2):
            p = pos_ref[0, 0, k * tm + t]
            _row_copy(h_ref.at[pl.ds(t, 1)], xs_ref.at[pl.ds(p, 1)], sem).start(priority=k)
        return carry

    lax.fori_loop(0, tm, start, 0, unroll=8)
    for k in range(2):
        _row_copy(h_ref, xs_ref.at[pl.ds(0, tm)], sem).wait()


def moe_dispatch(pos3, h2, n_rows, *, tm=256):
    t, d = h2.shape
    xs0 = jnp.zeros((n_rows, d), h2.dtype)
    return pl.pallas_call(
        functools.partial(_dispatch_body, tm=tm),
        grid=(t // tm,),
        in_specs=[pl.BlockSpec((1, 1, 2 * tm), lambda i: (i, 0, 0), memory_space=pltpu.SMEM),
                  pl.BlockSpec((tm, d), lambda i: (i, 0)),
                  pl.BlockSpec(memory_space=pl.ANY)],
        out_specs=pl.BlockSpec(memory_space=pl.ANY),
        out_shape=jax.ShapeDtypeStruct((n_rows, d), h2.dtype),
        scratch_shapes=[pltpu.SemaphoreType.DMA(())],
        input_output_aliases={2: 0},
        compiler_params=_cp("arbitrary"),
        name="moe_dispatch",
    )(pos3, h2, xs0)


def _ffn_body(te_ref, nv_ref, x_ref, wg_ref, wu_ref, wd_ref, y_ref):
    del te_ref
    i = pl.program_id(0)

    @pl.when(i < nv_ref[0])
    def _():
        p = x_ref[...]
        half = p.shape[1]
        x_lo = pltpu.bitcast(p << 16, F32).astype(BF16)
        x_hi = pltpu.bitcast(p & jnp.uint32(0xFFFF0000), F32).astype(BF16)
        g = _dot(x_lo, wg_ref[0:half, :]) + _dot(x_hi, wg_ref[half:2 * half, :])
        u = _dot(x_lo, wu_ref[0:half, :]) + _dot(x_hi, wu_ref[half:2 * half, :])
        y_ref[...] = _dot((g * _sigmoid(g) * u).astype(BF16), wd_ref[...])

    @pl.when(i >= nv_ref[0])
    def _():
        y_ref[...] = jnp.zeros_like(y_ref)


def moe_ffn(tile_expert, n_valid, xs, wg, wu, wd, layer):
    n_rows, dp = xs.shape
    d, ff = wg.shape[-2:]
    n_tiles = n_rows // FFN_TM
    grid_spec = pltpu.PrefetchScalarGridSpec(
        num_scalar_prefetch=2,
        grid=(n_tiles,),
        in_specs=[pl.BlockSpec((FFN_TM, dp), lambda i, te, nv: (jnp.minimum(i, nv[0] - 1), 0)),
                  pl.BlockSpec((None, None, d, ff), lambda i, te, nv: (layer, te[i], 0, 0)),
                  pl.BlockSpec((None, None, d, ff), lambda i, te, nv: (layer, te[i], 0, 0)),
                  pl.BlockSpec((None, None, ff, d), lambda i, te, nv: (layer, te[i], 0, 0))],
        out_specs=pl.BlockSpec((FFN_TM, d), lambda i, te, nv: (i, 0)))
    return pl.pallas_call(
        _ffn_body,
        grid_spec=grid_spec,
        out_shape=jax.ShapeDtypeStruct((n_rows, d), F32),
        compiler_params=_cp("arbitrary"),
        name="moe_ffn",
    )(tile_expert, n_valid, xs, wg, wu, wd)


def _combine_body(pos_ref, ys_ref, x_ref, w_ref, mod_ref, o_ref, buf_ref, sem, *, tm, tpb):
    def start(t, carry):
        for k in range(2):
            p = pos_ref[0, 0, k * tm + t]
            _row_copy(ys_ref.at[pl.ds(p, 1)], buf_ref.at[k, pl.ds(t, 1)], sem).start(priority=k)
        return carry

    lax.fori_loop(0, tm, start, 0, unroll=8)
    for k in range(2):
        _row_copy(ys_ref.at[pl.ds(0, tm)], buf_ref.at[k], sem).wait()
    w = w_ref[...]
    y = w[:, 0:1] * buf_ref[0] + w[:, 1:2] * buf_ref[1]
    row0 = (pl.program_id(0) % tpb) * tm
    o_ref[...] = x_ref[...] + _mod_rows(mod_ref, 5, row0, tm) * y


def moe_combine(pos3, ys, x, wt, mod, *, tm=256):
    t, d = x.shape
    tpb = ROWS // tm
    row = pl.BlockSpec((tm, d), lambda i: (i, 0))
    return pl.pallas_call(
        functools.partial(_combine_body, tm=tm, tpb=tpb),
        grid=(t // tm,),
        in_specs=[pl.BlockSpec((1, 1, 2 * tm), lambda i: (i, 0, 0), memory_space=pltpu.SMEM),
                  pl.BlockSpec(memory_space=pl.ANY), row,
                  pl.BlockSpec((tm, 2), lambda i: (i, 0)),
                  pl.BlockSpec((1, 2, 6, d), lambda i: (i // tpb, 0, 0, 0))],
        out_specs=row,
        out_shape=jax.ShapeDtypeStruct((t, d), F32),
        scratch_shapes=[pltpu.VMEM((2, tm, d), F32), pltpu.SemaphoreType.DMA(())],
        compiler_params=_cp("arbitrary"),
        name="moe_combine",
    )(pos3, ys, x, wt, mod)


def moe_layout(e, rank, counts, *, tm=256):
    t = e.shape[1]
    n_rows_max = -(-(2 * t + N_EXPERTS * (FFN_TM - 1)) // FFN_TM) * FFN_TM
    n_tiles = n_rows_max // FFN_TM
    cnt = counts[:, 0].astype(jnp.int32)
    gsz = ((cnt + FFN_TM - 1) // FFN_TM) * FFN_TM
    end = jnp.cumsum(gsz)
    off = end - gsz
    onehot = e[:, :, None] == jnp.arange(N_EXPERTS, dtype=jnp.int32)
    pos = rank + jnp.sum(jnp.where(onehot, off, 0), axis=-1)
    pos3 = pos.reshape(2, t // tm, tm).transpose(1, 0, 2).reshape(t // tm, 1, 2 * tm)
    n_valid = (end[-1] // FFN_TM).astype(jnp.int32)
    starts = jnp.minimum(jnp.arange(n_tiles, dtype=jnp.int32), n_valid - 1) * FFN_TM
    tile_expert = jnp.sum(starts[:, None] >= end[None, :], axis=1).astype(jnp.int32)
    return pos3, tile_expert, n_valid.reshape(1), n_rows_max


def _block_diag(w):
    n, bi, bj = w.shape[-3:]
    eye = jnp.eye(n, dtype=w.dtype)
    out = w[..., :, :, None, :] * eye[:, None, :, None]
    return out.reshape(w.shape[:-3] + (n * bi, n * bj))


def kernel(x, c, ctx, c_ctx, w_mod, b_mod, norm1_g, norm2_g, w_in, a_mu, a_w0, a_w2, a_a0, a_a2, a_g2, a_kk, a_ka,
           a_rk, a_gn_g, a_gn_b, b_qn, b_kn, b_lam, b_subln, c_qn, c_kn, c_rpb, d_conv_w, d_conv_b, d_wa, d_ba,
           d_wx, d_bx, d_lam, w_branch, w_out, router_w, router_b, e_gate, e_up, e_down):
    nb = x.shape[0]
    d = D_MODEL
    t = nb * ROWS
    xs = jnp.concatenate([ctx, x], axis=1).reshape(t, d)

    cc = jnp.concatenate([c, c_ctx[None], jnp.zeros((-(nb + 1) % 8, d), F32)], axis=0)
    mods = mod_vectors(cc, w_mod, b_mod)
    mod_l = mods[:, :nb].reshape(DEPTH, nb, 1, 6, d)
    mod_c = jnp.broadcast_to(mods[:, nb].reshape(DEPTH, 1, 1, 6, d), (DEPTH, nb, 1, 6, d))
    mod = jnp.concatenate([mod_l, mod_c], axis=2)

    o_b, o_c, o_d, o_g = A_COLS, A_COLS + 1536, A_COLS + 3072, A_COLS + 3072 + 1024
    w_mix = jnp.concatenate([w_in[:, :, :o_b], jnp.zeros((DEPTH, d, 2048 - A_COLS), F32), w_in[:, :, o_b:o_g]],
                            axis=-1).astype(BF16)
    tn_m = 256
    w_gate = w_in[:, :, o_g:].astype(BF16).reshape(DEPTH, d, N_BRANCH, d // tn_m, tn_m)
    w_gate = jnp.transpose(w_gate, (0, 3, 1, 2, 4)).reshape(DEPTH, d // tn_m, d, N_BRANCH * tn_m)
    mu = jnp.pad(a_mu, ((0, 0), (0, 0), (0, 2048 - A_COLS)))
    w2bd = _block_diag(a_w2).astype(BF16)
    a2bd = _block_diag(a_a2).astype(BF16)
    g2 = a_g2.astype(BF16)
    wa_bd = _block_diag(d_wa).astype(BF16)
    wx_bd = _block_diag(d_wx).astype(BF16)
    wb = w_branch.astype(BF16)
    wo = w_out.astype(BF16)
    rw = jnp.stack(_split(router_w.T), axis=0)
    rb = router_b.reshape(N_EXPERTS, 1)
    eg, eu, ed = e_gate.astype(BF16), e_up.astype(BF16), e_down.astype(BF16)
    cos, sin = rope_tables()
    bias = rpb_tables(c_rpb)
    qn_c = jnp.tile(c_qn, (1, C_HEADS)).reshape(DEPTH, 1, BLK)
    kn_c = jnp.tile(c_kn, (1, C_HEADS)).reshape(DEPTH, 1, BLK)

    for l in range(DEPTH):
        lam_init = 0.8 - 0.6 * math.exp(-0.3 * l)
        z, hn = in_projection(xs, mod[l], norm1_g[l].reshape(1, d), w_mix, l)
        prep = rwkv_prepare(z, mu[l], w2bd[l], a2bd[l], g2[l], a_w0[l].reshape(1, 1024), a_a0[l].reshape(1, 1024),
                            a_kk[l].reshape(1, BLK), a_ka[l].reshape(1, BLK), a_rk[l].reshape(1, BLK))
        ya = rwkv_scan(prep, 0)
        oa = rwkv_scan(prep, 1, ya, a_gn_g[l].reshape(1, BLK), a_gn_b[l].reshape(1, BLK))
        ob = diff_attention(z, cos, sin, b_qn[l].reshape(1, 128), b_kn[l].reshape(1, 128), b_lam[l],
                            b_subln[l].reshape(1, 128), lam_init)
        oc = na_attention(z, bias, qn_c[l], kn_c[l], l)
        lru = [(d_conv_w[l], d_conv_b[l].reshape(1, BLK), wa_bd[l, k], d_ba[l, k].reshape(1, BLK), wx_bd[l, k],
                d_bx[l, k].reshape(1, BLK), d_lam[l, k].reshape(1, BLK)) for k in range(2)]
        hd = rglru(z, 0, *lru[0])
        od = rglru(z, 1, *lru[1], h0=hd)
        y = merge_branches(hn, w_gate, (oa, ob, oc, od), wb, l, tn=tn_m)
        xs, h2, e, wts, rank, counts = post_attention(y, wo, xs, mod[l], norm2_g[l].reshape(1, d), rw, rb, l)
        pos3, tile_expert, n_valid, n_rows = moe_layout(e, rank, counts)
        xg = moe_dispatch(pos3, h2, n_rows)
        yg = moe_ffn(tile_expert, n_valid, xg, eg, eu, ed, l)
        xs = moe_combine(pos3, yg, xs, wts.T, mod[l])
    return xs.reshape(nb, ROWS, d)[:, CTX:]
```

```python
import functools
import math

import numpy as np
import jax
import jax.numpy as jnp
from jax import lax
from jax.experimental import pallas as pl
from jax.experimental.pallas import tpu as pltpu

F32 = jnp.float32
BF16 = jnp.bfloat16

D_MODEL = 2048
DEPTH = 4
GRID_W = 64
CTX = 256
SEQ = 2048
ROWS = CTX + SEQ
HEAD = 64
A_WIDTH = 512
A_COLS = 1920
A_GN_EPS = 64e-5
B_HEADS = 4
C_HEADS = 8
NA_ROWS = 8
NA_COLS = 16
D_WIDTH = 512
D_CONV = 4
RG_C = 8.0
N_BRANCH = 4
N_EXPERTS = 16
N_GROUPS = 4
D_FF = 1024
ROPE_BASE = 10000.0
NORM_EPS = 1e-6
Z_COLS = 6144
BLK = 512

VMEM_LIMIT = 56 * 1024 * 1024

CHUNK = 64
TQ = 256
FFN_TM = 512


def _cp(*sem):
    return pltpu.CompilerParams(dimension_semantics=sem, vmem_limit_bytes=VMEM_LIMIT)


def _dot(a, b):
    return jnp.dot(a, b, preferred_element_type=F32)


def _dot_nt(a, b):
    return lax.dot_general(a, b, (((1,), (1,)), ((), ())), preferred_element_type=F32)


def _dot_tn(a, b):
    return lax.dot_general(a, b, (((0,), (0,)), ((), ())), preferred_element_type=F32)


def _split(x):
    hi = x.astype(BF16)
    lo = (x - hi.astype(F32)).astype(BF16)
    return hi, lo


def _dot_x(x, m):
    hi, lo = _split(x)
    return _dot(hi, m) + _dot(lo, m)


def _sigmoid(x):
    return 1.0 / (1.0 + jnp.exp(-x))


def _softplus(x):
    return jnp.maximum(x, 0.0) + jnp.log(1.0 + jnp.exp(-jnp.abs(x)))


def _iota(shape, dim):
    return lax.broadcasted_iota(jnp.int32, shape, dim)


def _seg_ones(width, seg):
    r = np.arange(width)
    return jnp.asarray((r[:, None] // seg) == (r[None, :] // seg), dtype=BF16)


def _mod_body(c_ref, w_ref, b_ref, o_ref):
    c = c_ref[...]
    s = (c * _sigmoid(c)).astype(BF16)
    o_ref[...] = _dot(s, w_ref[...].astype(BF16)) + b_ref[...]


def mod_vectors(cc, w_mod, b_mod):
    depth, d, n = w_mod.shape
    r = cc.shape[0]
    tn = 1536
    return pl.pallas_call(
        _mod_body,
        grid=(depth, n // tn),
        in_specs=[pl.BlockSpec((r, d), lambda l, j: (0, 0)),
                  pl.BlockSpec((None, d, tn), lambda l, j: (l, 0, j)),
                  pl.BlockSpec((None, 1, tn), lambda l, j: (l, 0, j))],
        out_specs=pl.BlockSpec((None, r, tn), lambda l, j: (l, 0, j)),
        out_shape=jax.ShapeDtypeStruct((depth, r, n), F32),
        compiler_params=_cp("arbitrary", "arbitrary"),
        name="mod_vectors",
    )(cc, w_mod, b_mod.reshape(depth, 1, n))


def _mod_rows(mod_ref, idx, row0, tm):
    rows = row0 + _iota((tm, 1), 0)
    return jnp.where(rows < CTX, mod_ref[0, 1, idx:idx + 1, :], mod_ref[0, 0, idx:idx + 1, :])


def _norm_mod(x, g, mod_ref, shift_idx, row0, tm):
    ms = jnp.mean(x * x, axis=-1, keepdims=True)
    y = x * lax.rsqrt(ms + NORM_EPS) * g
    return y * (1.0 + _mod_rows(mod_ref, shift_idx + 1, row0, tm)) + _mod_rows(mod_ref, shift_idx, row0, tm)


def _inproj_body(x_ref, mod_ref, g_ref, w_ref, z_ref, hn_ref, *, tm, tpb, slab):
    i = pl.program_id(0)

    @pl.when(pl.program_id(1) == 0)
    def _():
        for r in range(0, tm, slab):
            row0 = (i % tpb) * tm + r
            hn = _norm_mod(x_ref[r:r + slab, :], g_ref[...], mod_ref, 0, row0, slab).astype(BF16)
            hn_ref[r:r + slab, :] = hn
            z_ref[r:r + slab, :] = _dot(hn, w_ref[...])

    @pl.when(pl.program_id(1) != 0)
    def _():
        z_ref[...] = _dot(hn_ref[...], w_ref[...])


def in_projection(x, mod, g, w, layer, *, tm=1152, tn=768):
    t, d = x.shape
    n = w.shape[2]
    tpb = ROWS // tm
    return pl.pallas_call(
        functools.partial(_inproj_body, tm=tm, tpb=tpb, slab=384),
        grid=(t // tm, n // tn),
        in_specs=[pl.BlockSpec((tm, d), lambda i, j: (i, 0)),
                  pl.BlockSpec((1, 2, 6, d), lambda i, j: (i // tpb, 0, 0, 0)),
                  pl.BlockSpec((1, d), lambda i, j: (0, 0)),
                  pl.BlockSpec((None, d, tn), lambda i, j: (layer, 0, j))],
        out_specs=[pl.BlockSpec((tm, tn), lambda i, j: (i, j)),
                   pl.BlockSpec((tm, d), lambda i, j: (i, 0))],
        out_shape=[jax.ShapeDtypeStruct((t, n), F32), jax.ShapeDtypeStruct((t, d), BF16)],
        compiler_params=_cp("arbitrary", "arbitrary"),
        name="in_projection",
    )(x, mod, g, w)


def _tile_flags(ti, tpb):
    ctx_tiles = CTX // TQ
    has_prev = jnp.logical_and(ti != 0, ti != ctx_tiles)
    has_next = jnp.logical_and(ti != ctx_tiles - 1, ti != tpb - 1)
    return has_prev, has_next


def _halo_specs(width, col, n_tiles):
    r8 = TQ // 8
    return [pl.BlockSpec((TQ, width), lambda i: (i, col)),
            pl.BlockSpec((8, width), lambda i: (jnp.maximum(i * r8 - 1, 0), col)),
            pl.BlockSpec((8, width), lambda i: (jnp.minimum((i + 1) * r8, n_tiles * r8 - 1), col))]


def _shift_rows(z, halo_row, k, use_halo):
    n = z.shape[0]
    rows = _iota((n, 1), 0)
    fill = jnp.where(use_halo, halo_row, 0.0)
    if k > 0:
        return jnp.where(rows < k, fill, pltpu.roll(z, k, 0))
    return jnp.where(rows >= n + k, fill, pltpu.roll(z, n + k, 0))


def _rwkv_prep_body(z_ref, zp_ref, zn_ref, mu_ref, w2_ref, a2_ref, g2_ref, w0_ref, a0_ref,
                    kk_ref, ka_ref, rk_ref, seg_ref, o_ref, *, tpb):
    has_prev, has_next = _tile_flags(pl.program_id(0) % tpb, tpb)
    z = z_ref[...]
    prev = _shift_rows(z, zp_ref[7:8, :], 1, has_prev)
    nxt = _shift_rows(z, zn_ref[0:1, :], -1, has_next)
    zs = z + mu_ref[0:1, :] * (prev - z) + mu_ref[1:2, :] * (nxt - z)
    r = zs[:, 0:512]
    k = zs[:, 512:1024]
    v = zs[:, 1024:1536]
    wl = zs[:, 1536:1664]
    al = zs[:, 1664:1792]
    gl = zs[:, 1792:1920]
    seg = seg_ref[...]
    wlog = -_softplus(-(w0_ref[...] + _dot(jnp.tanh(wl).astype(BF16), w2_ref[...]))) - 0.5
    lw = -jnp.exp(wlog)
    a = _sigmoid(a0_ref[...] + _dot(al.astype(BF16), a2_ref[...]))
    g = _dot(_sigmoid(gl).astype(BF16), g2_ref[...])
    kk = k * kk_ref[...]
    kk = kk / jnp.maximum(jnp.sqrt(_dot_x(kk * kk, seg)), 1e-12)
    bonus = _dot_x(r * k * rk_ref[...], seg) * v
    ka = ka_ref[...]
    o_ref[:, 0:512] = r
    o_ref[:, 512:1024] = v
    o_ref[:, 1024:1536] = kk
    o_ref[:, 1536:2048] = bonus
    o_ref[:, 2048:2560] = g
    for d in range(2):
        ad = a[:, d * 512:(d + 1) * 512]
        base = 2560 + d * 1536
        o_ref[:, base:base + 512] = lw[:, d * 512:(d + 1) * 512]
        o_ref[:, base + 512:base + 1024] = kk * ad
        o_ref[:, base + 1024:base + 1536] = k * (1.0 + (ad - 1.0) * ka)


A_PREP_COLS = 2560 + 2 * 1536


def rwkv_prepare(z, mu, w2bd, a2bd, g2, w0, a0, k_k, k_a, r_k):
    t = z.shape[0]
    n_tiles = t // TQ
    tpb = ROWS // TQ
    full = lambda shape: pl.BlockSpec(shape, lambda i: (0,) * len(shape))
    return pl.pallas_call(
        functools.partial(_rwkv_prep_body, tpb=tpb),
        grid=(n_tiles,),
        in_specs=_halo_specs(2048, 0, n_tiles) + [
            full((2, 2048)), full((128, 1024)), full((128, 1024)), full((128, 512)),
            full((1, 1024)), full((1, 1024)), full((1, 512)), full((1, 512)), full((1, 512)),
            full((512, 512))],
        out_specs=pl.BlockSpec((TQ, A_PREP_COLS), lambda i: (i, 0)),
        out_shape=jax.ShapeDtypeStruct((t, A_PREP_COLS), F32),
        compiler_params=_cp("arbitrary"),
        name="rwkv_prepare",
    )(z, z, z, mu, w2bd, a2bd, g2, w0, a0, k_k, k_a, r_k, _seg_ones(512, HEAD))


def _rwkv_scan_body(r_ref, v_ref, kk_ref, lw_ref, b_ref, ke_ref, *rest, reverse, last, nbb):
    if last:
        y0_ref, bonus_ref, g_ref, gng_ref, gnb_ref, seg_ref, o_ref, s_ref = rest
    else:
        o_ref, s_ref = rest
    c = CHUNK
    nh = A_WIDTH // HEAD

    @pl.when(pl.program_id(1) == 0)
    def _():
        s_ref[...] = jnp.zeros_like(s_ref)

    ti = _iota((c, c), 0)
    si = _iota((c, c), 1)
    before = (si > ti) if reverse else (si < ti)
    upto = jnp.logical_or(before, si == ti)
    tri = jnp.where(upto, 1.0, 0.0).astype(BF16)
    eye = jnp.where(si == ti, 1.0, 0.0)
    pair = (ti >> 1) == (si >> 1)
    off_masks = []
    lg = 1
    while (2 << lg) <= c:
        off_masks.append(jnp.logical_and((ti >> (lg + 1)) == (si >> (lg + 1)), (ti >> lg) != (si >> lg)))
        lg += 1
    ti2 = _iota((c, 2 * c), 0)
    si2 = _iota((c, 2 * c), 1) % c
    upto2 = (si2 >= ti2) if reverse else (si2 <= ti2)
    hi2 = _iota((c, 2 * c), 1) >= c
    before_hi = jnp.logical_and(hi2, (si2 > ti2) if reverse else (si2 < ti2))

    chains = [(bi, h) for bi in range(nbb) for h in range(nh)]
    cols = {}
    gammas = []
    for bi in range(nbb):
        lw = lw_ref[bi]
        lw_hi, lw_lo = _split(lw)
        cs = _dot(tri, lw_hi) + _dot(tri, lw_lo)
        tot = jnp.sum(lw, axis=0, keepdims=True)
        e_neg = jnp.exp(-cs)
        e_hat = jnp.exp(tot - cs)
        gammas.append(jnp.exp(tot))
        b = b_ref[bi]
        ke = ke_ref[bi]
        mats = dict(r=r_ref[bi] * jnp.exp(cs), a=-kk_ref[bi] * jnp.exp(cs - lw), b=b * e_neg, k=ke * e_neg,
                    bh=b * e_hat, kh=ke * e_hat, v=v_ref[bi])
        for name, m in mats.items():
            for h in range(nh):
                cols[name, bi, h] = m[:, h * HEAD:(h + 1) * HEAD].astype(BF16)

    def per_chain(fn):
        return [fn(i, bi, h) for i, (bi, h) in enumerate(chains)]

    prod = per_chain(lambda i, bi, h: _dot_nt(
        jnp.concatenate([cols['a', bi, h], cols['r', bi, h]], axis=0),
        jnp.concatenate([cols['b', bi, h], cols['k', bi, h]], axis=0)))
    lab = per_chain(lambda i, bi, h: jnp.where(before, prod[i][0:c, 0:c], 0.0))
    lak2 = per_chain(lambda i, bi, h: jnp.where(before_hi, prod[i][0:c, :], 0.0).astype(BF16))
    m = per_chain(lambda i, bi, h: jnp.where(upto2, prod[i][c:2 * c, :], 0.0).astype(BF16))
    tinv = per_chain(lambda i, bi, h: eye + jnp.where(pair, lab[i], 0.0))
    for off_mask in off_masks:
        tb = [t.astype(BF16) for t in tinv]
        x = per_chain(lambda i, bi, h: _dot(jnp.where(off_mask, lab[i], 0.0).astype(BF16), tb[i]))
        tinv = per_chain(lambda i, bi, h: tinv[i] + _dot(tb[i], x[i].astype(BF16)))
    vv = per_chain(lambda i, bi, h: jnp.concatenate([cols['v', bi, h], cols['v', bi, h]], axis=0))
    lv = per_chain(lambda i, bi, h: _dot(lak2[i], vv[i]))
    tb = [t.astype(BF16) for t in tinv]
    w1 = per_chain(lambda i, bi, h: _dot(tb[i], cols['a', bi, h]))
    u2 = per_chain(lambda i, bi, h: _dot(tb[i], lv[i].astype(BF16)))
    s0 = per_chain(lambda i, bi, h: s_ref[bi, h])
    s0b = [s.astype(BF16) for s in s0]
    ws = per_chain(lambda i, bi, h: _dot_nt(
        jnp.concatenate([w1[i].astype(BF16), cols['r', bi, h]], axis=0), s0b[i]))
    u = per_chain(lambda i, bi, h: ws[i][0:c] + u2[i])
    uv = per_chain(lambda i, bi, h: jnp.concatenate([u[i].astype(BF16), cols['v', bi, h]], axis=0))
    ys = per_chain(lambda i, bi, h: ws[i][c:2 * c] + _dot(m[i], uv[i]))
    for i, (bi, h) in enumerate(chains):
        bkh = jnp.concatenate([cols['bh', bi, h], cols['kh', bi, h]], axis=0)
        s_ref[bi, h] = s0[i] * gammas[bi][:, h * HEAD:(h + 1) * HEAD] + _dot_tn(uv[i], bkh)
    yb = [jnp.concatenate(ys[bi * nh:(bi + 1) * nh], axis=1) for bi in range(nbb)]
    if last:
        y = jnp.concatenate([yb[bi] + y0_ref[bi] for bi in range(nbb)], axis=0)
        seg = seg_ref[...]
        inv = 1.0 / HEAD
        mean = _dot_x(y, seg) * inv
        yc = y - mean
        var = _dot_x(yc * yc, seg) * inv
        yn = yc * lax.rsqrt(var + A_GN_EPS) * gng_ref[...] + gnb_ref[...]
        for bi in range(nbb):
            o_ref[bi] = ((yn[bi * c:(bi + 1) * c] + bonus_ref[bi]) * g_ref[bi]).astype(o_ref.dtype)
    else:
        for bi in range(nbb):
            o_ref[bi] = yb[bi]


def rwkv_scan(prep, direction, y0=None, gn_g=None, gn_b=None, *, nbb=4):
    t = prep.shape[0]
    nb = t // ROWS
    cpb = ROWS // CHUNK
    cctx = CTX // CHUNK
    reverse = direction == 1
    last = y0 is not None
    prep3 = prep.reshape(nb, ROWS, prep.shape[1])

    def blk(p):
        if reverse:
            return jnp.where(p < cctx, cctx - 1 - p, cpb - 1 + cctx - p)
        return p

    def col(cb):
        return pl.BlockSpec((nbb, CHUNK, BLK), lambda g, p: (g, blk(p), cb))

    base = 5 + 3 * direction
    in_specs = [col(0), col(1), col(2), col(base), col(base + 1), col(base + 2)]
    args = [prep3] * 6
    if last:
        in_specs += [col(0), col(3), col(4),
                     pl.BlockSpec((1, BLK), lambda g, p: (0, 0)), pl.BlockSpec((1, BLK), lambda g, p: (0, 0)),
                     pl.BlockSpec((BLK, BLK), lambda g, p: (0, 0))]
        args += [y0.reshape(nb, ROWS, BLK), prep3, prep3, gn_g, gn_b, _seg_ones(512, HEAD)]
    out = pl.pallas_call(
        functools.partial(_rwkv_scan_body, reverse=reverse, last=last, nbb=nbb),
        grid=(nb // nbb, cpb),
        in_specs=in_specs,
        out_specs=col(0),
        out_shape=jax.ShapeDtypeStruct((nb, ROWS, BLK), BF16 if last else F32),
        scratch_shapes=[pltpu.VMEM((nbb, A_WIDTH // HEAD, HEAD, HEAD), F32)],
        compiler_params=_cp("arbitrary", "arbitrary"),
        name="rwkv_scan_%d" % direction,
    )(*args)
    return out.reshape(t, BLK)


def _rope(x, cos, sin):
    lane = _iota(x.shape, 1)
    partner = jnp.where((lane % HEAD) < HEAD // 2, pltpu.roll(x, 128 - HEAD // 2, 1), pltpu.roll(x, HEAD // 2, 1))
    return x * cos + partner * sin


def _head_rms(x, gain, seg):
    ms = _dot_x(x * x, seg) * (1.0 / HEAD)
    return x * lax.rsqrt(ms + NORM_EPS) * gain


def _diff_body(q_ref, k_ref, v_ref, cos_ref, sin_ref, qn_ref, kn_ref, lam_ref, sub_ref, seg_ref,
               o_ref, kh_ref, vh_ref, *, lam_init, nq):
    iq = pl.program_id(2)
    seg = seg_ref[...]

    @pl.when(iq == 0)
    def _():
        kh = _rope(_head_rms(k_ref[...], kn_ref[...], seg), cos_ref[...], sin_ref[...])
        kh_ref[...] = kh.astype(BF16)
        vh_ref[:, 0:128] = v_ref[...].astype(BF16)
        vh_ref[:, 128:256] = jnp.ones((ROWS, 128), BF16)

    lv = lam_ref[...]
    lam = (jnp.exp(jnp.sum(lv[0:1] * lv[1:2], axis=1, keepdims=True))
           - jnp.exp(jnp.sum(lv[2:3] * lv[3:4], axis=1, keepdims=True)) + lam_init)
    r0 = pl.multiple_of(iq * TQ, TQ)
    q = _rope(_head_rms(q_ref[...], qn_ref[...], seg), cos_ref[pl.ds(r0, TQ), :], sin_ref[pl.ds(r0, TQ), :])
    q = q * (HEAD ** -0.5 * math.log2(math.e))
    lane = _iota(q.shape, 1)
    q1 = jnp.where(lane < HEAD, q, 0.0).astype(BF16)
    q2 = jnp.where(lane >= HEAD, q, 0.0).astype(BF16)

    def attend(kh, vh):
        s = [_dot_nt(qs, kh) for qs in (q1, q2)]
        e = [jnp.exp2(x - jnp.max(x, axis=-1, keepdims=True)) for x in s]
        pv = [_dot(x.astype(BF16), vh) for x in e]
        o = pv[0][:, 0:128] / pv[0][:, 128:256] - lam * (pv[1][:, 0:128] / pv[1][:, 128:256])
        ms = jnp.mean(o * o, axis=-1, keepdims=True)
        o_ref[...] = (o * lax.rsqrt(ms + NORM_EPS) * sub_ref[...] * (1.0 - lam_init)).astype(o_ref.dtype)

    @pl.when(iq == 0)
    def _():
        attend(kh_ref[0:CTX, :], vh_ref[0:CTX, :])

    @pl.when(iq != 0)
    def _():
        attend(kh_ref[...], vh_ref[...])


def diff_attention(z, cos, sin, qn, kn, lam_vecs, subln, lam_init):
    t = z.shape[0]
    nb = t // ROWS
    nq = ROWS // TQ
    cb = BLK // 128
    full = lambda shape: pl.BlockSpec(shape, lambda b, h, i: (0,) * len(shape))
    return pl.pallas_call(
        functools.partial(_diff_body, lam_init=lam_init, nq=nq),
        grid=(nb, B_HEADS, nq),
        in_specs=[pl.BlockSpec((TQ, 128), lambda b, h, i: (b * nq + i, 4 * cb + h)),
                  pl.BlockSpec((ROWS, 128), lambda b, h, i: (b, 5 * cb + h)),
                  pl.BlockSpec((ROWS, 128), lambda b, h, i: (b, 6 * cb + h)),
                  full((ROWS, 128)), full((ROWS, 128)), full((1, 128)), full((1, 128)),
                  full((4, HEAD)), full((1, 128)), full((128, 128))],
        out_specs=pl.BlockSpec((TQ, 128), lambda b, h, i: (b * nq + i, h)),
        out_shape=jax.ShapeDtypeStruct((t, BLK), BF16),
        scratch_shapes=[pltpu.VMEM((ROWS, 128), BF16), pltpu.VMEM((ROWS, 256), BF16)],
        compiler_params=_cp("arbitrary", "arbitrary", "arbitrary"),
        name="diff_attention",
    )(z, z, z, cos, sin, qn, kn, lam_vecs, subln, _seg_ones(128, HEAD))


def rope_tables():
    t = np.arange(SEQ)
    quarter = HEAD // 4
    inv = (1.0 / (ROPE_BASE ** (np.arange(quarter, dtype=np.float32) / quarter))).astype(np.float32)
    row = (t // GRID_W).astype(np.float32)[:, None] * inv
    col = (t % GRID_W).astype(np.float32)[:, None] * inv
    ang = np.concatenate([row, col], axis=-1)
    cos = np.concatenate([np.ones((CTX, HEAD // 2), np.float32), np.cos(ang)], axis=0)
    sin = np.concatenate([np.zeros((CTX, HEAD // 2), np.float32), np.sin(ang)], axis=0)
    cos = np.tile(cos, (1, 4))
    sin = np.tile(np.concatenate([-sin, sin], axis=1), (1, 2))
    return jnp.asarray(cos, F32), jnp.asarray(sin, F32)


def _rpb_body(r_ref, e_ref, o_ref):
    o_ref[...] = _dot_x(r_ref[...] * math.log2(math.e), e_ref[...])


def rpb_tables(rpb):
    depth = rpb.shape[0]
    nr, nc = 2 * NA_ROWS - 1, 2 * NA_COLS - 1
    q = np.arange(GRID_W)
    dc = np.clip(q[None, :] - q[:, None] + NA_COLS - 1, 0, nc - 1)
    onehot = (np.arange(32)[:, None, None] == dc[None]).astype(np.float32).reshape(32, GRID_W * GRID_W)
    r2 = jnp.pad(rpb.reshape(depth * C_HEADS * nr, nc), ((0, 0), (0, 32 - nc)))
    rows = r2.shape[0]
    toe = pl.pallas_call(
        _rpb_body,
        grid=(1,),
        in_specs=[pl.BlockSpec((rows, 32), lambda i: (0, 0)), pl.BlockSpec((32, GRID_W * GRID_W), lambda i: (0, 0))],
        out_specs=pl.BlockSpec((rows, GRID_W * GRID_W), lambda i: (0, 0)),
        out_shape=jax.ShapeDtypeStruct((rows, GRID_W * GRID_W), F32),
        compiler_params=_cp("arbitrary"),
        name="rpb_tables",
    )(r2, jnp.asarray(onehot, BF16))
    toe = toe.reshape(depth, C_HEADS, nr, GRID_W, GRID_W)
    rows = SEQ // GRID_W
    dr = np.zeros((3, NA_QROWS, NA_UNION), np.int32)
    ok_row = np.zeros((3, NA_QROWS, NA_UNION), bool)
    for ty, g in enumerate((0, 1, rows // NA_QROWS - 1)):
        us = _na_union_start(g, rows)
        for i in range(NA_QROWS):
            r = NA_QROWS * g + i
            rs = min(max(r - NA_ROWS // 2, 0), rows - NA_ROWS)
            for j in range(NA_UNION):
                ok_row[ty, i, j] = rs <= us + j < rs + NA_ROWS
                dr[ty, i, j] = min(max(us + j - r + NA_ROWS - 1, 0), nr - 1)
    cstart = np.clip(q - NA_COLS // 2, 0, GRID_W - NA_COLS)
    ok_col = (q[None, :] >= cstart[:, None]) & (q[None, :] < cstart[:, None] + NA_COLS)
    ok = ok_row[:, :, :, None, None] & ok_col[None, None, None]
    tab = toe[:, :, jnp.asarray(dr)]
    tab = jnp.where(jnp.asarray(ok), tab, -jnp.inf)
    tab = jnp.transpose(tab, (0, 2, 1, 3, 5, 4, 6))
    return tab.reshape(depth, 3, C_HEADS, NA_QROWS * GRID_W, NA_UNION * GRID_W)


NA_QROWS = 4
NA_UNION = NA_QROWS + NA_ROWS - 1


def _na_union_start(g, rows):
    lo = NA_QROWS * g - NA_ROWS // 2
    hi = rows - NA_UNION
    if isinstance(g, int):
        return min(max(lo, 0), hi)
    return jnp.clip(lo, 0, hi)


def _na_body(q_ref, k_ref, v_ref, bias_ref, qn_ref, kn_ref, seg_ref, o_ref, kh_ref, vh_ref):
    step = pl.program_id(1)
    seg = seg_ref[...]

    @pl.when(step == 0)
    def _():
        kh_ref[...] = _head_rms(k_ref[...], kn_ref[...], seg).astype(BF16)
        vh_ref[...] = v_ref[...].astype(BF16)

    nq = NA_QROWS * GRID_W
    q = _head_rms(q_ref[...], qn_ref[...], seg) * (HEAD ** -0.5 * math.log2(math.e))
    lane = _iota((nq, 128), 1)
    low = lane < HEAD
    sls = [slice((h // 2) * 128, (h // 2 + 1) * 128) for h in range(C_HEADS)]

    def masked_q(h):
        return jnp.where(low if h % 2 == 0 else jnp.logical_not(low), q[:, sls[h]], 0.0).astype(BF16)

    def store(hs, outs):
        for j in range(0, len(hs), 2):
            o_ref[:, sls[hs[j]]] = jnp.where(low, outs[j], outs[j + 1]).astype(o_ref.dtype)

    head_groups = [list(range(g, g + 4)) for g in range(0, C_HEADS, 4)]

    @pl.when(step == 0)
    def _():
        for hs in head_groups:
            qs = [masked_q(h) for h in hs]
            s = [_dot_nt(qs[j], kh_ref[0:CTX, sls[h]]) for j, h in enumerate(hs)]
            e = [jnp.exp2(x - jnp.max(x, axis=-1, keepdims=True)) for x in s]
            store(hs, [_dot(e[j].astype(BF16), vh_ref[0:CTX, sls[h]]) / jnp.sum(e[j], axis=-1, keepdims=True)
                       for j, h in enumerate(hs)])

    @pl.when(step > 0)
    def _():
        us = _na_union_start(step - 1, SEQ // GRID_W)
        k0 = pl.multiple_of(CTX + us * GRID_W, GRID_W)
        nloc = NA_UNION * GRID_W
        for hs in head_groups:
            qs = [masked_q(h) for h in hs]
            s_loc = [_dot_nt(qs[j], kh_ref[pl.ds(k0, nloc), sls[h]]) + bias_ref[0, h] for j, h in enumerate(hs)]
            s_ctx = [_dot_nt(qs[j], kh_ref[0:CTX, sls[h]]) for j, h in enumerate(hs)]
            mx = [jnp.maximum(jnp.max(a, axis=-1, keepdims=True), jnp.max(b, axis=-1, keepdims=True))
                  for a, b in zip(s_loc, s_ctx)]
            e_loc = [jnp.exp2(a - m_) for a, m_ in zip(s_loc, mx)]
            e_ctx = [jnp.exp2(b - m_) for b, m_ in zip(s_ctx, mx)]
            den = [jnp.sum(a, axis=-1, keepdims=True) + jnp.sum(b, axis=-1, keepdims=True)
                   for a, b in zip(e_loc, e_ctx)]
            store(hs, [(_dot(e_loc[j].astype(BF16), vh_ref[pl.ds(k0, nloc), sls[h]])
                        + _dot(e_ctx[j].astype(BF16), vh_ref[0:CTX, sls[h]])) / den[j] for j, h in enumerate(hs)])


def na_attention(z, bias, qn, kn, layer):
    t = z.shape[0]
    nb = t // ROWS
    nq = NA_QROWS * GRID_W
    assert nq == CTX
    steps = ROWS // nq
    groups = steps - 1

    def layout(i):
        g = jnp.maximum(i - 1, 0)
        return jnp.where(g == 0, 0, jnp.where(g == groups - 1, 2, 1))

    full = lambda shape: pl.BlockSpec(shape, lambda b, i: (0,) * len(shape))
    return pl.pallas_call(
        _na_body,
        grid=(nb, steps),
        in_specs=[pl.BlockSpec((nq, BLK), lambda b, i: (b * steps + i, 7)),
                  pl.BlockSpec((ROWS, BLK), lambda b, i: (b, 8)),
                  pl.BlockSpec((ROWS, BLK), lambda b, i: (b, 9)),
                  pl.BlockSpec((None, 1, C_HEADS, nq, NA_UNION * GRID_W), lambda b, i: (layer, layout(i), 0, 0, 0)),
                  full((1, BLK)), full((1, BLK)), full((BLK, BLK))],
        out_specs=pl.BlockSpec((nq, BLK), lambda b, i: (b * steps + i, 0)),
        out_shape=jax.ShapeDtypeStruct((t, BLK), BF16),
        scratch_shapes=[pltpu.VMEM((ROWS, BLK), BF16), pltpu.VMEM((ROWS, BLK), BF16)],
        compiler_params=_cp("arbitrary", "arbitrary"),
        name="na_attention",
    )(z, z, z, bias, qn, kn, _seg_ones(512, HEAD))


def _lru_body(x_ref, xp_ref, xn_ref, gate_ref, cw_ref, cb_ref, wa_ref, ba_ref, wx_ref, bx_ref, lam_ref,
              *rest, reverse, last, tpb):
    if last:
        h0_ref, o_ref, carry_ref = rest
    else:
        o_ref, carry_ref = rest
    p = pl.program_id(0) % tpb
    has_prev, has_next = _tile_flags(_lru_tile(p, tpb, reverse), tpb)

    @pl.when(p == 0)
    def _():
        carry_ref[...] = jnp.zeros_like(carry_ref)

    x = x_ref[...]
    n = x.shape[0]
    xm1 = _shift_rows(x, xp_ref[7:8, :], 1, has_prev)
    xp1 = _shift_rows(x, xn_ref[0:1, :], -1, has_next)
    rows = _iota((n, 1), 0)
    nx = jnp.where(has_next, xn_ref[0:2, :], 0.0)
    xp2 = pltpu.roll(x, n - 2, 0)
    xp2 = jnp.where(rows == n - 2, nx[0:1, :], jnp.where(rows == n - 1, nx[1:2, :], xp2))
    xr = (cw_ref[0:1, :] * xm1 + cw_ref[1:2, :] * x + cw_ref[2:3, :] * xp1 + cw_ref[3:4, :] * xp2) + cb_ref[...]
    xb = xr.astype(BF16)
    rg = _sigmoid(_dot(xb, wa_ref[...]) + ba_ref[...])
    ig = _sigmoid(_dot(xb, wx_ref[...]) + bx_ref[...])
    log_a = -RG_C * _softplus(-lam_ref[...]) * rg
    a = jnp.exp(log_a)
    b = jnp.sqrt(jnp.maximum(1.0 - jnp.exp(2.0 * log_a), 0.0)) * ig * xr
    sub = rows % 8
    k = 1
    while k < 8:
        if reverse:
            keep = sub < 8 - k
            a_s = pltpu.roll(a, n - k, 0)
            b_s = pltpu.roll(b, n - k, 0)
        else:
            keep = sub >= k
            a_s = pltpu.roll(a, k, 0)
            b_s = pltpu.roll(b, k, 0)
        b = jnp.where(keep, a * b_s + b, b)
        a = jnp.where(keep, a * a_s, a)
        k *= 2
    carry = carry_ref[...]
    order = range(n // 8 - 1, -1, -1) if reverse else range(n // 8)
    hs = [None] * (n // 8)
    for gi in order:
        hg = a[8 * gi:8 * gi + 8, :] * carry + b[8 * gi:8 * gi + 8, :]
        carry = hg[0:1, :] if reverse else hg[7:8, :]
        hs[gi] = hg
    h = jnp.concatenate(hs, axis=0)
    carry_ref[...] = carry
    if last:
        g = gate_ref[...]
        gelu = 0.5 * g * (1.0 + jnp.tanh(math.sqrt(2.0 / math.pi) * (g + 0.044715 * g * g * g)))
        o_ref[...] = ((h0_ref[...] + h) * gelu).astype(o_ref.dtype)
    else:
        o_ref[...] = h


def _lru_tile(p, tpb, reverse):
    cctx = CTX // TQ
    if reverse:
        return jnp.where(p < cctx, cctx - 1 - p, tpb - 1 + cctx - p)
    return p


def rglru(z, direction, conv_w, conv_b, wa_bd, ba, wx_bd, bx, lam, h0=None):
    t = z.shape[0]
    n_tiles = t // TQ
    tpb = ROWS // TQ
    reverse = direction == 1
    last = h0 is not None

    def blk(i):
        return (i // tpb) * tpb + _lru_tile(i % tpb, tpb, reverse)

    r8 = TQ // 8
    full = lambda shape: pl.BlockSpec(shape, lambda i: (0,) * len(shape))
    in_specs = [pl.BlockSpec((TQ, BLK), lambda i: (blk(i), 10)),
                pl.BlockSpec((8, BLK), lambda i: (jnp.maximum(blk(i) * r8 - 1, 0), 10)),
                pl.BlockSpec((8, BLK), lambda i: (jnp.minimum((blk(i) + 1) * r8, n_tiles * r8 - 1), 10)),
                pl.BlockSpec((TQ, BLK), lambda i: (blk(i), 11)),
                full((D_CONV, BLK)), full((1, BLK)), full((BLK, BLK)), full((1, BLK)),
                full((BLK, BLK)), full((1, BLK)), full((1, BLK))]
    args = [z, z, z, z, conv_w, conv_b, wa_bd, ba, wx_bd, bx, lam]
    if last:
        in_specs.append(pl.BlockSpec((TQ, BLK), lambda i: (blk(i), 0)))
        args.append(h0)

    return pl.pallas_call(
        functools.partial(_lru_body, reverse=reverse, last=last, tpb=tpb),
        grid=(n_tiles,),
        in_specs=in_specs,
        out_specs=pl.BlockSpec((TQ, BLK), lambda i: (blk(i), 0)),
        out_shape=jax.ShapeDtypeStruct((t, BLK), BF16 if last else F32),
        scratch_shapes=[pltpu.VMEM((1, BLK), F32)],
        compiler_params=_cp("arbitrary"),
        name="rglru_%d" % direction,
    )(*args)


def _merge_body(hn_ref, wg_ref, ba_ref, bb_ref, bc_ref, bd_ref, wb_ref, y_ref, *, tn):
    g = _dot(hn_ref[...], wg_ref[...])
    acc = None
    for j, br in enumerate((ba_ref, bb_ref, bc_ref, bd_ref)):
        term = _sigmoid(g[:, j * tn:(j + 1) * tn]) * _dot(br[...].astype(BF16), wb_ref[j])
        acc = term if acc is None else acc + term
    y_ref[...] = acc.astype(BF16)


def merge_branches(hn, wg, branches, wb, layer, *, tm=1152, tn=256):
    t, d = hn.shape
    br_spec = pl.BlockSpec((tm, BLK), lambda i, j: (i, 0))
    return pl.pallas_call(
        functools.partial(_merge_body, tn=tn),
        grid=(t // tm, d // tn),
        in_specs=[pl.BlockSpec((tm, d), lambda i, j: (i, 0)),
                  pl.BlockSpec((None, None, d, N_BRANCH * tn), lambda i, j: (layer, j, 0, 0)),
                  br_spec, br_spec, br_spec, br_spec,
                  pl.BlockSpec((None, N_BRANCH, BLK, tn), lambda i, j: (layer, 0, 0, j))],
        out_specs=pl.BlockSpec((tm, tn), lambda i, j: (i, j)),
        out_shape=jax.ShapeDtypeStruct((t, d), BF16),
        compiler_params=_cp("arbitrary", "arbitrary"),
        name="merge_branches",
    )(hn, wg, *branches, wb)


def _post_body(y_ref, wo_ref, x_ref, mod_ref, g2_ref, rw_ref, rb_ref,
               xo_ref, h2_ref, e_ref, w_ref, rank_ref, cnt_ref, base_ref, *, tm, tpb, slab):
    i = pl.program_id(0)
    row0 = (i % tpb) * tm

    @pl.when(i == 0)
    def _():
        base_ref[...] = jnp.zeros_like(base_ref)

    slabs = [slice(r, r + slab) for r in range(0, tm, slab)]
    attn = [_dot(y_ref[sl, :], wo_ref[...]) for sl in slabs]
    logit_parts = []
    for sl, a in zip(slabs, attn):
        x = x_ref[sl, :] + _mod_rows(mod_ref, 2, row0 + sl.start, slab) * a
        xo_ref[sl, :] = x
        h = _norm_mod(x, g2_ref[...], mod_ref, 3, row0 + sl.start, slab)
        hh, hl = _split(h)
        half = h.shape[1] // 2
        lo = pltpu.bitcast(hh[:, :half].astype(F32), jnp.uint32) >> 16
        hi = pltpu.bitcast(hh[:, half:].astype(F32), jnp.uint32) & jnp.uint32(0xFFFF0000)
        h2_ref[sl, :] = lo | hi
        logit_parts.append(_dot_nt(rw_ref[0], hh) + _dot_nt(rw_ref[0], hl) + _dot_nt(rw_ref[1], hh))
    logits = jnp.concatenate(logit_parts, axis=1)
    s = _sigmoid(logits)
    sb = s + rb_ref[...]
    per = N_EXPERTS // N_GROUPS
    srow = [s[e:e + 1, :] for e in range(N_EXPERTS)]
    brow = [sb[e:e + 1, :] for e in range(N_EXPERTS)]
    gscore = []
    for g in range(N_GROUPS):
        best = None
        for a in range(per):
            for b in range(a + 1, per):
                pair = brow[per * g + a] + brow[per * g + b]
                best = pair if best is None else jnp.maximum(best, pair)
        gscore.append(best)
    gbest = jnp.zeros((1, tm), jnp.int32)
    bscore = gscore[0]
    for g in range(1, N_GROUPS):
        better = gscore[g] > bscore
        gbest = jnp.where(better, g, gbest)
        bscore = jnp.where(better, gscore[g], bscore)

    def pick(rows, i):
        out = rows[i]
        for g in range(1, N_GROUPS):
            out = jnp.where(gbest == g, rows[per * g + i], out)
        return out

    cand = [pick(brow, i) for i in range(per)]
    cval = [pick(srow, i) for i in range(per)]
    i1 = jnp.zeros((1, tm), jnp.int32)
    m1 = cand[0]
    w1 = cval[0]
    for i_ in range(1, per):
        better = cand[i_] > m1
        i1 = jnp.where(better, i_, i1)
        m1 = jnp.where(better, cand[i_], m1)
        w1 = jnp.where(better, cval[i_], w1)
    i2 = jnp.full((1, tm), -1, jnp.int32)
    m2 = jnp.full((1, tm), -jnp.inf, F32)
    w2 = jnp.zeros((1, tm), F32)
    for i_ in range(per):
        better = jnp.logical_and(i1 != i_, cand[i_] > m2)
        i2 = jnp.where(better, i_, i2)
        m2 = jnp.where(better, cand[i_], m2)
        w2 = jnp.where(better, cval[i_], w2)
    e1 = per * gbest + i1
    e2 = per * gbest + i2
    wsum = w1 + w2
    e_ref[0:1, :] = e1
    e_ref[1:2, :] = e2
    w_ref[0:1, :] = w1 / wsum
    w_ref[1:2, :] = w2 / wsum
    er = _iota((N_EXPERTS, tm), 0)
    oh1 = jnp.where(er == e1, 1.0, 0.0)
    oh2 = jnp.where(er == e2, 1.0, 0.0)
    upper = jnp.where(_iota((tm, tm), 0) < _iota((tm, tm), 1), 1.0, 0.0).astype(BF16)
    ex1 = _dot(oh1.astype(BF16), upper)
    ex2 = _dot(oh2.astype(BF16), upper)
    tot1 = jnp.sum(oh1, axis=1, keepdims=True)
    tot2 = jnp.sum(oh2, axis=1, keepdims=True)
    base = base_ref[...]
    rank_ref[0:1, :] = jnp.sum(oh1 * (base + ex1), axis=0, keepdims=True).astype(jnp.int32)
    rank_ref[1:2, :] = jnp.sum(oh2 * (base + tot1 + ex2), axis=0, keepdims=True).astype(jnp.int32)
    base = base + tot1 + tot2
    base_ref[...] = base
    cnt_ref[...] = jnp.broadcast_to(base, cnt_ref.shape)


def post_attention(y, w_out, x, mod, g2, rw, rb, layer, *, tm=384):
    t, d = x.shape
    tpb = ROWS // tm
    full = lambda shape: pl.BlockSpec(shape, lambda i: (0,) * len(shape))
    row = pl.BlockSpec((tm, d), lambda i: (i, 0))
    tok = pl.BlockSpec((2, tm), lambda i: (0, i))
    return pl.pallas_call(
        functools.partial(_post_body, tm=tm, tpb=tpb, slab=128),
        grid=(t // tm,),
        in_specs=[row, pl.BlockSpec((None, d, d), lambda i: (layer, 0, 0)), row,
                  pl.BlockSpec((1, 2, 6, d), lambda i: (i // tpb, 0, 0, 0)),
                  full((1, d)), full((2, N_EXPERTS, d)), full((N_EXPERTS, 1))],
        out_specs=[row, pl.BlockSpec((tm, d // 2), lambda i: (i, 0)), tok, tok, tok, full((N_EXPERTS, 128))],
        out_shape=[jax.ShapeDtypeStruct((t, d), F32), jax.ShapeDtypeStruct((t, d // 2), jnp.uint32),
                   jax.ShapeDtypeStruct((2, t), jnp.int32), jax.ShapeDtypeStruct((2, t), F32),
                   jax.ShapeDtypeStruct((2, t), jnp.int32), jax.ShapeDtypeStruct((N_EXPERTS, 128), F32)],
        scratch_shapes=[pltpu.VMEM((N_EXPERTS, 1), F32)],
        compiler_params=_cp("arbitrary"),
        name="post_attention",
    )(y, w_out, x, mod, g2, rw, rb)


def _row_copy(src, dst, sem):
    return pltpu.make_async_copy(src, dst, sem)


def _dispatch_body(pos_ref, h_ref, xs_in_ref, xs_ref, sem, *, tm):
    del xs_in_ref

    def start(t, carry):
        for k in range(2):
            p = pos_ref[0, 0, k * tm + t]
            _row_copy(h_ref.at[pl.ds(t, 1)], xs_ref.at[pl.ds(p, 1)], sem).start(priority=k)
        return carry

    lax.fori_loop(0, tm, start, 0, unroll=8)
    for k in range(2):
        _row_copy(h_ref, xs_ref.at[pl.ds(0, tm)], sem).wait()


def moe_dispatch(pos3, h2, n_rows, init=None, *, tm=256):
    t, d = h2.shape
    xs0 = jnp.zeros((n_rows, d), h2.dtype) if init is None else init
    return pl.pallas_call(
        functools.partial(_dispatch_body, tm=tm),
        grid=(t // tm,),
        in_specs=[pl.BlockSpec((1, 1, 2 * tm), lambda i: (i, 0, 0), memory_space=pltpu.SMEM),
                  pl.BlockSpec((tm, d), lambda i: (i, 0)),
                  pl.BlockSpec(memory_space=pl.ANY)],
        out_specs=pl.BlockSpec(memory_space=pl.ANY),
        out_shape=jax.ShapeDtypeStruct((n_rows, d), h2.dtype),
        scratch_shapes=[pltpu.SemaphoreType.DMA(())],
        input_output_aliases={2: 0},
        compiler_params=_cp("arbitrary"),
        name="moe_dispatch",
    )(pos3, h2, xs0)


def _ffn_body(te_ref, nv_ref, x_ref, wg_ref, wu_ref, wd_ref, y_ref):
    del te_ref
    i = pl.program_id(0)

    @pl.when(i < nv_ref[0])
    def _():
        p = x_ref[...]
        half = p.shape[1]
        x_lo = pltpu.bitcast(p << 16, F32).astype(BF16)
        x_hi = pltpu.bitcast(p & jnp.uint32(0xFFFF0000), F32).astype(BF16)
        g = _dot(x_lo, wg_ref[0:half, :]) + _dot(x_hi, wg_ref[half:2 * half, :])
        u = _dot(x_lo, wu_ref[0:half, :]) + _dot(x_hi, wu_ref[half:2 * half, :])
        y_ref[...] = _dot((g * _sigmoid(g) * u).astype(BF16), wd_ref[...])

    @pl.when(i >= nv_ref[0])
    def _():
        y_ref[...] = jnp.zeros_like(y_ref)


def moe_ffn(tile_expert, n_valid, xs, wg, wu, wd, layer):
    n_rows, dp = xs.shape
    d, ff = wg.shape[-2:]
    n_tiles = n_rows // FFN_TM
    grid_spec = pltpu.PrefetchScalarGridSpec(
        num_scalar_prefetch=2,
        grid=(n_tiles,),
        in_specs=[pl.BlockSpec((FFN_TM, dp), lambda i, te, nv: (jnp.minimum(i, nv[0] - 1), 0)),
                  pl.BlockSpec((None, None, d, ff), lambda i, te, nv: (layer, te[i], 0, 0)),
                  pl.BlockSpec((None, None, d, ff), lambda i, te, nv: (layer, te[i], 0, 0)),
                  pl.BlockSpec((None, None, ff, d), lambda i, te, nv: (layer, te[i], 0, 0))],
        out_specs=pl.BlockSpec((FFN_TM, d), lambda i, te, nv: (i, 0)))
    return pl.pallas_call(
        _ffn_body,
        grid_spec=grid_spec,
        out_shape=jax.ShapeDtypeStruct((n_rows, d), F32),
        compiler_params=_cp("arbitrary"),
        name="moe_ffn",
    )(tile_expert, n_valid, xs, wg, wu, wd)


def _combine_body(pos_ref, ys_ref, x_ref, w_ref, mod_ref, o_ref, buf_ref, sem, *, tm, tpb):
    def start(t, carry):
        for k in range(2):
            p = pos_ref[0, 0, k * tm + t]
            _row_copy(ys_ref.at[pl.ds(p, 1)], buf_ref.at[k, pl.ds(t, 1)], sem).start(priority=k)
        return carry

    lax.fori_loop(0, tm, start, 0, unroll=8)
    for k in range(2):
        _row_copy(ys_ref.at[pl.ds(0, tm)], buf_ref.at[k], sem).wait()
    w = w_ref[...]
    y = w[:, 0:1] * buf_ref[0] + w[:, 1:2] * buf_ref[1]
    row0 = (pl.program_id(0) % tpb) * tm
    o_ref[...] = x_ref[...] + _mod_rows(mod_ref, 5, row0, tm) * y


def moe_combine(pos3, ys, x, wt, mod, *, tm=256):
    t, d = x.shape
    tpb = ROWS // tm
    row = pl.BlockSpec((tm, d), lambda i: (i, 0))
    return pl.pallas_call(
        functools.partial(_combine_body, tm=tm, tpb=tpb),
        grid=(t // tm,),
        in_specs=[pl.BlockSpec((1, 1, 2 * tm), lambda i: (i, 0, 0), memory_space=pltpu.SMEM),
                  pl.BlockSpec(memory_space=pl.ANY), row,
                  pl.BlockSpec((tm, 2), lambda i: (i, 0)),
                  pl.BlockSpec((1, 2, 6, d), lambda i: (i // tpb, 0, 0, 0))],
        out_specs=row,
        out_shape=jax.ShapeDtypeStruct((t, d), F32),
        scratch_shapes=[pltpu.VMEM((2, tm, d), F32), pltpu.SemaphoreType.DMA(())],
        compiler_params=_cp("arbitrary"),
        name="moe_combine",
    )(pos3, ys, x, wt, mod)


def moe_layout(e, rank, counts, *, tm=256):
    t = e.shape[1]
    n_rows_max = -(-(2 * t + N_EXPERTS * (FFN_TM - 1)) // FFN_TM) * FFN_TM
    n_tiles = n_rows_max // FFN_TM
    cnt = counts[:, 0].astype(jnp.int32)
    gsz = ((cnt + FFN_TM - 1) // FFN_TM) * FFN_TM
    end = jnp.cumsum(gsz)
    off = end - gsz
    onehot = e[:, :, None] == jnp.arange(N_EXPERTS, dtype=jnp.int32)
    pos = rank + jnp.sum(jnp.where(onehot, off, 0), axis=-1)
    pos3 = pos.reshape(2, t // tm, tm).transpose(1, 0, 2).reshape(t // tm, 1, 2 * tm)
    n_valid = (end[-1] // FFN_TM).astype(jnp.int32)
    starts = jnp.minimum(jnp.arange(n_tiles, dtype=jnp.int32), n_valid - 1) * FFN_TM
    tile_expert = jnp.sum(starts[:, None] >= end[None, :], axis=1).astype(jnp.int32)
    return pos3, tile_expert, n_valid.reshape(1), n_rows_max


def _block_diag(w):
    n, bi, bj = w.shape[-3:]
    eye = jnp.eye(n, dtype=w.dtype)
    out = w[..., :, :, None, :] * eye[:, None, :, None]
    return out.reshape(w.shape[:-3] + (n * bi, n * bj))


def kernel(x, c, ctx, c_ctx, w_mod, b_mod, norm1_g, norm2_g, w_in, a_mu, a_w0, a_w2, a_a0, a_a2, a_g2, a_kk, a_ka,
           a_rk, a_gn_g, a_gn_b, b_qn, b_kn, b_lam, b_subln, c_qn, c_kn, c_rpb, d_conv_w, d_conv_b, d_wa, d_ba,
           d_wx, d_bx, d_lam, w_branch, w_out, router_w, router_b, e_gate, e_up, e_down):
    nb = x.shape[0]
    d = D_MODEL
    t = nb * ROWS
    xs = jnp.concatenate([ctx, x], axis=1).reshape(t, d)

    cc = jnp.concatenate([c, c_ctx[None], jnp.zeros((-(nb + 1) % 8, d), F32)], axis=0)
    mods = mod_vectors(cc, w_mod, b_mod)
    mod_l = mods[:, :nb].reshape(DEPTH, nb, 1, 6, d)
    mod_c = jnp.broadcast_to(mods[:, nb].reshape(DEPTH, 1, 1, 6, d), (DEPTH, nb, 1, 6, d))
    mod = jnp.concatenate([mod_l, mod_c], axis=2)

    o_b, o_c, o_d, o_g = A_COLS, A_COLS + 1536, A_COLS + 3072, A_COLS + 3072 + 1024
    w_mix = jnp.concatenate([w_in[:, :, :o_b], jnp.zeros((DEPTH, d, 2048 - A_COLS), F32), w_in[:, :, o_b:o_g]],
                            axis=-1).astype(BF16)
    tn_m = 256
    w_gate = w_in[:, :, o_g:].astype(BF16).reshape(DEPTH, d, N_BRANCH, d // tn_m, tn_m)
    w_gate = jnp.transpose(w_gate, (0, 3, 1, 2, 4)).reshape(DEPTH, d // tn_m, d, N_BRANCH * tn_m)
    mu = jnp.pad(a_mu, ((0, 0), (0, 0), (0, 2048 - A_COLS)))
    w2bd = _block_diag(a_w2).astype(BF16)
    a2bd = _block_diag(a_a2).astype(BF16)
    g2 = a_g2.astype(BF16)
    wa_bd = _block_diag(d_wa).astype(BF16)
    wx_bd = _block_diag(d_wx).astype(BF16)
    wb = w_branch.astype(BF16)
    wo = w_out.astype(BF16)
    rw = jnp.stack(_split(router_w.T), axis=0)
    rb = router_b.reshape(N_EXPERTS, 1)
    eg, eu, ed = e_gate.astype(BF16), e_up.astype(BF16), e_down.astype(BF16)
    cos, sin = rope_tables()
    bias = rpb_tables(c_rpb)
    qn_c = jnp.tile(c_qn, (1, C_HEADS)).reshape(DEPTH, 1, BLK)
    kn_c = jnp.tile(c_kn, (1, C_HEADS)).reshape(DEPTH, 1, BLK)

    xg = None
    for l in range(DEPTH):
        lam_init = 0.8 - 0.6 * math.exp(-0.3 * l)
        z, hn = in_projection(xs, mod[l], norm1_g[l].reshape(1, d), w_mix, l)
        prep = rwkv_prepare(z, mu[l], w2bd[l], a2bd[l], g2[l], a_w0[l].reshape(1, 1024), a_a0[l].reshape(1, 1024),
                            a_kk[l].reshape(1, BLK), a_ka[l].reshape(1, BLK), a_rk[l].reshape(1, BLK))
        ya = rwkv_scan(prep, 0)
        oa = rwkv_scan(prep, 1, ya, a_gn_g[l].reshape(1, BLK), a_gn_b[l].reshape(1, BLK))
        ob = diff_attention(z, cos, sin, b_qn[l].reshape(1, 128), b_kn[l].reshape(1, 128), b_lam[l],
                            b_subln[l].reshape(1, 128), lam_init)
        oc = na_attention(z, bias, qn_c[l], kn_c[l], l)
        lru = [(d_conv_w[l], d_conv_b[l].reshape(1, BLK), wa_bd[l, k], d_ba[l, k].reshape(1, BLK), wx_bd[l, k],
                d_bx[l, k].reshape(1, BLK), d_lam[l, k].reshape(1, BLK)) for k in range(2)]
        hd = rglru(z, 0, *lru[0])
        od = rglru(z, 1, *lru[1], h0=hd)
        y = merge_branches(hn, w_gate, (oa, ob, oc, od), wb, l, tn=tn_m)
        xs, h2, e, wts, rank, counts = post_attention(y, wo, xs, mod[l], norm2_g[l].reshape(1, d), rw, rb, l)
        pos3, tile_expert, n_valid, n_rows = moe_layout(e, rank, counts)
        xg = moe_dispatch(pos3, h2, n_rows, xg)
        yg = moe_ffn(tile_expert, n_valid, xg, eg, eu, ed, l)
        xs = moe_combine(pos3, yg, xs, wts.T, mod[l])
    return xs.reshape(nb, ROWS, d)[:, CTX:]
```
